```python
import jax, jax.numpy as jnp
from jax import lax
import numpy as np

D_MODEL = 2048
BATCH = 1
SEQ = 16384
DEPTH = 4
DEC_BATCH = 32
DEC_SEQ = 32
PAST_LEN = 4096

CHUNK = 64
N_META = 16
N_MIXERS = 3
N_GDN = (DEPTH + 2) // 3
N_FOX = (DEPTH + 1) // 3
N_RWKV = DEPTH // 3
NORM_EPS = 1e-6

GDN_DK = 128
GDN_DV = 128
GDN_QK_HEADS = D_MODEL // 128
GDN_V_HEADS = 2 * GDN_QK_HEADS
GDN_CONV = 4
GDN_QK_DIM = GDN_QK_HEADS * GDN_DK
GDN_V_DIM = GDN_V_HEADS * GDN_DV
GDN_CONV_DIM = 2 * GDN_QK_DIM + GDN_V_DIM
GDN_IN_DIM = GDN_CONV_DIM + GDN_V_DIM + 2 * GDN_V_HEADS

FOX_DH = 128
FOX_HEADS = D_MODEL // FOX_DH
FOX_DIM = FOX_HEADS * FOX_DH
FOX_IN_DIM = 4 * FOX_DIM + FOX_HEADS
FOX_QBLOCK = 128

RWKV_HEAD = 64
RWKV_HEADS = D_MODEL // RWKV_HEAD
RWKV_DECAY_LORA = 96
RWKV_A_LORA = 96
RWKV_GN_EPS = 64e-5

kernel_name = 'hybrid_streaming_gdn_fox_rwkv7_step'


def rms_norm(x, g, eps=NORM_EPS):
    xf = x.astype(jnp.float32)
    return xf * lax.rsqrt(jnp.mean(xf * xf, axis=-1, keepdims=True) + eps) * g.astype(jnp.float32)


def l2norm(x, eps=1e-6):
    return x * lax.rsqrt(jnp.sum(x * x, axis=-1, keepdims=True) + eps)


def causal_conv(u, buf, w):
    L = u.shape[1]
    full = jnp.concatenate([buf.astype(u.dtype), u], axis=1)
    out = full[:, 0:L] * w[0]
    for i in range(1, GDN_CONV):
        out = out + full[:, i:i + L] * w[i]
    return out, full[:, L:]


def gdn_chunk(S, q, k, v, g, beta):
    L = q.shape[2]
    G = jnp.cumsum(g, axis=-1)
    causal = jnp.tril(jnp.ones((L, L), dtype=bool))
    strict = jnp.tril(jnp.ones((L, L), dtype=bool), -1)
    decay = jnp.exp(jnp.where(causal, G[..., :, None] - G[..., None, :], -jnp.inf))
    A = jnp.where(strict, jnp.einsum('bhtd,bhsd->bhts', k, k) * decay * beta[..., :, None], 0.0)
    A = A + jnp.eye(L, dtype=A.dtype)
    rhs = jnp.concatenate([v * beta[..., None], k * (beta * jnp.exp(G))[..., None]], axis=-1)
    sol = lax.linalg.triangular_solve(A, rhs, left_side=True, lower=True, unit_diagonal=True)
    u_v, w_k = sol[..., :GDN_DV], sol[..., GDN_DV:]
    v_new = u_v - jnp.einsum('bhtk,bhkv->bhtv', w_k, S)
    qk = jnp.einsum('bhtd,bhsd->bhts', q, k) * decay
    o = (jnp.einsum('bhtk,bhkv->bhtv', q * jnp.exp(G)[..., None], S)
         + jnp.einsum('bhts,bhsv->bhtv', qk, v_new))
    G_last = G[..., -1]
    S_new = (S * jnp.exp(G_last)[..., None, None]
             + jnp.einsum('bhsk,bhsv->bhkv', k * jnp.exp(G_last[..., None] - G)[..., None], v_new))
    return S_new, o


def gdn_sequence(S0, q, k, v, g, beta, lead):
    S, o_lead = gdn_chunk(S0, q[:, :, :lead], k[:, :, :lead], v[:, :, :lead], g[:, :, :lead], beta[:, :, :lead])
    rest = q.shape[2] - lead
    if rest == 0:
        return S, o_lead
    n = rest // CHUNK

    def blocks(x):
        x = x[:, :, lead:]
        return jnp.moveaxis(x.reshape(x.shape[:2] + (n, CHUNK) + x.shape[3:]), 2, 0)

    S, o_rest = lax.scan(lambda s, xs: gdn_chunk(s, *xs), S,
                         (blocks(q), blocks(k), blocks(v), blocks(g), blocks(beta)))
    B, H = q.shape[:2]
    o_rest = jnp.moveaxis(o_rest, 0, 2).reshape(B, H, rest, GDN_DV)
    return S, jnp.concatenate([o_lead, o_rest], axis=2)


def gdn_branch(h, conv_buf, s0, w_in, conv_w, a_log, dt_bias, o_norm, w_out, lead):
    B, T, _ = h.shape
    f32 = jnp.float32
    proj = h @ w_in
    o0, o1, o2 = GDN_CONV_DIM, GDN_CONV_DIM + GDN_V_DIM, GDN_CONV_DIM + GDN_V_DIM + GDN_V_HEADS
    z = proj[..., o0:o1].astype(f32)
    b_gate = proj[..., o1:o2].astype(f32)
    a_gate = proj[..., o2:].astype(f32)
    qkv, new_buf = causal_conv(proj[..., :o0], conv_buf, conv_w)
    qkv = jax.nn.silu(qkv.astype(f32))
    rep = GDN_V_HEADS // GDN_QK_HEADS
    q = l2norm(qkv[..., :GDN_QK_DIM].reshape(B, T, GDN_QK_HEADS, GDN_DK)) * GDN_DK ** -0.5
    k = l2norm(qkv[..., GDN_QK_DIM:2 * GDN_QK_DIM].reshape(B, T, GDN_QK_HEADS, GDN_DK))
    v = qkv[..., 2 * GDN_QK_DIM:].reshape(B, T, GDN_V_HEADS, GDN_DV)
    q = jnp.repeat(q, rep, axis=2).transpose(0, 2, 1, 3)
    k = jnp.repeat(k, rep, axis=2).transpose(0, 2, 1, 3)
    v = v.transpose(0, 2, 1, 3)
    beta = jax.nn.sigmoid(b_gate).transpose(0, 2, 1)
    g = (-jnp.exp(a_log.astype(f32)) * jax.nn.softplus(a_gate + dt_bias.astype(f32))).transpose(0, 2, 1)
    s_new, o = gdn_sequence(s0.astype(f32), q, k, v, g, beta, lead)
    o = rms_norm(o.transpose(0, 2, 1, 3), o_norm) * jax.nn.silu(z.reshape(B, T, GDN_V_HEADS, GDN_DV))
    y = o.reshape(B, T, GDN_V_DIM).astype(h.dtype) @ w_out
    return y, new_buf, s_new


def fox_project(h, w_in, b_f):
    B, T, _ = h.shape
    proj = h @ w_in
    q = proj[..., :FOX_DIM].reshape(B, T, FOX_HEADS, FOX_DH)
    k = proj[..., FOX_DIM:2 * FOX_DIM].reshape(B, T, FOX_HEADS, FOX_DH)
    v = proj[..., 2 * FOX_DIM:3 * FOX_DIM].reshape(B, T, FOX_HEADS, FOX_DH)
    z = proj[..., 3 * FOX_DIM:4 * FOX_DIM]
    logf = jax.nn.log_sigmoid(proj[..., 4 * FOX_DIM:].astype(jnp.float32) + b_f.astype(jnp.float32))
    return q, k, v, z, logf


def fox_output(o, z, w_out):
    B, T = o.shape[:2]
    o = o.reshape(B, T, FOX_DIM).astype(jnp.float32) * jax.nn.silu(z.astype(jnp.float32))
    return o.astype(z.dtype) @ w_out


def fox_attend(q, k, v, cq, ck, q_pos, k_pos):
    s = jnp.einsum('bqhd,bkhd->bhqk', q, k, preferred_element_type=jnp.float32) * FOX_DH ** -0.5
    s = s + cq.transpose(0, 2, 1)[..., :, None] - ck.transpose(0, 2, 1)[..., None, :]
    s = jnp.where(k_pos[None, :] <= q_pos[:, None], s, -jnp.inf)
    p = jax.nn.softmax(s, axis=-1)
    return jnp.einsum('bhqk,bkhd->bqhd', p.astype(v.dtype), v)


def fox_prompt(h, w_in, b_f, w_out):
    q, k, v, z, logf = fox_project(h, w_in, b_f)
    B, T = h.shape[:2]
    Tp = -(-T // FOX_QBLOCK) * FOX_QBLOCK
    nb = Tp // FOX_QBLOCK
    pad4 = ((0, 0), (0, Tp - T), (0, 0), (0, 0))
    qp, kp, vp = jnp.pad(q, pad4), jnp.pad(k, pad4), jnp.pad(v, pad4)
    c = jnp.cumsum(jnp.pad(logf, ((0, 0), (0, Tp - T), (0, 0))), axis=1)
    pos = jnp.arange(Tp)
    q_blocks = jnp.moveaxis(qp.reshape(B, nb, FOX_QBLOCK, FOX_HEADS, FOX_DH), 1, 0)
    c_blocks = jnp.moveaxis(c.reshape(B, nb, FOX_QBLOCK, FOX_HEADS), 1, 0)
    p_blocks = pos.reshape(nb, FOX_QBLOCK)
    o = lax.map(lambda xs: fox_attend(xs[0], kp, vp, xs[1], c, xs[2], pos), (q_blocks, c_blocks, p_blocks))
    o = jnp.moveaxis(o, 0, 1).reshape(B, Tp, FOX_HEADS, FOX_DH)[:, :T]
    return fox_output(o, z, w_out), k, v, logf


def fox_sample(h, cache_k, cache_v, cache_logf, w_in, b_f, w_out):
    q, k, v, z, logf = fox_project(h, w_in, b_f)
    P, T = cache_k.shape[1], h.shape[1]
    c = jnp.cumsum(jnp.concatenate([cache_logf.astype(jnp.float32), logf], axis=1), axis=1)
    cq = c[:, P:].transpose(0, 2, 1)[..., :, None]
    scale = FOX_DH ** -0.5
    s_c = jnp.einsum('bqhd,bkhd->bhqk', q, cache_k.astype(q.dtype), preferred_element_type=jnp.float32) * scale
    s_c = s_c + cq - c[:, :P].transpose(0, 2, 1)[..., None, :]
    s_n = jnp.einsum('bqhd,bkhd->bhqk', q, k, preferred_element_type=jnp.float32) * scale
    s_n = s_n + cq - c[:, P:].transpose(0, 2, 1)[..., None, :]
    s_n = jnp.where(jnp.tril(jnp.ones((T, T), dtype=bool)), s_n, -jnp.inf)
    p = jax.nn.softmax(jnp.concatenate([s_c, s_n], axis=-1), axis=-1)
    o = (jnp.einsum('bhqk,bkhd->bqhd', p[..., :P].astype(v.dtype), cache_v.astype(v.dtype))
         + jnp.einsum('bhqk,bkhd->bqhd', p[..., P:].astype(v.dtype), v))
    return fox_output(o, z, w_out), k, v, logf


def rwkv_branch(h, shift, s0, mu, w_r, w_k, w_v, w_g, w0, w1, w2, a0, a1, a2, k_k, k_a, r_k, gn_w, gn_b, w_out):
    B, T, D = h.shape
    f32 = jnp.float32
    prev = jnp.concatenate([shift[:, None, :].astype(h.dtype), h[:, :-1]], axis=1)
    xx = prev - h
    xr, xw, xk, xv, xa, xg = [h + xx * mu[i] for i in range(6)]
    r = (xr @ w_r).astype(f32)
    k = (xk @ w_k).astype(f32)
    v = (xv @ w_v).astype(f32)
    gate = jax.nn.silu((xg @ w_g).astype(f32))
    w = -jax.nn.softplus(-(w0 + jnp.tanh(xw @ w1) @ w2).astype(f32)) - 0.5
    decay = jnp.exp(-jnp.exp(w))
    a = jax.nn.sigmoid((a0 + (xa @ a1) @ a2).astype(f32))
    heads = lambda t: t.reshape(B, T, RWKV_HEADS, RWKV_HEAD)
    kk = l2norm(heads(k * k_k.astype(f32)))
    k = k * (1.0 + (a - 1.0) * k_a.astype(f32))
    r_h, k_h, v_h = heads(r), heads(k), heads(v)
    tm = lambda t: jnp.moveaxis(t, 1, 0)

    def step(S, xs):
        r_t, w_t, k_t, v_t, a_t, b_t = xs
        sa = jnp.einsum('bhvk,bhk->bhv', S, a_t)
        S = S * w_t[:, :, None, :] + sa[..., None] * b_t[:, :, None, :] + v_t[..., None] * k_t[:, :, None, :]
        return S, jnp.einsum('bhvk,bhk->bhv', S, r_t)

    s_new, o = lax.scan(step, s0.astype(f32),
                        (tm(r_h), tm(heads(decay)), tm(k_h), tm(v_h), tm(-kk), tm(kk * heads(a))))
    o = jnp.moveaxis(o, 0, 1)
    mean = jnp.mean(o, axis=-1, keepdims=True)
    var = jnp.mean(jnp.square(o - mean), axis=-1, keepdims=True)
    o = ((o - mean) * lax.rsqrt(var + RWKV_GN_EPS)).reshape(B, T, D) * gn_w.astype(f32) + gn_b.astype(f32)
    bonus = jnp.sum(r_h * k_h * r_k.astype(f32), axis=-1, keepdims=True) * v_h
    o = o + bonus.reshape(B, T, D)
    y = (o * gate).astype(h.dtype) @ w_out
    return y, h[:, -1], s_new


def setup_inputs(seed: int = 0) -> dict:
    key = jax.random.key(seed)
    ks = iter(jax.random.split(key, 48))
    f32 = jnp.float32
    D = D_MODEL

    def nrm(shape, scale):
        return jax.random.normal(next(ks), shape, f32) * scale

    def unif(shape, lo, hi):
        return jax.random.uniform(next(ks), shape, f32, lo, hi)

    dt = jnp.exp(unif((N_GDN, GDN_V_HEADS), float(np.log(1e-3)), float(np.log(1e-1))))
    inp = {}
    inp['x_prompt'] = nrm((BATCH, SEQ, D), 1.0)
    inp['x_sample'] = nrm((DEC_BATCH, DEC_SEQ, D), 1.0)
    inp['state_gdn_S'] = nrm((N_GDN, DEC_BATCH, GDN_V_HEADS, GDN_DK, GDN_DV), 0.1)
    inp['state_gdn_conv'] = nrm((N_GDN, DEC_BATCH, GDN_CONV - 1, GDN_CONV_DIM), 1.0)
    inp['cache_fox_k'] = nrm((N_FOX, DEC_BATCH, PAST_LEN, FOX_HEADS, FOX_DH), 1.0)
    inp['cache_fox_v'] = nrm((N_FOX, DEC_BATCH, PAST_LEN, FOX_HEADS, FOX_DH), 1.0)
    inp['cache_fox_logf'] = jax.nn.log_sigmoid(3.0 + nrm((N_FOX, DEC_BATCH, PAST_LEN, FOX_HEADS), 1.0))
    inp['state_rwkv_S'] = nrm((N_RWKV, DEC_BATCH, RWKV_HEADS, RWKV_HEAD, RWKV_HEAD), 0.1)
    inp['state_rwkv_shift'] = nrm((N_RWKV, DEC_BATCH, D), 1.0)
    inp['meta'] = nrm((N_META, D), 1.0)
    inp['norm_pre'] = 1.0 + nrm((DEPTH, D), 0.05)
    inp['norm_post'] = 1.0 + nrm((DEPTH, D), 0.05)
    inp['gdn_w_in'] = nrm((N_GDN, D, GDN_IN_DIM), D ** -0.5)
    inp['gdn_conv_w'] = nrm((N_GDN, GDN_CONV, GDN_CONV_DIM), 0.5)
    inp['gdn_a_log'] = jnp.log(unif((N_GDN, GDN_V_HEADS), 1.0, 16.0))
    inp['gdn_dt_bias'] = dt + jnp.log(-jnp.expm1(-dt))
    inp['gdn_o_norm'] = 1.0 + nrm((N_GDN, GDN_DV), 0.05)
    inp['gdn_w_out'] = nrm((N_GDN, GDN_V_DIM, D), GDN_V_DIM ** -0.5)
    inp['fox_w_in'] = nrm((N_FOX, D, FOX_IN_DIM), D ** -0.5)
    inp['fox_b_f'] = 3.0 + nrm((N_FOX, FOX_HEADS), 0.5)
    inp['fox_w_out'] = nrm((N_FOX, FOX_DIM, D), FOX_DIM ** -0.5)
    inp['rwkv_mu'] = unif((N_RWKV, 6, D), 0.0, 1.0)
    inp['rwkv_w_r'] = nrm((N_RWKV, D, D), D ** -0.5)
    inp['rwkv_w_k'] = nrm((N_RWKV, D, D), D ** -0.5)
    inp['rwkv_w_v'] = nrm((N_RWKV, D, D), D ** -0.5)
    inp['rwkv_w_g'] = nrm((N_RWKV, D, D), D ** -0.5)
    inp['rwkv_w0'] = unif((N_RWKV, D), -6.0, -1.0)
    inp['rwkv_w1'] = nrm((N_RWKV, D, RWKV_DECAY_LORA), D ** -0.5)
    inp['rwkv_w2'] = nrm((N_RWKV, RWKV_DECAY_LORA, D), 0.5 * RWKV_DECAY_LORA ** -0.5)
    inp['rwkv_a0'] = nrm((N_RWKV, D), 0.1)
    inp['rwkv_a1'] = nrm((N_RWKV, D, RWKV_A_LORA), D ** -0.5)
    inp['rwkv_a2'] = nrm((N_RWKV, RWKV_A_LORA, D), 0.5 * RWKV_A_LORA ** -0.5)
    inp['rwkv_k_k'] = 0.85 + nrm((N_RWKV, D), 0.05)
    inp['rwkv_k_a'] = 1.0 + nrm((N_RWKV, D), 0.05)
    inp['rwkv_r_k'] = nrm((N_RWKV, RWKV_HEADS, RWKV_HEAD), 0.1)
    inp['rwkv_gn_w'] = 1.0 + nrm((N_RWKV, D), 0.05)
    inp['rwkv_gn_b'] = nrm((N_RWKV, D), 0.01)
    inp['rwkv_w_out'] = nrm((N_RWKV, D, D), D ** -0.5)
    return inp


def reference(x_prompt, x_sample, state_gdn_S, state_gdn_conv, cache_fox_k, cache_fox_v, cache_fox_logf,
              state_rwkv_S, state_rwkv_shift, meta, norm_pre, norm_post,
              gdn_w_in, gdn_conv_w, gdn_a_log, gdn_dt_bias, gdn_o_norm, gdn_w_out,
              fox_w_in, fox_b_f, fox_w_out,
              rwkv_mu, rwkv_w_r, rwkv_w_k, rwkv_w_v, rwkv_w_g, rwkv_w0, rwkv_w1, rwkv_w2,
              rwkv_a0, rwkv_a1, rwkv_a2, rwkv_k_k, rwkv_k_a, rwkv_r_k, rwkv_gn_w, rwkv_gn_b, rwkv_w_out):
    B = x_prompt.shape[0]
    dt = x_prompt.dtype
    xp = jnp.concatenate([jnp.broadcast_to(meta[None].astype(dt), (B, N_META, D_MODEL)), x_prompt], axis=1)
    xs = x_sample
    p_gS, p_gc, p_fk, p_fv, p_fl, p_rS, p_rs = [], [], [], [], [], [], []
    s_gS, s_gc, s_fk, s_fv, s_fl, s_rS, s_rs = [], [], [], [], [], [], []
    for i in range(DEPTH):
        kind, j = i % N_MIXERS, i // N_MIXERS
        hp = rms_norm(xp, norm_pre[i]).astype(xp.dtype)
        hs = rms_norm(xs, norm_pre[i]).astype(xs.dtype)
        if kind == 0:
            w = (gdn_w_in[j], gdn_conv_w[j], gdn_a_log[j], gdn_dt_bias[j], gdn_o_norm[j], gdn_w_out[j])
            buf0 = jnp.zeros((B, GDN_CONV - 1, GDN_CONV_DIM), hp.dtype)
            S0 = jnp.zeros((B, GDN_V_HEADS, GDN_DK, GDN_DV), jnp.float32)
            yp, cp, Sp = gdn_branch(hp, buf0, S0, *w, lead=N_META)
            ys, cs, Ss = gdn_branch(hs, state_gdn_conv[j], state_gdn_S[j], *w, lead=hs.shape[1])
            p_gS.append(Sp); p_gc.append(cp); s_gS.append(Ss); s_gc.append(cs)
        elif kind == 1:
            yp, kp, vp, lp = fox_prompt(hp, fox_w_in[j], fox_b_f[j], fox_w_out[j])
            ys, ks_, vs_, ls_ = fox_sample(hs, cache_fox_k[j], cache_fox_v[j], cache_fox_logf[j],
                                           fox_w_in[j], fox_b_f[j], fox_w_out[j])
            p_fk.append(kp); p_fv.append(vp); p_fl.append(lp)
            s_fk.append(ks_); s_fv.append(vs_); s_fl.append(ls_)
        else:
            w = (rwkv_mu[j], rwkv_w_r[j], rwkv_w_k[j], rwkv_w_v[j], rwkv_w_g[j], rwkv_w0[j], rwkv_w1[j],
                 rwkv_w2[j], rwkv_a0[j], rwkv_a1[j], rwkv_a2[j], rwkv_k_k[j], rwkv_k_a[j], rwkv_r_k[j],
                 rwkv_gn_w[j], rwkv_gn_b[j], rwkv_w_out[j])
            shift0 = jnp.zeros((B, D_MODEL), hp.dtype)
            S0 = jnp.zeros((B, RWKV_HEADS, RWKV_HEAD, RWKV_HEAD), jnp.float32)
            yp, shp, Sp = rwkv_branch(hp, shift0, S0, *w)
            ys, shs, Ss = rwkv_branch(hs, state_rwkv_shift[j], state_rwkv_S[j], *w)
            p_rS.append(Sp); p_rs.append(shp); s_rS.append(Ss); s_rs.append(shs)
        xp = xp + rms_norm(yp, norm_post[i]).astype(xp.dtype)
        xs = xs + rms_norm(ys, norm_post[i]).astype(xs.dtype)
    y_prompt = xp[:, N_META:]
    return (y_prompt, xs,
            jnp.stack(p_gS), jnp.stack(p_gc), jnp.stack(p_fk), jnp.stack(p_fv), jnp.stack(p_fl),
            jnp.stack(p_rS), jnp.stack(p_rs),
            jnp.stack(s_gS), jnp.stack(s_gc), jnp.stack(s_fk), jnp.stack(s_fv), jnp.stack(s_fl),
            jnp.stack(s_rS), jnp.stack(s_rs))
```

```python
import functools

import jax
import jax.numpy as jnp
from jax import lax
from jax.experimental import pallas as pl
from jax.experimental.pallas import tpu as pltpu

F32 = jnp.float32
BF16 = jnp.bfloat16

NORM_EPS = 1e-6
L2_EPS = 1e-6
RWKV_GN_EPS = 64e-5
GDN_HEAD = 128
GDN_CONV = 4
FOX_HEAD = 128
RWKV_HEAD = 64
LANES = 128
SUBLANES = 8
CONV_PAD = 8
INV_BASE = 16
VMEM_LIMIT = 48 * 1024 * 1024


def _cparams(sem):
    return pltpu.CompilerParams(dimension_semantics=sem, vmem_limit_bytes=VMEM_LIMIT)


def _mm(a, b):
    return jnp.dot(a.astype(BF16), b.astype(BF16), preferred_element_type=F32)


def _mm_nt(a, b):
    return lax.dot_general(a.astype(BF16), b.astype(BF16), (((1,), (1,)), ((), ())),
                           preferred_element_type=F32)


def _mm_tn(a, b):
    return lax.dot_general(a.astype(BF16), b.astype(BF16), (((0,), (0,)), ((), ())),
                           preferred_element_type=F32)


def _split2(a):
    hi = a.astype(BF16)
    lo = (a - hi.astype(F32)).astype(BF16)
    return hi, lo


def _mm3(a, b):
    ah, al = _split2(a)
    bh, bl = _split2(b)
    d = functools.partial(jnp.dot, preferred_element_type=F32)
    return d(ah, bh) + (d(ah, bl) + d(al, bh))


def _split3(a):
    hi = a.astype(BF16)
    r = a - hi.astype(F32)
    mid = r.astype(BF16)
    lo = (r - mid.astype(F32)).astype(BF16)
    return hi, mid, lo


def _cumsum_rows(tril, g):
    hi, mid, lo = _split3(g)
    d = functools.partial(jnp.dot, preferred_element_type=F32)
    return d(tril, hi) + (d(tril, mid) + d(tril, lo))


def _cumsum_cols(g, triu):
    hi, mid, lo = _split3(g)
    d = functools.partial(jnp.dot, preferred_element_type=F32)
    return d(hi, triu) + (d(mid, triu) + d(lo, triu))


def _tri_inv(a, row, col):
    L = a.shape[0]
    eye = jnp.where(row == col, 1.0, 0.0).astype(F32)
    diff = row ^ col
    same = diff < INV_BASE
    n1 = jnp.where(same, -a, 0.0)
    p = eye + n1
    npow = n1
    steps = 1
    while 2 * steps < INV_BASE:
        npow = _mm3(npow, npow)
        p = p + _mm3(p, npow)
        steps *= 2
    blk = INV_BASE
    while blk < L:
        e = jnp.where((diff < 2 * blk) & (diff >= blk), a, 0.0)
        p = p - _mm3(_mm3(p, e), p)
        blk *= 2
    return p


def _silu(x):
    return x * jax.nn.sigmoid(x)


def _norm_proj_kernel(x_ref, g_ref, w_ref, o_ref, h_scr):
    @pl.when(pl.program_id(1) == 0)
    def _():
        x = x_ref[...]
        ms = jnp.mean(x * x, axis=-1, keepdims=True)
        h_scr[...] = (x * lax.rsqrt(ms + NORM_EPS) * g_ref[...]).astype(BF16)

    o_ref[...] = jnp.dot(h_scr[...], w_ref[...], preferred_element_type=F32)


def norm_proj(x, g, w, tm=512, tn=1024):
    T, D = x.shape
    N = w.shape[1]
    tn = min(tn, N)
    assert N % tn == 0
    return pl.pallas_call(
        _norm_proj_kernel,
        grid=(pl.cdiv(T, tm), N // tn),
        in_specs=[pl.BlockSpec((tm, D), lambda i, j: (i, 0)),
                  pl.BlockSpec((1, D), lambda i, j: (0, 0)),
                  pl.BlockSpec((D, tn), lambda i, j: (0, j))],
        out_specs=pl.BlockSpec((tm, tn), lambda i, j: (i, j)),
        out_shape=jax.ShapeDtypeStruct((T, N), F32),
        scratch_shapes=[pltpu.VMEM((tm, D), BF16)],
        compiler_params=_cparams(("parallel", "arbitrary")),
        name="norm_proj",
    )(x, g.reshape(1, D), w)


def _norm_kernel(x_ref, g_ref, o_ref):
    x = x_ref[...]
    ms = jnp.mean(x * x, axis=-1, keepdims=True)
    o_ref[...] = x * lax.rsqrt(ms + NORM_EPS) * g_ref[...]


def norm_only(x, g, tm=512):
    T, D = x.shape
    return pl.pallas_call(
        _norm_kernel,
        grid=(pl.cdiv(T, tm),),
        in_specs=[pl.BlockSpec((tm, D), lambda i: (i, 0)), pl.BlockSpec((1, D), lambda i: (0, 0))],
        out_specs=pl.BlockSpec((tm, D), lambda i: (i, 0)),
        out_shape=jax.ShapeDtypeStruct((T, D), F32),
        compiler_params=_cparams(("parallel",)),
        name="norm_only",
    )(x, g.reshape(1, D))


def _mix_proj_kernel(h_ref, p_ref, mu_ref, w_ref, o_ref, l_scr):
    @pl.when(pl.program_id(2) == 0)
    def _():
        h = h_ref[...]
        l_scr[...] = (h + (p_ref[...] - h) * mu_ref[0]).astype(BF16)

    o_ref[...] = jnp.dot(l_scr[...], w_ref[0], preferred_element_type=F32)


def mix_proj(h, prev, mu, w, tm=512, tn=1024):
    T, D = h.shape
    G, _, N = w.shape
    tn = min(tn, N)
    nj = N // tn
    return pl.pallas_call(
        _mix_proj_kernel,
        grid=(pl.cdiv(T, tm), G, nj),
        in_specs=[pl.BlockSpec((tm, D), lambda i, g, j: (i, 0)),
                  pl.BlockSpec((tm, D), lambda i, g, j: (i, 0)),
                  pl.BlockSpec((1, 1, D), lambda i, g, j: (g, 0, 0)),
                  pl.BlockSpec((1, D, tn), lambda i, g, j: (g, 0, j))],
        out_specs=pl.BlockSpec((tm, tn), lambda i, g, j: (i, g * nj + j)),
        out_shape=jax.ShapeDtypeStruct((T, G * N), F32),
        scratch_shapes=[pltpu.VMEM((tm, D), BF16)],
        compiler_params=_cparams(("parallel", "arbitrary", "arbitrary")),
        name="mix_proj",
    )(h, prev, mu.reshape(G, 1, D), w)


def _lora_kernel(use_tanh, x_ref, w_ref, b_ref, o_ref):
    x = x_ref[...]
    if use_tanh:
        x = jnp.tanh(x)
    o_ref[...] = b_ref[...] + jnp.dot(x.astype(BF16), w_ref[...], preferred_element_type=F32)


def lora_out(x, col_block, w, b, use_tanh, tm=512):
    T = x.shape[0]
    K, N = w.shape
    return pl.pallas_call(
        functools.partial(_lora_kernel, use_tanh),
        grid=(pl.cdiv(T, tm),),
        in_specs=[pl.BlockSpec((tm, K), lambda i: (i, col_block)),
                  pl.BlockSpec((K, N), lambda i: (0, 0)),
                  pl.BlockSpec((1, N), lambda i: (0, 0))],
        out_specs=pl.BlockSpec((tm, N), lambda i: (i, 0)),
        out_shape=jax.ShapeDtypeStruct((T, N), F32),
        compiler_params=_cparams(("parallel",)),
        name="lora_out",
    )(x, w, b.reshape(1, N))


def _out_proj_kernel(tn, a_ref, w_ref, x_ref, g_ref, o_ref, y_scr):
    j = pl.program_id(1)
    off = pl.multiple_of(j * tn, tn)
    y_scr[:, pl.ds(off, tn)] = jnp.dot(a_ref[...], w_ref[...], preferred_element_type=F32)

    @pl.when(j == pl.num_programs(1) - 1)
    def _():
        y = y_scr[...]
        ms = jnp.mean(y * y, axis=-1, keepdims=True)
        o_ref[...] = x_ref[...] + y * lax.rsqrt(ms + NORM_EPS) * g_ref[...]


def out_proj(a, w, x, g, tm=512, tn=512):
    T, K = a.shape
    D = w.shape[1]
    return pl.pallas_call(
        functools.partial(_out_proj_kernel, tn),
        grid=(pl.cdiv(T, tm), D // tn),
        in_specs=[pl.BlockSpec((tm, K), lambda i, j: (i, 0)),
                  pl.BlockSpec((K, tn), lambda i, j: (0, j)),
                  pl.BlockSpec((tm, D), lambda i, j: (i, 0)),
                  pl.BlockSpec((1, D), lambda i, j: (0, 0))],
        out_specs=pl.BlockSpec((tm, D), lambda i, j: (i, 0)),
        out_shape=jax.ShapeDtypeStruct((T, D), F32),
        scratch_shapes=[pltpu.VMEM((tm, D), F32)],
        compiler_params=_cparams(("parallel", "arbitrary")),
        name="out_proj",
    )(a, w, x, g.reshape(1, D))


GDN_VH = 4


def _gdn_kernel(L, q_ref, k_ref, v_ref, z_ref, cq_ref, ck_ref, cv_ref, wq_ref, wk_ref, wv_ref,
                gc_ref, gr_ref, bc_ref, s0_ref, on_ref,
                o_ref, so_ref, cqo_ref, cko_ref, cvo_ref,
                s_scr, qbuf, kbuf, vbuf):
    c = pl.program_id(2)
    HD = GDN_HEAD

    @pl.when(c == 0)
    def _():
        s_scr[...] = s0_ref[0]
        qbuf[CONV_PAD - 3:CONV_PAD, :] = cq_ref[0]
        kbuf[CONV_PAD - 3:CONV_PAD, :] = ck_ref[0]
        vbuf[CONV_PAD - 3:CONV_PAD, :] = cv_ref[0]

    def conv_silu(buf, u_ref, w_ref, tail_ref):
        u = u_ref[...]
        buf[CONV_PAD:CONV_PAD + L, :] = u
        acc = buf[CONV_PAD - 3:CONV_PAD - 3 + L, :] * w_ref[0:1, :]
        acc = acc + buf[CONV_PAD - 2:CONV_PAD - 2 + L, :] * w_ref[1:2, :]
        acc = acc + buf[CONV_PAD - 1:CONV_PAD - 1 + L, :] * w_ref[2:3, :]
        acc = acc + u * w_ref[3:4, :]
        tail = buf[CONV_PAD + L - 3:CONV_PAD + L, :]
        buf[CONV_PAD - 3:CONV_PAD, :] = tail
        tail_ref[0] = tail
        return _silu(acc)

    q = conv_silu(qbuf, q_ref, wq_ref, cqo_ref)
    k = conv_silu(kbuf, k_ref, wk_ref, cko_ref)
    v = conv_silu(vbuf, v_ref, wv_ref, cvo_ref)

    row = lax.broadcasted_iota(jnp.int32, (L, L), 0)
    col = lax.broadcasted_iota(jnp.int32, (L, L), 1)
    causal = col <= row
    strict = col < row
    tril = jnp.where(causal, 1.0, 0.0).astype(BF16)
    triu = jnp.where(row <= col, 1.0, 0.0).astype(BF16)

    gcol = _cumsum_rows(tril, gc_ref[0])
    grow = _cumsum_cols(gr_ref[0, 0], triu)
    bcol = bc_ref[0]
    glast = gcol[L - 1:L, :]
    eg = jnp.exp(gcol)
    etail = jnp.exp(glast - gcol)
    eglast = jnp.exp(glast)

    for hq in range(GDN_VH // 2):
        qh = q[:, hq * HD:(hq + 1) * HD]
        kh = k[:, hq * HD:(hq + 1) * HD]
        qh = qh * lax.rsqrt(jnp.sum(qh * qh, axis=-1, keepdims=True) + L2_EPS) * (HD ** -0.5)
        kh = kh * lax.rsqrt(jnp.sum(kh * kh, axis=-1, keepdims=True) + L2_EPS)
        kk = _mm_nt(kh, kh)
        qk = _mm_nt(qh, kh)
        for jv in range(2):
            j = 2 * hq + jv
            vj = v[:, j * HD:(j + 1) * HD]
            bc = bcol[:, j:j + 1]
            dmat = gcol[:, j:j + 1] - grow[j:j + 1, :]
            decay = jnp.where(causal, jnp.exp(jnp.where(causal, dmat, 0.0)), 0.0)
            a = jnp.where(strict, kk * decay * bc, 0.0)
            tinv = _tri_inv(a, row, col)
            rhs = jnp.concatenate([vj * bc, kh * (bc * eg[:, j:j + 1])], axis=1)
            sol = _mm(tinv, rhs)
            u_v = sol[:, :HD]
            w_k = sol[:, HD:]
            s = s_scr[j]
            v_new = u_v - _mm(w_k, s)
            o = _mm(qh * eg[:, j:j + 1], s) + _mm(qk * decay, v_new)
            s_scr[j] = s * eglast[:, j:j + 1] + _mm_tn(kh * etail[:, j:j + 1], v_new)
            zj = z_ref[:, j * HD:(j + 1) * HD]
            ms = jnp.mean(o * o, axis=-1, keepdims=True)
            on = o * lax.rsqrt(ms + NORM_EPS) * on_ref[...]
            o_ref[:, j * HD:(j + 1) * HD] = (on * _silu(zj)).astype(BF16)

    @pl.when(c == pl.num_programs(2) - 1)
    def _():
        so_ref[0] = s_scr[...]


def gdn_mix(proj, gate_g, gate_b, conv_state, s0, conv_w, o_norm, row0, B, NC, L):
    HV = s0.shape[1]
    HD = GDN_HEAD
    HG = HV // GDN_VH
    QW = (GDN_VH // 2) * HD
    VW = GDN_VH * HD
    nq = (HV // 2) * HD // QW
    Tseg = B * NC * L
    rb0 = row0 // L
    assert row0 % L == 0

    lane_pad = lambda t: jnp.pad(t, ((0, 0), (0, 0), (0, LANES - GDN_VH)))
    gc = lane_pad(gate_g.reshape(Tseg, HG, GDN_VH).transpose(1, 0, 2))
    bc = lane_pad(gate_b.reshape(Tseg, HG, GDN_VH).transpose(1, 0, 2))
    gr = gate_g.reshape(B * NC, L, HG, GDN_VH).transpose(2, 0, 3, 1)
    gr = jnp.pad(gr, ((0, 0), (0, 0), (0, SUBLANES - GDN_VH), (0, 0)))
    cq = conv_state[:, :, :nq * QW]
    ck = conv_state[:, :, nq * QW:2 * nq * QW]
    cv = conv_state[:, :, 2 * nq * QW:]
    wq = conv_w[:, :nq * QW]
    wk = conv_w[:, nq * QW:2 * nq * QW]
    wv = conv_w[:, 2 * nq * QW:]

    rowmap = lambda b, h, c: rb0 + b * NC + c
    in_specs = [
        pl.BlockSpec((L, QW), lambda b, h, c: (rowmap(b, h, c), h)),
        pl.BlockSpec((L, QW), lambda b, h, c: (rowmap(b, h, c), nq + h)),
        pl.BlockSpec((L, VW), lambda b, h, c: (rowmap(b, h, c), HG + h)),
        pl.BlockSpec((L, VW), lambda b, h, c: (rowmap(b, h, c), 2 * HG + h)),
        pl.BlockSpec((1, GDN_CONV - 1, QW), lambda b, h, c: (b, 0, h)),
        pl.BlockSpec((1, GDN_CONV - 1, QW), lambda b, h, c: (b, 0, h)),
        pl.BlockSpec((1, GDN_CONV - 1, VW), lambda b, h, c: (b, 0, h)),
        pl.BlockSpec((GDN_CONV, QW), lambda b, h, c: (0, h)),
        pl.BlockSpec((GDN_CONV, QW), lambda b, h, c: (0, h)),
        pl.BlockSpec((GDN_CONV, VW), lambda b, h, c: (0, h)),
        pl.BlockSpec((1, L, LANES), lambda b, h, c: (h, b * NC + c, 0)),
        pl.BlockSpec((1, 1, SUBLANES, L), lambda b, h, c: (h, b * NC + c, 0, 0)),
        pl.BlockSpec((1, L, LANES), lambda b, h, c: (h, b * NC + c, 0)),
        pl.BlockSpec((1, GDN_VH, HD, HD), lambda b, h, c: (b, h, 0, 0)),
        pl.BlockSpec((1, HD), lambda b, h, c: (0, 0)),
    ]
    out_specs = [
        pl.BlockSpec((L, VW), lambda b, h, c: (b * NC + c, h)),
        pl.BlockSpec((1, GDN_VH, HD, HD), lambda b, h, c: (b, h, 0, 0)),
        pl.BlockSpec((1, GDN_CONV - 1, QW), lambda b, h, c: (b, 0, h)),
        pl.BlockSpec((1, GDN_CONV - 1, QW), lambda b, h, c: (b, 0, h)),
        pl.BlockSpec((1, GDN_CONV - 1, VW), lambda b, h, c: (b, 0, h)),
    ]
    out_shape = [
        jax.ShapeDtypeStruct((Tseg, HV * HD), BF16),
        jax.ShapeDtypeStruct((B, HV, HD, HD), F32),
        jax.ShapeDtypeStruct((B, GDN_CONV - 1, nq * QW), F32),
        jax.ShapeDtypeStruct((B, GDN_CONV - 1, nq * QW), F32),
        jax.ShapeDtypeStruct((B, GDN_CONV - 1, HV * HD), F32),
    ]
    o, s_new, cqo, cko, cvo = pl.pallas_call(
        functools.partial(_gdn_kernel, L),
        grid=(B, HG, NC),
        in_specs=in_specs,
        out_specs=out_specs,
        out_shape=out_shape,
        scratch_shapes=[pltpu.VMEM((GDN_VH, HD, HD), F32),
                        pltpu.VMEM((CONV_PAD + L, QW), F32),
                        pltpu.VMEM((CONV_PAD + L, QW), F32),
                        pltpu.VMEM((CONV_PAD + L, VW), F32)],
        compiler_params=_cparams(("parallel", "parallel", "arbitrary")),
        name="gdn_mix",
    )(proj, proj, proj, proj, cq, ck, cv, wq, wk, wv, gc, gr, bc, s0, o_norm.reshape(1, HD))
    return o, s_new, jnp.concatenate([cqo, cko, cvo], axis=-1)


def _fox_prompt_kernel(TQ, scale, q_ref, k_ref, v_ref, km_ref, vm_ref, z_ref, ck_ref, cm_ref, o_ref):
    qi = pl.program_id(1)
    q = q_ref[...].astype(BF16)

    s = _mm_nt(q, km_ref[...]) * scale - cm_ref[0]
    m = jnp.max(s, axis=-1, keepdims=True)
    p = jnp.exp(s - m)
    l = jnp.sum(p, axis=-1, keepdims=True)
    acc = _mm(p, vm_ref[...])

    def step(kb, carry, diagonal):
        m, l, acc = carry
        off = pl.multiple_of(kb * TQ, TQ)
        s = _mm_nt(q, k_ref[pl.ds(off, TQ), :]) * scale - ck_ref[0, :, pl.ds(off, TQ)]
        if diagonal:
            row = lax.broadcasted_iota(jnp.int32, (TQ, TQ), 0)
            col = lax.broadcasted_iota(jnp.int32, (TQ, TQ), 1)
            s = jnp.where(col <= row, s, -jnp.inf)
        m_new = jnp.maximum(m, jnp.max(s, axis=-1, keepdims=True))
        alpha = jnp.exp(m - m_new)
        p = jnp.exp(s - m_new)
        l = alpha * l + jnp.sum(p, axis=-1, keepdims=True)
        acc = alpha * acc + _mm(p, v_ref[pl.ds(off, TQ), :])
        return m_new, l, acc

    carry = lax.fori_loop(0, qi, lambda kb, cr: step(kb, cr, False), (m, l, acc))
    m, l, acc = step(qi, carry, True)
    o_ref[...] = ((acc / l) * _silu(z_ref[...])).astype(BF16)


def fox_prompt(proj, kb16, vb16, c_main, c_meta, T, meta_row0, n_meta, H, TQ=512):
    HD = FOX_HEAD
    assert T % TQ == 0 and meta_row0 % n_meta == 0
    mb = meta_row0 // n_meta
    return pl.pallas_call(
        functools.partial(_fox_prompt_kernel, TQ, HD ** -0.5),
        grid=(H, T // TQ),
        in_specs=[pl.BlockSpec((TQ, HD), lambda h, i: (i, h)),
                  pl.BlockSpec((T, HD), lambda h, i: (0, h)),
                  pl.BlockSpec((T, HD), lambda h, i: (0, h)),
                  pl.BlockSpec((n_meta, HD), lambda h, i: (mb, h)),
                  pl.BlockSpec((n_meta, HD), lambda h, i: (mb, h)),
                  pl.BlockSpec((TQ, HD), lambda h, i: (i, 3 * H + h)),
                  pl.BlockSpec((1, 1, T), lambda h, i: (h, 0, 0)),
                  pl.BlockSpec((1, 1, n_meta), lambda h, i: (h, 0, 0))],
        out_specs=pl.BlockSpec((TQ, HD), lambda h, i: (i, h)),
        out_shape=jax.ShapeDtypeStruct((T, H * HD), BF16),
        compiler_params=_cparams(("parallel", "arbitrary")),
        name="fox_prompt",
    )(proj, kb16, vb16, kb16, vb16, proj, c_main, c_meta)


def _fox_short_kernel(has_cache, scale, q_ref, k_ref, v_ref, z_ref, cn_ref, *rest):
    if has_cache:
        kc_ref, vc_ref, cc_ref, o_ref = rest
    else:
        (o_ref,) = rest
    TQ = q_ref.shape[0]
    q = q_ref[...].astype(BF16)
    row = lax.broadcasted_iota(jnp.int32, (TQ, TQ), 0)
    col = lax.broadcasted_iota(jnp.int32, (TQ, TQ), 1)
    s_n = _mm_nt(q, k_ref[...]) * scale - cn_ref[0, 0]
    s_n = jnp.where(col <= row, s_n, -jnp.inf)
    m = jnp.max(s_n, axis=-1, keepdims=True)
    if has_cache:
        kc = kc_ref[0].astype(BF16)
        s_c = _mm_nt(q, kc) * scale - cc_ref[0, 0]
        m = jnp.maximum(m, jnp.max(s_c, axis=-1, keepdims=True))
    p_n = jnp.exp(s_n - m)
    l = jnp.sum(p_n, axis=-1, keepdims=True)
    acc = _mm(p_n, v_ref[...])
    if has_cache:
        p_c = jnp.exp(s_c - m)
        l = l + jnp.sum(p_c, axis=-1, keepdims=True)
        acc = acc + _mm(p_c, vc_ref[0])
    o_ref[...] = ((acc / l) * _silu(z_ref[...])).astype(BF16)


def fox_short(proj, c_new, row0, B, TQ, H, cache=None):
    HD = FOX_HEAD
    assert row0 % TQ == 0
    rb0 = row0 // TQ
    in_specs = [pl.BlockSpec((TQ, HD), lambda b, h: (rb0 + b, h)),
                pl.BlockSpec((TQ, HD), lambda b, h: (rb0 + b, H + h)),
                pl.BlockSpec((TQ, HD), lambda b, h: (rb0 + b, 2 * H + h)),
                pl.BlockSpec((TQ, HD), lambda b, h: (rb0 + b, 3 * H + h)),
                pl.BlockSpec((1, 1, 1, TQ), lambda b, h: (b, h, 0, 0))]
    args = [proj, proj, proj, proj, c_new]
    if cache is not None:
        kc, vc, cc = cache
        P = kc.shape[1]
        in_specs += [pl.BlockSpec((1, P, HD), lambda b, h: (b, 0, h)),
                     pl.BlockSpec((1, P, HD), lambda b, h: (b, 0, h)),
                     pl.BlockSpec((1, 1, 1, P), lambda b, h: (b, h, 0, 0))]
        args += [kc, vc, cc]
    return pl.pallas_call(
        functools.partial(_fox_short_kernel, cache is not None, HD ** -0.5),
        grid=(B, H),
        in_specs=in_specs,
        out_specs=pl.BlockSpec((TQ, HD), lambda b, h: (b, h)),
        out_shape=jax.ShapeDtypeStruct((B * TQ, H * HD), BF16),
        compiler_params=_cparams(("parallel", "parallel")),
        name="fox_short",
    )(*args)


WKV_H = 4


def _wkv_kernel(L, r_ref, lw_ref, k_ref, v_ref, a_ref, b_ref, s0_ref, o_ref, so_ref, s_scr):
    c = pl.program_id(2)
    HD = RWKV_HEAD

    @pl.when(c == 0)
    def _():
        s_scr[...] = s0_ref[0]

    row = lax.broadcasted_iota(jnp.int32, (L, L), 0)
    col = lax.broadcasted_iota(jnp.int32, (L, L), 1)
    causal = col <= row
    strict = col < row
    tril = jnp.where(causal, 1.0, 0.0).astype(BF16)

    lw = lw_ref[...]
    cum = _cumsum_rows(tril, lw)
    clast = cum[L - 1:L, :]
    e_neg = jnp.exp(-cum)
    e_tail = jnp.exp(clast - cum)
    rt = r_ref[...] * jnp.exp(cum)
    at = a_ref[...] * jnp.exp(cum - lw)
    k = k_ref[...]
    b = b_ref[...]
    kt = k * e_neg
    bt = b * e_neg
    khat = k * e_tail
    bhat = b * e_tail
    glast = jnp.exp(clast)
    v = v_ref[...]

    for j in range(WKV_H):
        sl = slice(j * HD, (j + 1) * HD)
        a_ab = jnp.where(strict, _mm_nt(at[:, sl], bt[:, sl]), 0.0)
        a_ak = jnp.where(strict, _mm_nt(at[:, sl], kt[:, sl]), 0.0)
        a_rb = jnp.where(causal, _mm_nt(rt[:, sl], bt[:, sl]), 0.0)
        a_rk = jnp.where(causal, _mm_nt(rt[:, sl], kt[:, sl]), 0.0)
        tinv = _tri_inv(-a_ab, row, col)
        s = s_scr[j]
        vj = v[:, sl]
        u = _mm(tinv, _mm_nt(at[:, sl], s) + _mm(a_ak, vj))
        o_ref[:, sl] = _mm_nt(rt[:, sl], s) + _mm(a_rk, vj) + _mm(a_rb, u)
        s_scr[j] = s * glast[:, sl] + _mm_tn(vj, khat[:, sl]) + _mm_tn(u, bhat[:, sl])

    @pl.when(c == pl.num_programs(2) - 1)
    def _():
        so_ref[0] = s_scr[...]


def wkv_mix(r, lw, k, v, a, b, s0, row0, B, NC, L):
    HN = s0.shape[1]
    HD = RWKV_HEAD
    W = WKV_H * HD
    HG = HN // WKV_H
    Tseg = B * NC * L
    rb0 = row0 // L
    assert row0 % L == 0
    tok = pl.BlockSpec((L, W), lambda bb, h, c: (rb0 + bb * NC + c, h))
    o, s_new = pl.pallas_call(
        functools.partial(_wkv_kernel, L),
        grid=(B, HG, NC),
        in_specs=[tok, tok, tok, tok, tok, tok,
                  pl.BlockSpec((1, WKV_H, HD, HD), lambda bb, h, c: (bb, h, 0, 0))],
        out_specs=[pl.BlockSpec((L, W), lambda bb, h, c: (bb * NC + c, h)),
                   pl.BlockSpec((1, WKV_H, HD, HD), lambda bb, h, c: (bb, h, 0, 0))],
        out_shape=[jax.ShapeDtypeStruct((Tseg, HN * HD), F32),
                   jax.ShapeDtypeStruct((B, HN, HD, HD), F32)],
        scratch_shapes=[pltpu.VMEM((WKV_H, HD, HD), F32)],
        compiler_params=_cparams(("parallel", "parallel", "arbitrary")),
        name="wkv_mix",
    )(r, lw, k, v, a, b, s0)
    return o, s_new


def _pad_cols(w, n):
    return jnp.pad(w, ((0, 0), (0, n - w.shape[1])))


def _gdn_layer(X, seg, g_pre, g_post, w_in, conv_w, a_log, dt_bias, o_norm, w_out, st_S, st_conv):
    (T, BS, TS, NM) = seg
    HV = a_log.shape[0]
    CD = conv_w.shape[1]
    VD = HV * GDN_HEAD
    w_main = w_in[:, :CD + VD].astype(BF16)
    w_gate = _pad_cols(w_in[:, CD + VD:], LANES).astype(BF16)
    proj = norm_proj(X, g_pre, w_main)
    gates = norm_proj(X, g_pre, w_gate)
    beta = jax.nn.sigmoid(gates[:, :HV])
    g = -jnp.exp(a_log) * jax.nn.softplus(gates[:, HV:2 * HV] + dt_bias)

    r_s, r_m = T, T + BS * TS
    z_conv = jnp.zeros((1, GDN_CONV - 1, CD), F32)
    z_S = jnp.zeros((1, HV, GDN_HEAD, GDN_HEAD), F32)
    o_m, S_m, c_m = gdn_mix(proj, g[r_m:], beta[r_m:], z_conv, z_S, conv_w, o_norm, r_m, 1, 1, NM)
    LP = 64
    o_p, S_p, c_p = gdn_mix(proj, g[:T], beta[:T], c_m, S_m, conv_w, o_norm, 0, 1, T // LP, LP)
    o_s, S_s, c_s = gdn_mix(proj, g[r_s:r_m], beta[r_s:r_m], st_conv, st_S, conv_w, o_norm, r_s, BS, 1, TS)
    o = jnp.concatenate([o_p, o_s, o_m], axis=0)
    X = out_proj(o, w_out.astype(BF16), X, g_post)
    return X, (S_p, c_p, S_s, c_s)


def _fox_layer(X, seg, g_pre, g_post, w_in, b_f, w_out, cache_k, cache_v, cache_logf):
    (T, BS, TS, NM) = seg
    H = b_f.shape[0]
    FD = H * FOX_HEAD
    proj = norm_proj(X, g_pre, w_in[:, :4 * FD].astype(BF16))
    gates = norm_proj(X, g_pre, _pad_cols(w_in[:, 4 * FD:], LANES).astype(BF16))
    logf = jax.nn.log_sigmoid(gates[:, :H] + b_f)
    r_s, r_m = T, T + BS * TS

    c_p = jnp.cumsum(jnp.concatenate([logf[r_m:], logf[:T]], axis=0), axis=0)
    c_meta = c_p[:NM].T.reshape(H, 1, NM)
    c_main = c_p[NM:].T.reshape(H, 1, T)
    P = cache_k.shape[1]
    c_s = jnp.cumsum(jnp.concatenate([cache_logf, logf[r_s:r_m].reshape(BS, TS, H)], axis=1), axis=1)
    c_s = c_s.transpose(0, 2, 1)
    c_cache = c_s[:, :, :P].reshape(BS, H, 1, P)
    c_new = c_s[:, :, P:].reshape(BS, H, 1, TS)

    kb16 = proj[:, FD:2 * FD].astype(BF16)
    vb16 = proj[:, 2 * FD:3 * FD].astype(BF16)
    o_p = fox_prompt(proj, kb16, vb16, c_main, c_meta, T, r_m, NM, H)
    o_m = fox_short(proj, c_meta.reshape(1, H, 1, NM), r_m, 1, NM, H)
    o_s = fox_short(proj, c_new, r_s, BS, TS, H,
                    cache=(cache_k.reshape(BS, P, FD), cache_v.reshape(BS, P, FD), c_cache))
    o = jnp.concatenate([o_p, o_s, o_m], axis=0)
    X = out_proj(o, w_out.astype(BF16), X, g_post)

    def seq(a, n):
        return jnp.concatenate([a[r_m:], a[:T]], axis=0).reshape(1, NM + T, H, n)

    k_all = proj[:, FD:2 * FD]
    v_all = proj[:, 2 * FD:3 * FD]
    outs = (seq(k_all, FOX_HEAD), seq(v_all, FOX_HEAD),
            jnp.concatenate([logf[r_m:], logf[:T]], axis=0).reshape(1, NM + T, H),
            k_all[r_s:r_m].reshape(BS, TS, H, FOX_HEAD), v_all[r_s:r_m].reshape(BS, TS, H, FOX_HEAD),
            logf[r_s:r_m].reshape(BS, TS, H))
    return X, outs


def _rwkv_layer(X, seg, g_pre, g_post, mu, w_r, w_k, w_v, w_g, w0, w1, w2, a0, a1, a2, k_k, k_a, r_k,
                gn_w, gn_b, w_out, st_S, st_shift):
    (T, BS, TS, NM) = seg
    D = X.shape[1]
    HN = r_k.shape[0]
    HD = RWKV_HEAD
    r_s, r_m = T, T + BS * TS
    h = norm_only(X, g_pre)
    h_s = h[r_s:r_m].reshape(BS, TS, D)
    prev = jnp.concatenate([
        h[r_m + NM - 1:r_m + NM], h[:T - 1],
        jnp.concatenate([st_shift[:, None, :], h_s[:, :-1]], axis=1).reshape(BS * TS, D),
        jnp.zeros((1, D), F32), h[r_m:r_m + NM - 1]], axis=0)

    big = mix_proj(h, prev, mu[jnp.array([0, 2, 3, 5])], jnp.stack([w_r, w_k, w_v, w_g]).astype(BF16))
    r, k, v, gate_pre = big[:, :D], big[:, D:2 * D], big[:, 2 * D:3 * D], big[:, 3 * D:]
    lora_w = jnp.stack([_pad_cols(w1, LANES), _pad_cols(a1, LANES)]).astype(BF16)
    lo = mix_proj(h, prev, mu[jnp.array([1, 4])], lora_w)
    pad_rows = lambda w: jnp.pad(w, ((0, LANES - w.shape[0]), (0, 0))).astype(BF16)
    w_pre = lora_out(lo, 0, pad_rows(w2), w0, True)
    a_pre = lora_out(lo, 1, pad_rows(a2), a0, False)

    w = -jax.nn.softplus(-w_pre) - 0.5
    lw = -jnp.exp(w)
    a = jax.nn.sigmoid(a_pre)
    R = X.shape[0]
    kk = (k * k_k).reshape(R, HN, HD)
    kk = (kk * lax.rsqrt(jnp.sum(kk * kk, axis=-1, keepdims=True) + L2_EPS)).reshape(R, D)
    k2 = k * (1.0 + (a - 1.0) * k_a)
    a_t = -kk
    b_t = kk * a

    zS = jnp.zeros((1, HN, HD, HD), F32)
    o_m, S_m = wkv_mix(r, lw, k2, v, a_t, b_t, zS, r_m, 1, 1, NM)
    LP = 64
    o_p, S_p = wkv_mix(r, lw, k2, v, a_t, b_t, S_m, 0, 1, T // LP, LP)
    o_s, S_s = wkv_mix(r, lw, k2, v, a_t, b_t, st_S, r_s, BS, 1, TS)
    o = jnp.concatenate([o_p, o_s, o_m], axis=0).reshape(R, HN, HD)

    mean = jnp.mean(o, axis=-1, keepdims=True)
    var = jnp.mean(jnp.square(o - mean), axis=-1, keepdims=True)
    o = ((o - mean) * lax.rsqrt(var + RWKV_GN_EPS)).reshape(R, D) * gn_w + gn_b
    bonus = jnp.sum((r * k2).reshape(R, HN, HD) * r_k, axis=-1, keepdims=True) * v.reshape(R, HN, HD)
    o = o + bonus.reshape(R, D)
    o = (o * _silu(gate_pre)).astype(BF16)
    X = out_proj(o, w_out.astype(BF16), X, g_post)
    return X, (S_p, h[T - 1:T], S_s, h_s[:, -1])


def kernel(x_prompt, x_sample, state_gdn_S, state_gdn_conv, cache_fox_k, cache_fox_v, cache_fox_logf, state_rwkv_S, state_rwkv_shift, meta, norm_pre, norm_post, gdn_w_in, gdn_conv_w, gdn_a_log, gdn_dt_bias, gdn_o_norm, gdn_w_out, fox_w_in, fox_b_f, fox_w_out, rwkv_mu, rwkv_w_r, rwkv_w_k, rwkv_w_v, rwkv_w_g, rwkv_w0, rwkv_w1, rwkv_w2, rwkv_a0, rwkv_a1, rwkv_a2, rwkv_k_k, rwkv_k_a, rwkv_r_k, rwkv_gn_w, rwkv_gn_b, rwkv_w_out):
    _, T, D = x_prompt.shape
    BS, TS, _ = x_sample.shape
    NM = meta.shape[0]
    depth = norm_pre.shape[0]
    assert x_prompt.shape[0] == 1
    seg = (T, BS, TS, NM)
    X = jnp.concatenate([x_prompt[0], x_sample.reshape(BS * TS, D), meta.astype(x_prompt.dtype)], axis=0)

    gdn_out, fox_out, rwkv_out = [], [], []
    for i in range(depth):
        kind, j = i % 3, i // 3
        if kind == 0:
            X, st = _gdn_layer(X, seg, norm_pre[i], norm_post[i], gdn_w_in[j], gdn_conv_w[j], gdn_a_log[j],
                               gdn_dt_bias[j], gdn_o_norm[j], gdn_w_out[j], state_gdn_S[j], state_gdn_conv[j])
            gdn_out.append(st)
        elif kind == 1:
            X, st = _fox_layer(X, seg, norm_pre[i], norm_post[i], fox_w_in[j], fox_b_f[j], fox_w_out[j],
                               cache_fox_k[j], cache_fox_v[j], cache_fox_logf[j])
            fox_out.append(st)
        else:
            X, st = _rwkv_layer(X, seg, norm_pre[i], norm_post[i], rwkv_mu[j], rwkv_w_r[j], rwkv_w_k[j],
                                rwkv_w_v[j], rwkv_w_g[j], rwkv_w0[j], rwkv_w1[j], rwkv_w2[j], rwkv_a0[j],
                                rwkv_a1[j], rwkv_a2[j], rwkv_k_k[j], rwkv_k_a[j], rwkv_r_k[j], rwkv_gn_w[j],
                                rwkv_gn_b[j], rwkv_w_out[j], state_rwkv_S[j], state_rwkv_shift[j])
            rwkv_out.append(st)

    stack = lambda items, n: jnp.stack([it[n] for it in items])
    y_prompt = X[:T][None]
    y_sample = X[T:T + BS * TS].reshape(BS, TS, D)
    return (y_prompt, y_sample,
            stack(gdn_out, 0), stack(gdn_out, 1),
            stack(fox_out, 0), stack(fox_out, 1), stack(fox_out, 2),
            stack(rwkv_out, 0), stack(rwkv_out, 1),
            stack(gdn_out, 2), stack(gdn_out, 3),
            stack(fox_out, 3), stack(fox_out, 4), stack(fox_out, 5),
            stack(rwkv_out, 2), stack(rwkv_out, 3))
```

```python
import functools

import jax
import jax.numpy as jnp
from jax import lax
from jax.experimental import pallas as pl
from jax.experimental.pallas import tpu as pltpu

F32 = jnp.float32
BF16 = jnp.bfloat16

NORM_EPS = 1e-6
L2_EPS = 1e-6
RWKV_GN_EPS = 64e-5
GDN_HEAD = 128
GDN_CONV = 4
FOX_HEAD = 128
RWKV_HEAD = 64
LANES = 128
SUBLANES = 8
CONV_PAD = 8
INV_BASE = 16
STACK = 4
VMEM_LIMIT = 48 * 1024 * 1024


def _cparams(sem):
    return pltpu.CompilerParams(dimension_semantics=sem, vmem_limit_bytes=VMEM_LIMIT)


def _log2(n):
    assert n & (n - 1) == 0
    return n.bit_length() - 1


def _dot(a, b):
    return jnp.dot(a, b, preferred_element_type=F32)


def _mm(a, b):
    return _dot(a.astype(BF16), b.astype(BF16))


def _mm_nt(a, b):
    return lax.dot_general(a.astype(BF16), b.astype(BF16), (((1,), (1,)), ((), ())),
                           preferred_element_type=F32)


def _mm_tn(a, b):
    return lax.dot_general(a.astype(BF16), b.astype(BF16), (((0,), (0,)), ((), ())),
                           preferred_element_type=F32)


def _split2(a):
    hi = a.astype(BF16)
    lo = (a - hi.astype(F32)).astype(BF16)
    return hi, lo


def _mm3(a, b):
    m = a.shape[0]
    ah, al = _split2(a)
    bh, bl = _split2(b)
    t = _dot(jnp.concatenate([ah, al], axis=0), bh)
    return t[:m] + (t[m:] + _dot(ah, bl))


def _split3(a):
    hi = a.astype(BF16)
    r = a - hi.astype(F32)
    mid = r.astype(BF16)
    lo = (r - mid.astype(F32)).astype(BF16)
    return hi, mid, lo


def _cumsum_rows(tril, g):
    hi, mid, lo = _split3(g)
    return _dot(tril, hi) + (_dot(tril, mid) + _dot(tril, lo))


def _cumsum_cols(g, triu):
    hi, mid, lo = _split3(g)
    return _dot(hi, triu) + (_dot(mid, triu) + _dot(lo, triu))


def _tri_inv(a_list, diff, eye, L):
    n1 = [jnp.where(diff < INV_BASE, -a, 0.0) for a in a_list]
    p = [eye + n for n in n1]
    npow = n1
    steps = 1
    while 2 * steps < INV_BASE:
        npow = [_mm3(x, x) for x in npow]
        p = [pp + _mm3(pp, x) for pp, x in zip(p, npow)]
        steps *= 2
    blk = INV_BASE
    while blk < L:
        e = [jnp.where((diff < 2 * blk) & (diff >= blk), a, 0.0) for a in a_list]
        pe = [_mm3(pp, ee) for pp, ee in zip(p, e)]
        p = [pp - _mm3(x, pp) for pp, x in zip(p, pe)]
        blk *= 2
    return p


def _stack_heads(x, n, w):
    return jnp.concatenate([x[:, j * w:(j + 1) * w] for j in range(n)], axis=0)


def _expand_heads(x, n, L, w):
    t = jnp.concatenate([x] * n, axis=0)
    rh = lax.shift_right_logical(lax.broadcasted_iota(jnp.int32, t.shape, 0), _log2(L))
    ch = lax.shift_right_logical(lax.broadcasted_iota(jnp.int32, t.shape, 1), _log2(w))
    return jnp.where(rh == ch, t, 0.0)


def _silu(x):
    return x * jax.nn.sigmoid(x)


def _norm_proj_kernel(x_ref, g_ref, w_ref, o_ref, h_scr):
    @pl.when(pl.program_id(1) == 0)
    def _():
        x = x_ref[...]
        ms = jnp.mean(x * x, axis=-1, keepdims=True)
        h_scr[...] = (x * lax.rsqrt(ms + NORM_EPS) * g_ref[...]).astype(BF16)

    o_ref[...] = _dot(h_scr[...], w_ref[...])


def norm_proj(x, g, w, tm=512, tn=1024):
    T, D = x.shape
    N = w.shape[1]
    tn = min(tn, N)
    assert N % tn == 0
    return pl.pallas_call(
        _norm_proj_kernel,
        grid=(pl.cdiv(T, tm), N // tn),
        in_specs=[pl.BlockSpec((tm, D), lambda i, j: (i, 0)),
                  pl.BlockSpec((1, D), lambda i, j: (0, 0)),
                  pl.BlockSpec((D, tn), lambda i, j: (0, j))],
        out_specs=pl.BlockSpec((tm, tn), lambda i, j: (i, j)),
        out_shape=jax.ShapeDtypeStruct((T, N), F32),
        scratch_shapes=[pltpu.VMEM((tm, D), BF16)],
        compiler_params=_cparams(("parallel", "arbitrary")),
        name="norm_proj",
    )(x, g.reshape(1, D), w)


def _norm_kernel(x_ref, g_ref, o_ref):
    x = x_ref[...]
    ms = jnp.mean(x * x, axis=-1, keepdims=True)
    o_ref[...] = x * lax.rsqrt(ms + NORM_EPS) * g_ref[...]


def norm_only(x, g, tm=512):
    T, D = x.shape
    return pl.pallas_call(
        _norm_kernel,
        grid=(pl.cdiv(T, tm),),
        in_specs=[pl.BlockSpec((tm, D), lambda i: (i, 0)), pl.BlockSpec((1, D), lambda i: (0, 0))],
        out_specs=pl.BlockSpec((tm, D), lambda i: (i, 0)),
        out_shape=jax.ShapeDtypeStruct((T, D), F32),
        compiler_params=_cparams(("parallel",)),
        name="norm_only",
    )(x, g.reshape(1, D))


def _mix_proj_kernel(h_ref, p_ref, mu_ref, w_ref, o_ref, l_scr):
    @pl.when(pl.program_id(2) == 0)
    def _():
        h = h_ref[...]
        l_scr[...] = (h + (p_ref[...] - h) * mu_ref[0]).astype(BF16)

    o_ref[...] = _dot(l_scr[...], w_ref[0])


def mix_proj(h, prev, mu, w, tm=512, tn=1024):
    T, D = h.shape
    G, _, N = w.shape
    tn = min(tn, N)
    nj = N // tn
    return pl.pallas_call(
        _mix_proj_kernel,
        grid=(pl.cdiv(T, tm), G, nj),
        in_specs=[pl.BlockSpec((tm, D), lambda i, g, j: (i, 0)),
                  pl.BlockSpec((tm, D), lambda i, g, j: (i, 0)),
                  pl.BlockSpec((1, 1, D), lambda i, g, j: (g, 0, 0)),
                  pl.BlockSpec((1, D, tn), lambda i, g, j: (g, 0, j))],
        out_specs=pl.BlockSpec((tm, tn), lambda i, g, j: (i, g * nj + j)),
        out_shape=jax.ShapeDtypeStruct((T, G * N), F32),
        scratch_shapes=[pltpu.VMEM((tm, D), BF16)],
        compiler_params=_cparams(("parallel", "arbitrary", "arbitrary")),
        name="mix_proj",
    )(h, prev, mu.reshape(G, 1, D), w)


def _lora_kernel(use_tanh, x_ref, w_ref, b_ref, o_ref):
    x = x_ref[...]
    if use_tanh:
        x = jnp.tanh(x)
    o_ref[...] = b_ref[...] + _dot(x.astype(BF16), w_ref[...])


def lora_out(x, col_block, w, b, use_tanh, tm=512):
    T = x.shape[0]
    K, N = w.shape
    return pl.pallas_call(
        functools.partial(_lora_kernel, use_tanh),
        grid=(pl.cdiv(T, tm),),
        in_specs=[pl.BlockSpec((tm, K), lambda i: (i, col_block)),
                  pl.BlockSpec((K, N), lambda i: (0, 0)),
                  pl.BlockSpec((1, N), lambda i: (0, 0))],
        out_specs=pl.BlockSpec((tm, N), lambda i: (i, 0)),
        out_shape=jax.ShapeDtypeStruct((T, N), F32),
        compiler_params=_cparams(("parallel",)),
        name="lora_out",
    )(x, w, b.reshape(1, N))


def _out_proj_kernel(tn, a_ref, w_ref, x_ref, g_ref, o_ref, y_scr):
    j = pl.program_id(1)
    off = pl.multiple_of(j * tn, tn)
    y_scr[:, pl.ds(off, tn)] = _dot(a_ref[...], w_ref[...])

    @pl.when(j == pl.num_programs(1) - 1)
    def _():
        y = y_scr[...]
        ms = jnp.mean(y * y, axis=-1, keepdims=True)
        o_ref[...] = x_ref[...] + y * lax.rsqrt(ms + NORM_EPS) * g_ref[...]


def out_proj(a, w, x, g, tm=512, tn=512):
    T, K = a.shape
    D = w.shape[1]
    return pl.pallas_call(
        functools.partial(_out_proj_kernel, tn),
        grid=(pl.cdiv(T, tm), D // tn),
        in_specs=[pl.BlockSpec((tm, K), lambda i, j: (i, 0)),
                  pl.BlockSpec((K, tn), lambda i, j: (0, j)),
                  pl.BlockSpec((tm, D), lambda i, j: (i, 0)),
                  pl.BlockSpec((1, D), lambda i, j: (0, 0))],
        out_specs=pl.BlockSpec((tm, D), lambda i, j: (i, 0)),
        out_shape=jax.ShapeDtypeStruct((T, D), F32),
        scratch_shapes=[pltpu.VMEM((tm, D), F32)],
        compiler_params=_cparams(("parallel", "arbitrary")),
        name="out_proj",
    )(a, w, x, g.reshape(1, D))


GDN_NU = 2


def _gdn_intra_kernel(L, qn_ref, kn_ref, vn_ref, qp_ref, kp_ref, vp_ref, cq_ref, ck_ref, cv_ref,
                      wq_ref, wk_ref, wv_ref, gc_ref, gr_ref, bc_ref,
                      uv_ref, wqe_ref, kt_ref, qkd_ref, egl_ref, qbuf, kbuf, vbuf):
    c = pl.program_id(2)
    HD, NH = GDN_HEAD, STACK
    R = NH * L
    first = c == 0

    def conv_silu(buf, u_ref, p_ref, st_ref, w_ref):
        buf[CONV_PAD - 3:CONV_PAD, :] = jnp.where(first, st_ref[0], p_ref[SUBLANES - 3:SUBLANES, :])
        buf[CONV_PAD:CONV_PAD + L, :] = u_ref[...]
        acc = buf[CONV_PAD - 3:CONV_PAD - 3 + L, :] * w_ref[0:1, :]
        acc = acc + buf[CONV_PAD - 2:CONV_PAD - 2 + L, :] * w_ref[1:2, :]
        acc = acc + buf[CONV_PAD - 1:CONV_PAD - 1 + L, :] * w_ref[2:3, :]
        acc = acc + buf[CONV_PAD:CONV_PAD + L, :] * w_ref[3:4, :]
        return _silu(acc)

    q = conv_silu(qbuf, qn_ref, qp_ref, cq_ref, wq_ref)
    k = conv_silu(kbuf, kn_ref, kp_ref, ck_ref, wk_ref)
    v = conv_silu(vbuf, vn_ref, vp_ref, cv_ref, wv_ref)

    rl = lax.broadcasted_iota(jnp.int32, (L, L), 0)
    cl = lax.broadcasted_iota(jnp.int32, (L, L), 1)
    tril = jnp.where(cl <= rl, 1.0, 0.0).astype(BF16)
    triu = jnp.where(rl <= cl, 1.0, 0.0).astype(BF16)
    row = lax.broadcasted_iota(jnp.int32, (R, R), 0)
    col = lax.broadcasted_iota(jnp.int32, (R, R), 1)
    diff = row ^ col
    causal = (diff < L) & (col <= row)
    strict = (diff < L) & (col < row)
    eye = jnp.where(row == col, 1.0, 0.0).astype(F32)

    def l2n(x):
        return x * lax.rsqrt(jnp.sum(x * x, axis=-1, keepdims=True) + L2_EPS)

    def col_stack(x):
        return jnp.concatenate([x[:, j:j + 1] for j in range(NH)], axis=0)

    kst, qst, vst, gst, bst, glast_st, grow_st, glast = [], [], [], [], [], [], [], []
    for u in range(GDN_NU):
        qh = [l2n(q[:, (2 * u + i) * HD:(2 * u + i + 1) * HD]) * (HD ** -0.5) for i in range(2)]
        kh = [l2n(k[:, (2 * u + i) * HD:(2 * u + i + 1) * HD]) for i in range(2)]
        kst.append(jnp.concatenate([kh[0], kh[0], kh[1], kh[1]], axis=0))
        qst.append(jnp.concatenate([qh[0], qh[0], qh[1], qh[1]], axis=0))
        vst.append(_stack_heads(v[:, u * NH * HD:(u + 1) * NH * HD], NH, HD))
        gcol = _cumsum_rows(tril, gc_ref[u])
        grow = _cumsum_cols(gr_ref[u, 0], triu)
        gl = gcol[L - 1:L, :]
        glast.append(gl)
        gst.append(col_stack(gcol))
        bst.append(col_stack(bc_ref[u]))
        glast_st.append(jnp.concatenate([jnp.broadcast_to(gl[:, j:j + 1], (L, 1)) for j in range(NH)], axis=0))
        grow_st.append(jnp.concatenate([grow[j:j + 1, :] for j in range(NH)], axis=1))

    U = range(GDN_NU)
    kk = [_mm_nt(kst[u], kst[u]) for u in U]
    qk = [_mm_nt(qst[u], kst[u]) for u in U]
    decay = [jnp.where(causal, jnp.exp(jnp.where(causal, gst[u] - grow_st[u], 0.0)), 0.0) for u in U]
    a = [jnp.where(strict, kk[u] * decay[u] * bst[u], 0.0) for u in U]
    tinv = _tri_inv(a, diff, eye, L)
    rhs = [jnp.concatenate([vst[u] * bst[u], kst[u] * (bst[u] * jnp.exp(gst[u]))], axis=1) for u in U]
    sol = [_mm(tinv[u], rhs[u]) for u in U]
    for u in U:
        uv_ref[0, u] = sol[u][:, :HD]
        wk = sol[u][:, HD:]
        qe = qst[u] * jnp.exp(gst[u])
        wqe_ref[0, u] = jnp.concatenate(
            [x[j * L:(j + 1) * L] for j in range(NH) for x in (wk, qe)], axis=0).astype(BF16)
        kt_ref[0, u] = (kst[u] * jnp.exp(glast_st[u] - gst[u])).astype(BF16)
        qkd_ref[0, u] = (qk[u] * decay[u]).astype(BF16)
        egl_ref[0, u] = jnp.exp(glast[u])


def _gdn_inter_kernel(L, HG, uv_ref, wqe_ref, kt_ref, qkd_ref, egl_ref, z_ref, s0_ref, on_ref,
                      o_ref, so_ref, s_scr):
    c = pl.program_id(1)
    HD, NH = GDN_HEAD, STACK
    R = NH * L

    @pl.when(c == 0)
    def _():
        for g in range(HG):
            s_scr[g] = jnp.concatenate([s0_ref[0, NH * g + j] for j in range(NH)], axis=1)

    rh = lax.shift_right_logical(lax.broadcasted_iota(jnp.int32, (R, NH * HD), 0), _log2(L))
    ch = lax.shift_right_logical(lax.broadcasted_iota(jnp.int32, (R, NH * HD), 1), _log2(HD))
    own = rh == ch

    s_old = [s_scr[g] for g in range(HG)]
    d1 = [[_dot(wqe_ref[0, g, 2 * L * j:2 * L * (j + 1), :], s_old[g][:, j * HD:(j + 1) * HD].astype(BF16))
           for j in range(NH)] for g in range(HG)]
    v_new = [uv_ref[0, g] - jnp.concatenate([d1[g][j][:L] for j in range(NH)], axis=0) for g in range(HG)]
    o_st = [jnp.concatenate([d1[g][j][L:] for j in range(NH)], axis=0) + _dot(qkd_ref[0, g], v_new[g].astype(BF16))
            for g in range(HG)]
    for g in range(HG):
        v_exp = jnp.where(own, jnp.concatenate([v_new[g]] * NH, axis=1), 0.0)
        egl = egl_ref[0, g]
        e_exp = jnp.concatenate([jnp.broadcast_to(egl[:, j:j + 1], (1, HD)) for j in range(NH)], axis=1)
        s_scr[g] = s_old[g] * e_exp + _mm_tn(kt_ref[0, g], v_exp)

    outs = []
    for g in range(HG):
        for j in range(NH):
            o = o_st[g][j * L:(j + 1) * L]
            zj = z_ref[:, (NH * g + j) * HD:(NH * g + j + 1) * HD]
            ms = jnp.mean(o * o, axis=-1, keepdims=True)
            outs.append((o * lax.rsqrt(ms + NORM_EPS) * on_ref[...] * _silu(zj)).astype(BF16))
    o_ref[...] = jnp.concatenate(outs, axis=1)

    @pl.when(c == pl.num_programs(1) - 1)
    def _():
        for g in range(HG):
            for j in range(NH):
                so_ref[0, NH * g + j] = s_scr[g][:, j * HD:(j + 1) * HD]


def gdn_mix(proj, gate_g, gate_b, conv_state, s0, conv_w, o_norm, row0, B, NC, L):
    HV = s0.shape[1]
    HD, NH, NU = GDN_HEAD, STACK, GDN_NU
    HG = HV // NH
    QW = NU * (NH // 2) * HD
    VW = NU * NH * HD
    nq = (HV // 2) * HD // QW
    nv = HV * HD // VW
    Tseg = B * NC * L
    R = NH * L
    rb0 = row0 // L
    assert row0 % L == 0 and L % SUBLANES == 0 and HG % NU == 0

    lane_pad = lambda t: jnp.pad(t, ((0, 0), (0, 0), (0, LANES - NH)))
    gc = lane_pad(gate_g.reshape(Tseg, HG, NH).transpose(1, 0, 2))
    bc = lane_pad(gate_b.reshape(Tseg, HG, NH).transpose(1, 0, 2))
    gr = gate_g.reshape(B * NC, L, HG, NH).transpose(2, 0, 3, 1)
    gr = jnp.pad(gr, ((0, 0), (0, 0), (0, SUBLANES - NH), (0, 0)))
    cq = conv_state[:, :, :nq * QW]
    ck = conv_state[:, :, nq * QW:2 * nq * QW]
    cv = conv_state[:, :, 2 * nq * QW:]
    wq = conv_w[:, :nq * QW]
    wk = conv_w[:, nq * QW:2 * nq * QW]
    wv = conv_w[:, 2 * nq * QW:]

    chunk = lambda b, h, c: rb0 + b * NC + c
    before = lambda b, h, c: jnp.maximum(chunk(b, h, c) * (L // SUBLANES) - 1, 0)
    intra_in = [
        pl.BlockSpec((L, QW), lambda b, h, c: (chunk(b, h, c), h)),
        pl.BlockSpec((L, QW), lambda b, h, c: (chunk(b, h, c), nq + h)),
        pl.BlockSpec((L, VW), lambda b, h, c: (chunk(b, h, c), nv + h)),
        pl.BlockSpec((SUBLANES, QW), lambda b, h, c: (before(b, h, c), h)),
        pl.BlockSpec((SUBLANES, QW), lambda b, h, c: (before(b, h, c), nq + h)),
        pl.BlockSpec((SUBLANES, VW), lambda b, h, c: (before(b, h, c), nv + h)),
        pl.BlockSpec((1, GDN_CONV - 1, QW), lambda b, h, c: (b, 0, h)),
        pl.BlockSpec((1, GDN_CONV - 1, QW), lambda b, h, c: (b, 0, h)),
        pl.BlockSpec((1, GDN_CONV - 1, VW), lambda b, h, c: (b, 0, h)),
        pl.BlockSpec((GDN_CONV, QW), lambda b, h, c: (0, h)),
        pl.BlockSpec((GDN_CONV, QW), lambda b, h, c: (0, h)),
        pl.BlockSpec((GDN_CONV, VW), lambda b, h, c: (0, h)),
        pl.BlockSpec((NU, L, LANES), lambda b, h, c: (h, b * NC + c, 0)),
        pl.BlockSpec((NU, 1, SUBLANES, L), lambda b, h, c: (h, b * NC + c, 0, 0)),
        pl.BlockSpec((NU, L, LANES), lambda b, h, c: (h, b * NC + c, 0)),
    ]
    unit = lambda w: pl.BlockSpec((1, NU, w[0], w[1]), lambda b, h, c: (b * NC + c, h, 0, 0))
    NCH = B * NC
    uv, wqe, kt, qkd, egl = pl.pallas_call(
        functools.partial(_gdn_intra_kernel, L),
        grid=(B, HG // NU, NC),
        in_specs=intra_in,
        out_specs=[unit((R, HD)), unit((2 * R, HD)), unit((R, HD)), unit((R, R)), unit((1, LANES))],
        out_shape=[jax.ShapeDtypeStruct((NCH, HG, R, HD), F32),
                   jax.ShapeDtypeStruct((NCH, HG, 2 * R, HD), BF16),
                   jax.ShapeDtypeStruct((NCH, HG, R, HD), BF16),
                   jax.ShapeDtypeStruct((NCH, HG, R, R), BF16),
                   jax.ShapeDtypeStruct((NCH, HG, 1, LANES), F32)],
        scratch_shapes=[pltpu.VMEM((CONV_PAD + L, QW), F32),
                        pltpu.VMEM((CONV_PAD + L, QW), F32),
                        pltpu.VMEM((CONV_PAD + L, VW), F32)],
        compiler_params=_cparams(("parallel", "parallel", "parallel")),
        name="gdn_intra",
    )(proj, proj, proj, proj, proj, proj, cq, ck, cv, wq, wk, wv, gc, gr, bc)

    allu = lambda w: pl.BlockSpec((1, HG, w[0], w[1]), lambda b, c: (b * NC + c, 0, 0, 0))
    zblk = 2 * nv * VW // (HV * HD)
    o, s_new = pl.pallas_call(
        functools.partial(_gdn_inter_kernel, L, HG),
        grid=(B, NC),
        in_specs=[allu((R, HD)), allu((2 * R, HD)), allu((R, HD)), allu((R, R)), allu((1, LANES)),
                  pl.BlockSpec((L, HV * HD), lambda b, c: (rb0 + b * NC + c, zblk)),
                  pl.BlockSpec((1, HV, HD, HD), lambda b, c: (b, 0, 0, 0)),
                  pl.BlockSpec((1, HD), lambda b, c: (0, 0))],
        out_specs=[pl.BlockSpec((L, HV * HD), lambda b, c: (b * NC + c, 0)),
                   pl.BlockSpec((1, HV, HD, HD), lambda b, c: (b, 0, 0, 0))],
        out_shape=[jax.ShapeDtypeStruct((Tseg, HV * HD), BF16),
                   jax.ShapeDtypeStruct((B, HV, HD, HD), F32)],
        scratch_shapes=[pltpu.VMEM((HG, HD, NH * HD), F32)],
        compiler_params=_cparams(("parallel", "arbitrary")),
        name="gdn_inter",
    )(uv, wqe, kt, qkd, egl, proj, s0, o_norm.reshape(1, HD))
    return o, s_new


def _fox_prompt_kernel(TQ, scale, q_ref, k_ref, v_ref, km_ref, vm_ref, z_ref, ck_ref, cm_ref, o_ref):
    qi = pl.program_id(1)
    q = q_ref[...].astype(BF16)

    s = _mm_nt(q, km_ref[...]) * scale - cm_ref[0]
    m = jnp.max(s, axis=-1, keepdims=True)
    p = jnp.exp(s - m)
    l = jnp.sum(p, axis=-1, keepdims=True)
    acc = _mm(p, vm_ref[...])

    def step(kb, carry, diagonal):
        m, l, acc = carry
        off = pl.multiple_of(kb * TQ, TQ)
        s = _mm_nt(q, k_ref[pl.ds(off, TQ), :]) * scale - ck_ref[0, :, pl.ds(off, TQ)]
        if diagonal:
            row = lax.broadcasted_iota(jnp.int32, (TQ, TQ), 0)
            col = lax.broadcasted_iota(jnp.int32, (TQ, TQ), 1)
            s = jnp.where(col <= row, s, -jnp.inf)
        m_new = jnp.maximum(m, jnp.max(s, axis=-1, keepdims=True))
        alpha = jnp.exp(m - m_new)
        p = jnp.exp(s - m_new)
        l = alpha * l + jnp.sum(p, axis=-1, keepdims=True)
        acc = alpha * acc + _mm(p, v_ref[pl.ds(off, TQ), :])
        return m_new, l, acc

    carry = lax.fori_loop(0, qi, lambda kb, cr: step(kb, cr, False), (m, l, acc))
    m, l, acc = step(qi, carry, True)
    o_ref[...] = ((acc / l) * _silu(z_ref[...])).astype(BF16)


def fox_prompt(proj, kb16, vb16, c_main, c_meta, T, meta_row0, n_meta, H, TQ=512):
    HD = FOX_HEAD
    TQ = min(TQ, T)
    assert T % TQ == 0 and meta_row0 % n_meta == 0
    mb = meta_row0 // n_meta
    return pl.pallas_call(
        functools.partial(_fox_prompt_kernel, TQ, HD ** -0.5),
        grid=(H, T // TQ),
        in_specs=[pl.BlockSpec((TQ, HD), lambda h, i: (i, h)),
                  pl.BlockSpec((T, HD), lambda h, i: (0, h)),
                  pl.BlockSpec((T, HD), lambda h, i: (0, h)),
                  pl.BlockSpec((n_meta, HD), lambda h, i: (mb, h)),
                  pl.BlockSpec((n_meta, HD), lambda h, i: (mb, h)),
                  pl.BlockSpec((TQ, HD), lambda h, i: (i, 3 * H + h)),
                  pl.BlockSpec((1, 1, T), lambda h, i: (h, 0, 0)),
                  pl.BlockSpec((1, 1, n_meta), lambda h, i: (h, 0, 0))],
        out_specs=pl.BlockSpec((TQ, HD), lambda h, i: (i, h)),
        out_shape=jax.ShapeDtypeStruct((T, H * HD), BF16),
        compiler_params=_cparams(("parallel", "arbitrary")),
        name="fox_prompt",
    )(proj, kb16, vb16, kb16, vb16, proj, c_main, c_meta)


def _fox_short_kernel(has_cache, scale, q_ref, k_ref, v_ref, z_ref, cn_ref, *rest):
    if has_cache:
        kc_ref, vc_ref, cc_ref, o_ref = rest
    else:
        (o_ref,) = rest
    TQ = q_ref.shape[0]
    q = q_ref[...].astype(BF16)
    row = lax.broadcasted_iota(jnp.int32, (TQ, TQ), 0)
    col = lax.broadcasted_iota(jnp.int32, (TQ, TQ), 1)
    s_n = _mm_nt(q, k_ref[...]) * scale - cn_ref[0, 0]
    s_n = jnp.where(col <= row, s_n, -jnp.inf)
    m = jnp.max(s_n, axis=-1, keepdims=True)
    if has_cache:
        kc = kc_ref[0].astype(BF16)
        s_c = _mm_nt(q, kc) * scale - cc_ref[0, 0]
        m = jnp.maximum(m, jnp.max(s_c, axis=-1, keepdims=True))
    p_n = jnp.exp(s_n - m)
    l = jnp.sum(p_n, axis=-1, keepdims=True)
    acc = _mm(p_n, v_ref[...])
    if has_cache:
        p_c = jnp.exp(s_c - m)
        l = l + jnp.sum(p_c, axis=-1, keepdims=True)
        acc = acc + _mm(p_c, vc_ref[0])
    o_ref[...] = ((acc / l) * _silu(z_ref[...])).astype(BF16)


def fox_short(proj, c_new, row0, B, TQ, H, cache=None):
    HD = FOX_HEAD
    assert row0 % TQ == 0
    rb0 = row0 // TQ
    in_specs = [pl.BlockSpec((TQ, HD), lambda b, h: (rb0 + b, h)),
                pl.BlockSpec((TQ, HD), lambda b, h: (rb0 + b, H + h)),
                pl.BlockSpec((TQ, HD), lambda b, h: (rb0 + b, 2 * H + h)),
                pl.BlockSpec((TQ, HD), lambda b, h: (rb0 + b, 3 * H + h)),
                pl.BlockSpec((1, 1, 1, TQ), lambda b, h: (b, h, 0, 0))]
    args = [proj, proj, proj, proj, c_new]
    if cache is not None:
        kc, vc, cc, b0 = cache
        P = kc.shape[1]
        in_specs += [pl.BlockSpec((1, P, HD), lambda b, h: (b0 + b, 0, h)),
                     pl.BlockSpec((1, P, HD), lambda b, h: (b0 + b, 0, h)),
                     pl.BlockSpec((1, 1, 1, P), lambda b, h: (b, h, 0, 0))]
        args += [kc, vc, cc]
    return pl.pallas_call(
        functools.partial(_fox_short_kernel, cache is not None, HD ** -0.5),
        grid=(B, H),
        in_specs=in_specs,
        out_specs=pl.BlockSpec((TQ, HD), lambda b, h: (b, h)),
        out_shape=jax.ShapeDtypeStruct((B * TQ, H * HD), BF16),
        compiler_params=_cparams(("parallel", "parallel")),
        name="fox_short",
    )(*args)


WKV_NU = 2


def _wkv_intra_kernel(L, r_ref, lw_ref, k_ref, v_ref, a_ref, b_ref,
                      x_ref, u0_ref, o0_ref, arb_ref, vk_ref, bh_ref, gl_ref):
    HD, NH = RWKV_HEAD, STACK
    R = NH * L
    W = NH * HD

    rl = lax.broadcasted_iota(jnp.int32, (L, L), 0)
    cl = lax.broadcasted_iota(jnp.int32, (L, L), 1)
    tril = jnp.where(cl <= rl, 1.0, 0.0).astype(BF16)
    row = lax.broadcasted_iota(jnp.int32, (R, R), 0)
    col = lax.broadcasted_iota(jnp.int32, (R, R), 1)
    diff = row ^ col
    causal = (diff < L) & (col <= row)
    strict = (diff < L) & (col < row)
    eye = jnp.where(row == col, 1.0, 0.0).astype(F32)

    lw = lw_ref[...]
    cum = _cumsum_rows(tril, lw)
    clast = cum[L - 1:L, :]
    e_neg = jnp.exp(-cum)
    e_tail = jnp.exp(clast - cum)
    rt = r_ref[...] * jnp.exp(cum)
    at = a_ref[...] * jnp.exp(cum - lw)
    k = k_ref[...]
    b = b_ref[...]
    kt = k * e_neg
    bt = b * e_neg
    khat = k * e_tail
    bhat = b * e_tail
    v = v_ref[...]
    gl_ref[0] = jnp.exp(clast)

    U = range(WKV_NU)
    sl = lambda x, u: x[:, u * W:(u + 1) * W]
    st = lambda x, u: _stack_heads(sl(x, u), NH, HD)
    v_st = [st(v, u) for u in U]
    aa = [_mm_nt(jnp.concatenate([st(at, u), st(rt, u)], axis=0),
                 jnp.concatenate([st(bt, u), st(kt, u)], axis=0)) for u in U]
    a_ab = [jnp.where(strict, aa[u][:R, :R], 0.0) for u in U]
    a_ak = [jnp.where(strict, aa[u][:R, R:], 0.0) for u in U]
    a_rb = [jnp.where(causal, aa[u][R:, :R], 0.0) for u in U]
    a_rk = [jnp.where(causal, aa[u][R:, R:], 0.0) for u in U]
    av = [_mm(jnp.concatenate([a_ak[u], a_rk[u]], axis=0), v_st[u]) for u in U]
    vk = [_mm_tn(v_st[u], _expand_heads(sl(khat, u), NH, L, HD)) for u in U]
    tinv = _tri_inv([-x for x in a_ab], diff, eye, L)
    w_exp = [_mm(tinv[u], _expand_heads(sl(at, u), NH, L, HD)) for u in U]
    u0 = [_mm(tinv[u], av[u][:R]) for u in U]
    for u in U:
        x_ref[0, u] = jnp.concatenate([w_exp[u], _expand_heads(sl(rt, u), NH, L, HD)], axis=0).astype(BF16)
        u0_ref[0, u] = u0[u]
        o0_ref[0, u] = av[u][R:]
        arb_ref[0, u] = a_rb[u].astype(BF16)
        vk_ref[0, u] = vk[u]
        bh_ref[0, u] = _expand_heads(sl(bhat, u), NH, L, HD).astype(BF16)


def _wkv_inter_kernel(L, HG, x_ref, u0_ref, o0_ref, arb_ref, vk_ref, bh_ref, gl_ref, s0_ref,
                      o_ref, so_ref, s_scr):
    c = pl.program_id(1)
    HD, NH = RWKV_HEAD, STACK
    R = NH * L
    W = NH * HD

    @pl.when(c == 0)
    def _():
        for g in range(HG):
            s_scr[g] = jnp.concatenate([s0_ref[0, NH * g + j] for j in range(NH)], axis=1)

    s_old = [s_scr[g] for g in range(HG)]
    d1 = [_mm_nt(x_ref[0, g], s_old[g]) for g in range(HG)]
    u_st = [u0_ref[0, g] + d1[g][:R] for g in range(HG)]
    o_st = [d1[g][R:] + o0_ref[0, g] + _dot(arb_ref[0, g], u_st[g].astype(BF16)) for g in range(HG)]
    for g in range(HG):
        s_scr[g] = s_old[g] * gl_ref[0, :, g * W:(g + 1) * W] + vk_ref[0, g] + _mm_tn(u_st[g], bh_ref[0, g])
    o_ref[...] = jnp.concatenate([o_st[g][j * L:(j + 1) * L] for g in range(HG) for j in range(NH)], axis=1)

    @pl.when(c == pl.num_programs(1) - 1)
    def _():
        for g in range(HG):
            for j in range(NH):
                so_ref[0, NH * g + j] = s_scr[g][:, j * HD:(j + 1) * HD]


def wkv_mix(r, lw, k, v, a, b, s0, row0, B, NC, L):
    HN = s0.shape[1]
    HD, NH, NU = RWKV_HEAD, STACK, WKV_NU
    HG = HN // NH
    W = NH * HD
    R = NH * L
    Tseg = B * NC * L
    NCH = B * NC
    rb0 = row0 // L
    assert row0 % L == 0 and HG % NU == 0
    tok = pl.BlockSpec((L, NU * W), lambda bb, h, c: (rb0 + bb * NC + c, h))
    unit = lambda w: pl.BlockSpec((1, NU, w[0], w[1]), lambda bb, h, c: (bb * NC + c, h, 0, 0))
    x, u0, o0, arb, vk, bh, gl = pl.pallas_call(
        functools.partial(_wkv_intra_kernel, L),
        grid=(B, HG // NU, NC),
        in_specs=[tok] * 6,
        out_specs=[unit((2 * R, W)), unit((R, HD)), unit((R, HD)), unit((R, R)), unit((HD, W)), unit((R, W)),
                   pl.BlockSpec((1, 1, NU * W), lambda bb, h, c: (bb * NC + c, 0, h))],
        out_shape=[jax.ShapeDtypeStruct((NCH, HG, 2 * R, W), BF16),
                   jax.ShapeDtypeStruct((NCH, HG, R, HD), F32),
                   jax.ShapeDtypeStruct((NCH, HG, R, HD), F32),
                   jax.ShapeDtypeStruct((NCH, HG, R, R), BF16),
                   jax.ShapeDtypeStruct((NCH, HG, HD, W), F32),
                   jax.ShapeDtypeStruct((NCH, HG, R, W), BF16),
                   jax.ShapeDtypeStruct((NCH, 1, HN * HD), F32)],
        compiler_params=_cparams(("parallel", "parallel", "parallel")),
        name="wkv_intra",
    )(r, lw, k, v, a, b)

    allu = lambda w: pl.BlockSpec((1, HG, w[0], w[1]), lambda bb, c: (bb * NC + c, 0, 0, 0))
    o, s_new = pl.pallas_call(
        functools.partial(_wkv_inter_kernel, L, HG),
        grid=(B, NC),
        in_specs=[allu((2 * R, W)), allu((R, HD)), allu((R, HD)), allu((R, R)), allu((HD, W)), allu((R, W)),
                  pl.BlockSpec((1, 1, HN * HD), lambda bb, c: (bb * NC + c, 0, 0)),
                  pl.BlockSpec((1, HN, HD, HD), lambda bb, c: (bb, 0, 0, 0))],
        out_specs=[pl.BlockSpec((L, HN * HD), lambda bb, c: (bb * NC + c, 0)),
                   pl.BlockSpec((1, HN, HD, HD), lambda bb, c: (bb, 0, 0, 0))],
        out_shape=[jax.ShapeDtypeStruct((Tseg, HN * HD), F32),
                   jax.ShapeDtypeStruct((B, HN, HD, HD), F32)],
        scratch_shapes=[pltpu.VMEM((HG, HD, W), F32)],
        compiler_params=_cparams(("parallel", "arbitrary")),
        name="wkv_inter",
    )(x, u0, o0, arb, vk, bh, gl, s0)
    return o, s_new


def _pad_cols(w, n):
    return jnp.pad(w, ((0, 0), (0, n - w.shape[1])))


def _gdn_layer(X, seg, g_pre, g_post, w_in, conv_w, a_log, dt_bias, o_norm, w_out, st_S, st_conv):
    (T, BS, TS, NM) = seg
    HV = a_log.shape[0]
    CD = conv_w.shape[1]
    VD = HV * GDN_HEAD
    w_main = w_in[:, :CD + VD].astype(BF16)
    w_gate = _pad_cols(w_in[:, CD + VD:], LANES).astype(BF16)
    proj = norm_proj(X, g_pre, w_main)
    gates = norm_proj(X, g_pre, w_gate)
    beta = jax.nn.sigmoid(gates[:, :HV])
    g = -jnp.exp(a_log) * jax.nn.softplus(gates[:, HV:2 * HV] + dt_bias)

    r_s, r_m = T, T + BS * TS
    tail = GDN_CONV - 1
    c_m = proj[r_m + NM - tail:r_m + NM, :CD][None]
    c_p = proj[T - tail:T, :CD][None]
    c_s = proj[r_s:r_m, :CD].reshape(BS, TS, CD)[:, TS - tail:]
    z_conv = jnp.zeros((1, tail, CD), F32)
    z_S = jnp.zeros((1, HV, GDN_HEAD, GDN_HEAD), F32)
    o_m, S_m = gdn_mix(proj, g[r_m:], beta[r_m:], z_conv, z_S, conv_w, o_norm, r_m, 1, 1, NM)
    LP = 64
    o_p, S_p = gdn_mix(proj, g[:T], beta[:T], c_m, S_m, conv_w, o_norm, 0, 1, T // LP, LP)
    o_s, S_s = gdn_mix(proj, g[r_s:r_m], beta[r_s:r_m], st_conv, st_S, conv_w, o_norm, r_s, BS, 1, TS)
    o = jnp.concatenate([o_p, o_s, o_m], axis=0)
    X = out_proj(o, w_out.astype(BF16), X, g_post)
    return X, (S_p, c_p, S_s, c_s)


def _fox_layer(X, seg, g_pre, g_post, w_in, b_f, w_out, caches, j):
    (T, BS, TS, NM) = seg
    cache_k, cache_v, cache_logf = caches
    H = b_f.shape[0]
    FD = H * FOX_HEAD
    proj = norm_proj(X, g_pre, w_in[:, :4 * FD].astype(BF16))
    gates = norm_proj(X, g_pre, _pad_cols(w_in[:, 4 * FD:], LANES).astype(BF16))
    logf = jax.nn.log_sigmoid(gates[:, :H] + b_f)
    r_s, r_m = T, T + BS * TS

    c_p = jnp.cumsum(jnp.concatenate([logf[r_m:], logf[:T]], axis=0), axis=0)
    c_meta = c_p[:NM].T.reshape(H, 1, NM)
    c_main = c_p[NM:].T.reshape(H, 1, T)
    P = cache_k.shape[2]
    c_s = jnp.cumsum(jnp.concatenate([cache_logf[j], logf[r_s:r_m].reshape(BS, TS, H)], axis=1), axis=1)
    c_s = c_s.transpose(0, 2, 1)
    c_cache = c_s[:, :, :P].reshape(BS, H, 1, P)
    c_new = c_s[:, :, P:].reshape(BS, H, 1, TS)

    kb16 = proj[:, FD:2 * FD].astype(BF16)
    vb16 = proj[:, 2 * FD:3 * FD].astype(BF16)
    o_p = fox_prompt(proj, kb16, vb16, c_main, c_meta, T, r_m, NM, H)
    o_m = fox_short(proj, c_meta.reshape(1, H, 1, NM), r_m, 1, NM, H)
    NF = cache_k.shape[0]
    o_s = fox_short(proj, c_new, r_s, BS, TS, H,
                    cache=(cache_k.reshape(NF * BS, P, FD), cache_v.reshape(NF * BS, P, FD), c_cache, j * BS))
    o = jnp.concatenate([o_p, o_s, o_m], axis=0)
    X = out_proj(o, w_out.astype(BF16), X, g_post)

    def seq(a, n):
        return jnp.concatenate([a[r_m:], a[:T]], axis=0).reshape(1, NM + T, H, n)

    k_all = proj[:, FD:2 * FD]
    v_all = proj[:, 2 * FD:3 * FD]
    outs = (seq(k_all, FOX_HEAD), seq(v_all, FOX_HEAD),
            jnp.concatenate([logf[r_m:], logf[:T]], axis=0).reshape(1, NM + T, H),
            k_all[r_s:r_m].reshape(BS, TS, H, FOX_HEAD), v_all[r_s:r_m].reshape(BS, TS, H, FOX_HEAD),
            logf[r_s:r_m].reshape(BS, TS, H))
    return X, outs


def _rwkv_layer(X, seg, g_pre, g_post, mu, w_r, w_k, w_v, w_g, w0, w1, w2, a0, a1, a2, k_k, k_a, r_k,
                gn_w, gn_b, w_out, st_S, st_shift):
    (T, BS, TS, NM) = seg
    D = X.shape[1]
    HN = r_k.shape[0]
    HD = RWKV_HEAD
    r_s, r_m = T, T + BS * TS
    h = norm_only(X, g_pre)
    h_s = h[r_s:r_m].reshape(BS, TS, D)
    prev = jnp.concatenate([
        h[r_m + NM - 1:r_m + NM], h[:T - 1],
        jnp.concatenate([st_shift[:, None, :], h_s[:, :-1]], axis=1).reshape(BS * TS, D),
        jnp.zeros((1, D), F32), h[r_m:r_m + NM - 1]], axis=0)

    big = mix_proj(h, prev, mu[jnp.array([0, 2, 3, 5])], jnp.stack([w_r, w_k, w_v, w_g]).astype(BF16))
    r, k, v, gate_pre = big[:, :D], big[:, D:2 * D], big[:, 2 * D:3 * D], big[:, 3 * D:]
    lora_w = jnp.stack([_pad_cols(w1, LANES), _pad_cols(a1, LANES)]).astype(BF16)
    lo = mix_proj(h, prev, mu[jnp.array([1, 4])], lora_w)
    pad_rows = lambda w: jnp.pad(w, ((0, LANES - w.shape[0]), (0, 0))).astype(BF16)
    w_pre = lora_out(lo, 0, pad_rows(w2), w0, True)
    a_pre = lora_out(lo, 1, pad_rows(a2), a0, False)

    w = -jax.nn.softplus(-w_pre) - 0.5
    lw = -jnp.exp(w)
    a = jax.nn.sigmoid(a_pre)
    R = X.shape[0]
    kk = (k * k_k).reshape(R, HN, HD)
    kk = (kk * lax.rsqrt(jnp.sum(kk * kk, axis=-1, keepdims=True) + L2_EPS)).reshape(R, D)
    k2 = k * (1.0 + (a - 1.0) * k_a)
    a_t = -kk
    b_t = kk * a

    zS = jnp.zeros((1, HN, HD, HD), F32)
    o_m, S_m = wkv_mix(r, lw, k2, v, a_t, b_t, zS, r_m, 1, 1, NM)
    LP = 64
    o_p, S_p = wkv_mix(r, lw, k2, v, a_t, b_t, S_m, 0, 1, T // LP, LP)
    o_s, S_s = wkv_mix(r, lw, k2, v, a_t, b_t, st_S, r_s, BS, 1, TS)
    o = jnp.concatenate([o_p, o_s, o_m], axis=0).reshape(R, HN, HD)

    mean = jnp.mean(o, axis=-1, keepdims=True)
    var = jnp.mean(jnp.square(o - mean), axis=-1, keepdims=True)
    o = ((o - mean) * lax.rsqrt(var + RWKV_GN_EPS)).reshape(R, D) * gn_w + gn_b
    bonus = jnp.sum((r * k2).reshape(R, HN, HD) * r_k, axis=-1, keepdims=True) * v.reshape(R, HN, HD)
    o = o + bonus.reshape(R, D)
    o = (o * _silu(gate_pre)).astype(BF16)
    X = out_proj(o, w_out.astype(BF16), X, g_post)
    return X, (S_p, h[T - 1:T], S_s, h_s[:, -1])


def kernel(x_prompt, x_sample, state_gdn_S, state_gdn_conv, cache_fox_k, cache_fox_v, cache_fox_logf, state_rwkv_S, state_rwkv_shift, meta, norm_pre, norm_post, gdn_w_in, gdn_conv_w, gdn_a_log, gdn_dt_bias, gdn_o_norm, gdn_w_out, fox_w_in, fox_b_f, fox_w_out, rwkv_mu, rwkv_w_r, rwkv_w_k, rwkv_w_v, rwkv_w_g, rwkv_w0, rwkv_w1, rwkv_w2, rwkv_a0, rwkv_a1, rwkv_a2, rwkv_k_k, rwkv_k_a, rwkv_r_k, rwkv_gn_w, rwkv_gn_b, rwkv_w_out):
    _, T, D = x_prompt.shape
    BS, TS, _ = x_sample.shape
    NM = meta.shape[0]
    depth = norm_pre.shape[0]
    assert x_prompt.shape[0] == 1
    seg = (T, BS, TS, NM)
    X = jnp.concatenate([x_prompt[0], x_sample.reshape(BS * TS, D), meta.astype(x_prompt.dtype)], axis=0)

    gdn_out, fox_out, rwkv_out = [], [], []
    for i in range(depth):
        kind, j = i % 3, i // 3
        if kind == 0:
            X, st = _gdn_layer(X, seg, norm_pre[i], norm_post[i], gdn_w_in[j], gdn_conv_w[j], gdn_a_log[j],
                               gdn_dt_bias[j], gdn_o_norm[j], gdn_w_out[j], state_gdn_S[j], state_gdn_conv[j])
            gdn_out.append(st)
        elif kind == 1:
            X, st = _fox_layer(X, seg, norm_pre[i], norm_post[i], fox_w_in[j], fox_b_f[j], fox_w_out[j],
                               (cache_fox_k, cache_fox_v, cache_fox_logf), j)
            fox_out.append(st)
        else:
            X, st = _rwkv_layer(X, seg, norm_pre[i], norm_post[i], rwkv_mu[j], rwkv_w_r[j], rwkv_w_k[j],
                                rwkv_w_v[j], rwkv_w_g[j], rwkv_w0[j], rwkv_w1[j], rwkv_w2[j], rwkv_a0[j],
                                rwkv_a1[j], rwkv_a2[j], rwkv_k_k[j], rwkv_k_a[j], rwkv_r_k[j], rwkv_gn_w[j],
                                rwkv_gn_b[j], rwkv_w_out[j], state_rwkv_S[j], state_rwkv_shift[j])
            rwkv_out.append(st)

    stack = lambda items, n: jnp.stack([it[n] for it in items])
    y_prompt = X[:T][None]
    y_sample = X[T:T + BS * TS].reshape(BS, TS, D)
    return (y_prompt, y_sample,
            stack(gdn_out, 0), stack(gdn_out, 1),
            stack(fox_out, 0), stack(fox_out, 1), stack(fox_out, 2),
            stack(rwkv_out, 0), stack(rwkv_out, 1),
            stack(gdn_out, 2), stack(gdn_out, 3),
            stack(fox_out, 3), stack(fox_out, 4), stack(fox_out, 5),
            stack(rwkv_out, 2), stack(rwkv_out, 3))
```

```python
import functools

import jax
import jax.numpy as jnp
from jax import lax
from jax.experimental import pallas as pl
from jax.experimental.pallas import tpu as pltpu

F32 = jnp.float32
BF16 = jnp.bfloat16

NORM_EPS = 1e-6
L2_EPS = 1e-6
RWKV_GN_EPS = 64e-5
GDN_HEAD = 128
GDN_CONV = 4
FOX_HEAD = 128
RWKV_HEAD = 64
LANES = 128
SUBLANES = 8
CONV_PAD = 8
INV_BASE = 16
STACK = 4
VMEM_LIMIT = 48 * 1024 * 1024


def _cparams(sem):
    return pltpu.CompilerParams(dimension_semantics=sem, vmem_limit_bytes=VMEM_LIMIT)


def _log2(n):
    assert n & (n - 1) == 0
    return n.bit_length() - 1


def _dot(a, b):
    return jnp.dot(a, b, preferred_element_type=F32)


def _mm(a, b):
    return _dot(a.astype(BF16), b.astype(BF16))


def _mm_nt(a, b):
    return lax.dot_general(a.astype(BF16), b.astype(BF16), (((1,), (1,)), ((), ())),
                           preferred_element_type=F32)


def _mm_tn(a, b):
    return lax.dot_general(a.astype(BF16), b.astype(BF16), (((0,), (0,)), ((), ())),
                           preferred_element_type=F32)


def _split3(a):
    hi = a.astype(BF16)
    r = a - hi.astype(F32)
    mid = r.astype(BF16)
    lo = (r - mid.astype(F32)).astype(BF16)
    return hi, mid, lo


def _cumsum_rows(tril, g):
    hi, mid, lo = _split3(g)
    return _dot(tril, hi) + (_dot(tril, mid) + _dot(tril, lo))


def _cumsum_cols(g, triu):
    hi, mid, lo = _split3(g)
    return _dot(hi, triu) + (_dot(mid, triu) + _dot(lo, triu))


def _tri_inv(a_list, diff, eye, L):
    n1 = [jnp.where(diff < INV_BASE, -a, 0.0) for a in a_list]
    p = [eye + n for n in n1]
    npow = n1
    steps = 1
    while 2 * steps < INV_BASE:
        npow = [_mm(x, x) for x in npow]
        p = [pp + _mm(pp, x) for pp, x in zip(p, npow)]
        steps *= 2
    blk = INV_BASE
    while blk < L:
        e = [jnp.where((diff < 2 * blk) & (diff >= blk), a, 0.0) for a in a_list]
        pe = [_mm(pp, ee) for pp, ee in zip(p, e)]
        p = [pp - _mm(x, pp) for pp, x in zip(p, pe)]
        blk *= 2
    return p


def _stack_heads(x, n, w):
    return jnp.concatenate([x[:, j * w:(j + 1) * w] for j in range(n)], axis=0)


def _expand_heads(x, n, L, w):
    t = jnp.concatenate([x] * n, axis=0)
    rh = lax.shift_right_logical(lax.broadcasted_iota(jnp.int32, t.shape, 0), _log2(L))
    ch = lax.shift_right_logical(lax.broadcasted_iota(jnp.int32, t.shape, 1), _log2(w))
    return jnp.where(rh == ch, t, 0.0)


def _silu(x):
    return x * jax.nn.sigmoid(x)


def _norm_proj_kernel(x_ref, g_ref, w_ref, o_ref, *rest):
    h_scr = rest[-1]

    @pl.when(pl.program_id(1) == 0)
    def _():
        x = x_ref[...]
        ms = jnp.mean(x * x, axis=-1, keepdims=True)
        h_scr[...] = (x * lax.rsqrt(ms + NORM_EPS) * g_ref[...]).astype(BF16)

    y = _dot(h_scr[...], w_ref[...])
    o_ref[...] = y
    if len(rest) == 2:
        rest[0][...] = y.astype(BF16)


def norm_proj(x, g, w, tm=512, tn=1024, also_bf16=False):
    T, D = x.shape
    N = w.shape[1]
    tn = min(tn, N)
    assert N % tn == 0
    out_spec = pl.BlockSpec((tm, tn), lambda i, j: (i, j))
    out_specs, out_shape = out_spec, jax.ShapeDtypeStruct((T, N), F32)
    if also_bf16:
        out_specs, out_shape = [out_spec, out_spec], [out_shape, jax.ShapeDtypeStruct((T, N), BF16)]
    return pl.pallas_call(
        _norm_proj_kernel,
        grid=(pl.cdiv(T, tm), N // tn),
        in_specs=[pl.BlockSpec((tm, D), lambda i, j: (i, 0)),
                  pl.BlockSpec((1, D), lambda i, j: (0, 0)),
                  pl.BlockSpec((D, tn), lambda i, j: (0, j))],
        out_specs=out_specs,
        out_shape=out_shape,
        scratch_shapes=[pltpu.VMEM((tm, D), BF16)],
        compiler_params=_cparams(("parallel", "arbitrary")),
        name="norm_proj",
    )(x, g.reshape(1, D), w)


def _norm_kernel(x_ref, g_ref, o_ref):
    x = x_ref[...]
    ms = jnp.mean(x * x, axis=-1, keepdims=True)
    o_ref[...] = x * lax.rsqrt(ms + NORM_EPS) * g_ref[...]


def norm_only(x, g, tm=512):
    T, D = x.shape
    return pl.pallas_call(
        _norm_kernel,
        grid=(pl.cdiv(T, tm),),
        in_specs=[pl.BlockSpec((tm, D), lambda i: (i, 0)), pl.BlockSpec((1, D), lambda i: (0, 0))],
        out_specs=pl.BlockSpec((tm, D), lambda i: (i, 0)),
        out_shape=jax.ShapeDtypeStruct((T, D), F32),
        compiler_params=_cparams(("parallel",)),
        name="norm_only",
    )(x, g.reshape(1, D))


def _mix_proj_kernel(h_ref, p_ref, mu_ref, w_ref, o_ref, l_scr):
    @pl.when(pl.program_id(2) == 0)
    def _():
        h = h_ref[...]
        l_scr[...] = (h + (p_ref[...] - h) * mu_ref[0]).astype(BF16)

    o_ref[...] = _dot(l_scr[...], w_ref[0])


def mix_proj(h, prev, mu, w, tm=512, tn=1024):
    T, D = h.shape
    G, _, N = w.shape
    tn = min(tn, N)
    nj = N // tn
    return pl.pallas_call(
        _mix_proj_kernel,
        grid=(pl.cdiv(T, tm), G, nj),
        in_specs=[pl.BlockSpec((tm, D), lambda i, g, j: (i, 0)),
                  pl.BlockSpec((tm, D), lambda i, g, j: (i, 0)),
                  pl.BlockSpec((1, 1, D), lambda i, g, j: (g, 0, 0)),
                  pl.BlockSpec((1, D, tn), lambda i, g, j: (g, 0, j))],
        out_specs=pl.BlockSpec((tm, tn), lambda i, g, j: (i, g * nj + j)),
        out_shape=jax.ShapeDtypeStruct((T, G * N), F32),
        scratch_shapes=[pltpu.VMEM((tm, D), BF16)],
        compiler_params=_cparams(("parallel", "arbitrary", "arbitrary")),
        name="mix_proj",
    )(h, prev, mu.reshape(G, 1, D), w)


def _lora_kernel(use_tanh, x_ref, w_ref, b_ref, o_ref):
    x = x_ref[...]
    if use_tanh:
        x = jnp.tanh(x)
    o_ref[...] = b_ref[...] + _dot(x.astype(BF16), w_ref[...])


def lora_out(x, col_block, w, b, use_tanh, tm=512):
    T = x.shape[0]
    K, N = w.shape
    return pl.pallas_call(
        functools.partial(_lora_kernel, use_tanh),
        grid=(pl.cdiv(T, tm),),
        in_specs=[pl.BlockSpec((tm, K), lambda i: (i, col_block)),
                  pl.BlockSpec((K, N), lambda i: (0, 0)),
                  pl.BlockSpec((1, N), lambda i: (0, 0))],
        out_specs=pl.BlockSpec((tm, N), lambda i: (i, 0)),
        out_shape=jax.ShapeDtypeStruct((T, N), F32),
        compiler_params=_cparams(("parallel",)),
        name="lora_out",
    )(x, w, b.reshape(1, N))


def _out_proj_kernel(tn, a_ref, w_ref, x_ref, g_ref, o_ref, y_scr):
    j = pl.program_id(1)
    off = pl.multiple_of(j * tn, tn)
    y_scr[:, pl.ds(off, tn)] = _dot(a_ref[...], w_ref[...])

    @pl.when(j == pl.num_programs(1) - 1)
    def _():
        y = y_scr[...]
        ms = jnp.mean(y * y, axis=-1, keepdims=True)
        o_ref[...] = x_ref[...] + y * lax.rsqrt(ms + NORM_EPS) * g_ref[...]


def out_proj(a, w, x, g, tm=512, tn=512):
    T, K = a.shape
    D = w.shape[1]
    return pl.pallas_call(
        functools.partial(_out_proj_kernel, tn),
        grid=(pl.cdiv(T, tm), D // tn),
        in_specs=[pl.BlockSpec((tm, K), lambda i, j: (i, 0)),
                  pl.BlockSpec((K, tn), lambda i, j: (0, j)),
                  pl.BlockSpec((tm, D), lambda i, j: (i, 0)),
                  pl.BlockSpec((1, D), lambda i, j: (0, 0))],
        out_specs=pl.BlockSpec((tm, D), lambda i, j: (i, 0)),
        out_shape=jax.ShapeDtypeStruct((T, D), F32),
        scratch_shapes=[pltpu.VMEM((tm, D), F32)],
        compiler_params=_cparams(("parallel", "arbitrary")),
        name="out_proj",
    )(a, w, x, g.reshape(1, D))


GDN_NU = 2


def _gdn_intra_kernel(L, qn_ref, kn_ref, vn_ref, qp_ref, kp_ref, vp_ref, cq_ref, ck_ref, cv_ref,
                      wq_ref, wk_ref, wv_ref, gc_ref, gr_ref, bc_ref,
                      uv_ref, wqe_ref, kt_ref, qkd_ref, egl_ref, qbuf, kbuf, vbuf):
    c = pl.program_id(2)
    HD, NH = GDN_HEAD, STACK
    R = NH * L
    first = c == 0

    def conv_silu(buf, u_ref, p_ref, st_ref, w_ref):
        buf[CONV_PAD - 3:CONV_PAD, :] = jnp.where(first, st_ref[0], p_ref[SUBLANES - 3:SUBLANES, :])
        buf[CONV_PAD:CONV_PAD + L, :] = u_ref[...]
        acc = buf[CONV_PAD - 3:CONV_PAD - 3 + L, :] * w_ref[0:1, :]
        acc = acc + buf[CONV_PAD - 2:CONV_PAD - 2 + L, :] * w_ref[1:2, :]
        acc = acc + buf[CONV_PAD - 1:CONV_PAD - 1 + L, :] * w_ref[2:3, :]
        acc = acc + buf[CONV_PAD:CONV_PAD + L, :] * w_ref[3:4, :]
        return _silu(acc)

    q = conv_silu(qbuf, qn_ref, qp_ref, cq_ref, wq_ref)
    k = conv_silu(kbuf, kn_ref, kp_ref, ck_ref, wk_ref)
    v = conv_silu(vbuf, vn_ref, vp_ref, cv_ref, wv_ref)

    rl = lax.broadcasted_iota(jnp.int32, (L, L), 0)
    cl = lax.broadcasted_iota(jnp.int32, (L, L), 1)
    tril = jnp.where(cl <= rl, 1.0, 0.0).astype(BF16)
    triu = jnp.where(rl <= cl, 1.0, 0.0).astype(BF16)
    row = lax.broadcasted_iota(jnp.int32, (R, R), 0)
    col = lax.broadcasted_iota(jnp.int32, (R, R), 1)
    diff = row ^ col
    causal = (diff < L) & (col <= row)
    strict = (diff < L) & (col < row)
    eye = jnp.where(row == col, 1.0, 0.0).astype(F32)

    def l2n(x):
        return x * lax.rsqrt(jnp.sum(x * x, axis=-1, keepdims=True) + L2_EPS)

    def col_stack(x):
        return jnp.concatenate([x[:, j:j + 1] for j in range(NH)], axis=0)

    kst, qst, vst, gst, bst, glast_st, grow_st, glast = [], [], [], [], [], [], [], []
    for u in range(GDN_NU):
        qh = [l2n(q[:, (2 * u + i) * HD:(2 * u + i + 1) * HD]) * (HD ** -0.5) for i in range(2)]
        kh = [l2n(k[:, (2 * u + i) * HD:(2 * u + i + 1) * HD]) for i in range(2)]
        kst.append(jnp.concatenate([kh[0], kh[0], kh[1], kh[1]], axis=0))
        qst.append(jnp.concatenate([qh[0], qh[0], qh[1], qh[1]], axis=0))
        vst.append(_stack_heads(v[:, u * NH * HD:(u + 1) * NH * HD], NH, HD))
        gcol = _cumsum_rows(tril, gc_ref[u])
        grow = _cumsum_cols(gr_ref[u, 0], triu)
        gl = gcol[L - 1:L, :]
        glast.append(gl)
        gst.append(col_stack(gcol))
        bst.append(col_stack(bc_ref[u]))
        glast_st.append(jnp.concatenate([jnp.broadcast_to(gl[:, j:j + 1], (L, 1)) for j in range(NH)], axis=0))
        grow_st.append(jnp.concatenate([grow[j:j + 1, :] for j in range(NH)], axis=1))

    U = range(GDN_NU)
    kk = [_mm_nt(kst[u], kst[u]) for u in U]
    qk = [_mm_nt(qst[u], kst[u]) for u in U]
    decay = [jnp.where(causal, jnp.exp(jnp.where(causal, gst[u] - grow_st[u], 0.0)), 0.0) for u in U]
    a = [jnp.where(strict, kk[u] * decay[u] * bst[u], 0.0) for u in U]
    tinv = _tri_inv(a, diff, eye, L)
    rhs = [jnp.concatenate([vst[u] * bst[u], kst[u] * (bst[u] * jnp.exp(gst[u]))], axis=1) for u in U]
    sol = [_mm(tinv[u], rhs[u]) for u in U]
    for u in U:
        uv_ref[0, u] = sol[u][:, :HD]
        wk = sol[u][:, HD:]
        qe = qst[u] * jnp.exp(gst[u])
        wqe_ref[0, u] = jnp.concatenate(
            [x[j * L:(j + 1) * L] for j in range(NH) for x in (wk, qe)], axis=0).astype(BF16)
        kt_ref[0, u] = (kst[u] * jnp.exp(glast_st[u] - gst[u])).astype(BF16)
        qkd_ref[0, u] = (qk[u] * decay[u]).astype(BF16)
        egl_ref[0, u] = jnp.exp(glast[u])


def _gdn_inter_kernel(L, HG, uv_ref, wqe_ref, kt_ref, qkd_ref, egl_ref, z_ref, s0_ref, on_ref,
                      o_ref, so_ref, s_scr):
    c = pl.program_id(1)
    HD, NH = GDN_HEAD, STACK
    R = NH * L

    @pl.when(c == 0)
    def _():
        for g in range(HG):
            s_scr[g] = jnp.concatenate([s0_ref[0, NH * g + j] for j in range(NH)], axis=1)

    rh = lax.shift_right_logical(lax.broadcasted_iota(jnp.int32, (R, NH * HD), 0), _log2(L))
    ch = lax.shift_right_logical(lax.broadcasted_iota(jnp.int32, (R, NH * HD), 1), _log2(HD))
    own = rh == ch

    s_old = [s_scr[g] for g in range(HG)]
    d1 = [[_dot(wqe_ref[0, g, 2 * L * j:2 * L * (j + 1), :], s_old[g][:, j * HD:(j + 1) * HD].astype(BF16))
           for j in range(NH)] for g in range(HG)]
    v_new = [uv_ref[0, g] - jnp.concatenate([d1[g][j][:L] for j in range(NH)], axis=0) for g in range(HG)]
    o_st = [jnp.concatenate([d1[g][j][L:] for j in range(NH)], axis=0) + _dot(qkd_ref[0, g], v_new[g].astype(BF16))
            for g in range(HG)]
    for g in range(HG):
        v_exp = jnp.where(own, jnp.concatenate([v_new[g]] * NH, axis=1), 0.0)
        egl = egl_ref[0, g]
        e_exp = jnp.concatenate([jnp.broadcast_to(egl[:, j:j + 1], (1, HD)) for j in range(NH)], axis=1)
        s_scr[g] = s_old[g] * e_exp + _mm_tn(kt_ref[0, g], v_exp)

    outs = []
    for g in range(HG):
        for j in range(NH):
            o = o_st[g][j * L:(j + 1) * L]
            zj = z_ref[:, (NH * g + j) * HD:(NH * g + j + 1) * HD]
            ms = jnp.mean(o * o, axis=-1, keepdims=True)
            outs.append((o * lax.rsqrt(ms + NORM_EPS) * on_ref[...] * _silu(zj)).astype(BF16))
    o_ref[...] = jnp.concatenate(outs, axis=1)

    @pl.when(c == pl.num_programs(1) - 1)
    def _():
        for g in range(HG):
            for j in range(NH):
                so_ref[0, NH * g + j] = s_scr[g][:, j * HD:(j + 1) * HD]


def gdn_mix(proj, gate_g, gate_b, conv_state, s0, conv_w, o_norm, row0, B, NC, L):
    HV = s0.shape[1]
    HD, NH, NU = GDN_HEAD, STACK, GDN_NU
    HG = HV // NH
    QW = NU * (NH // 2) * HD
    VW = NU * NH * HD
    nq = (HV // 2) * HD // QW
    nv = HV * HD // VW
    Tseg = B * NC * L
    R = NH * L
    rb0 = row0 // L
    assert row0 % L == 0 and L % SUBLANES == 0 and HG % NU == 0

    lane_pad = lambda t: jnp.pad(t, ((0, 0), (0, 0), (0, LANES - NH)))
    gc = lane_pad(gate_g.reshape(Tseg, HG, NH).transpose(1, 0, 2))
    bc = lane_pad(gate_b.reshape(Tseg, HG, NH).transpose(1, 0, 2))
    gr = gate_g.reshape(B * NC, L, HG, NH).transpose(2, 0, 3, 1)
    gr = jnp.pad(gr, ((0, 0), (0, 0), (0, SUBLANES - NH), (0, 0)))
    cq = conv_state[:, :, :nq * QW]
    ck = conv_state[:, :, nq * QW:2 * nq * QW]
    cv = conv_state[:, :, 2 * nq * QW:]
    wq = conv_w[:, :nq * QW]
    wk = conv_w[:, nq * QW:2 * nq * QW]
    wv = conv_w[:, 2 * nq * QW:]

    chunk = lambda b, h, c: rb0 + b * NC + c
    before = lambda b, h, c: jnp.maximum(chunk(b, h, c) * (L // SUBLANES) - 1, 0)
    intra_in = [
        pl.BlockSpec((L, QW), lambda b, h, c: (chunk(b, h, c), h)),
        pl.BlockSpec((L, QW), lambda b, h, c: (chunk(b, h, c), nq + h)),
        pl.BlockSpec((L, VW), lambda b, h, c: (chunk(b, h, c), nv + h)),
        pl.BlockSpec((SUBLANES, QW), lambda b, h, c: (before(b, h, c), h)),
        pl.BlockSpec((SUBLANES, QW), lambda b, h, c: (before(b, h, c), nq + h)),
        pl.BlockSpec((SUBLANES, VW), lambda b, h, c: (before(b, h, c), nv + h)),
        pl.BlockSpec((1, GDN_CONV - 1, QW), lambda b, h, c: (b, 0, h)),
        pl.BlockSpec((1, GDN_CONV - 1, QW), lambda b, h, c: (b, 0, h)),
        pl.BlockSpec((1, GDN_CONV - 1, VW), lambda b, h, c: (b, 0, h)),
        pl.BlockSpec((GDN_CONV, QW), lambda b, h, c: (0, h)),
        pl.BlockSpec((GDN_CONV, QW), lambda b, h, c: (0, h)),
        pl.BlockSpec((GDN_CONV, VW), lambda b, h, c: (0, h)),
        pl.BlockSpec((NU, L, LANES), lambda b, h, c: (h, b * NC + c, 0)),
        pl.BlockSpec((NU, 1, SUBLANES, L), lambda b, h, c: (h, b * NC + c, 0, 0)),
        pl.BlockSpec((NU, L, LANES), lambda b, h, c: (h, b * NC + c, 0)),
    ]
    unit = lambda w: pl.BlockSpec((1, NU, w[0], w[1]), lambda b, h, c: (b * NC + c, h, 0, 0))
    NCH = B * NC
    uv, wqe, kt, qkd, egl = pl.pallas_call(
        functools.partial(_gdn_intra_kernel, L),
        grid=(B, HG // NU, NC),
        in_specs=intra_in,
        out_specs=[unit((R, HD)), unit((2 * R, HD)), unit((R, HD)), unit((R, R)), unit((1, LANES))],
        out_shape=[jax.ShapeDtypeStruct((NCH, HG, R, HD), F32),
                   jax.ShapeDtypeStruct((NCH, HG, 2 * R, HD), BF16),
                   jax.ShapeDtypeStruct((NCH, HG, R, HD), BF16),
                   jax.ShapeDtypeStruct((NCH, HG, R, R), BF16),
                   jax.ShapeDtypeStruct((NCH, HG, 1, LANES), F32)],
        scratch_shapes=[pltpu.VMEM((CONV_PAD + L, QW), F32),
                        pltpu.VMEM((CONV_PAD + L, QW), F32),
                        pltpu.VMEM((CONV_PAD + L, VW), F32)],
        compiler_params=_cparams(("parallel", "parallel", "parallel")),
        name="gdn_intra",
    )(proj, proj, proj, proj, proj, proj, cq, ck, cv, wq, wk, wv, gc, gr, bc)

    allu = lambda w: pl.BlockSpec((1, HG, w[0], w[1]), lambda b, c: (b * NC + c, 0, 0, 0))
    zblk = 2 * nv * VW // (HV * HD)
    o, s_new = pl.pallas_call(
        functools.partial(_gdn_inter_kernel, L, HG),
        grid=(B, NC),
        in_specs=[allu((R, HD)), allu((2 * R, HD)), allu((R, HD)), allu((R, R)), allu((1, LANES)),
                  pl.BlockSpec((L, HV * HD), lambda b, c: (rb0 + b * NC + c, zblk)),
                  pl.BlockSpec((1, HV, HD, HD), lambda b, c: (b, 0, 0, 0)),
                  pl.BlockSpec((1, HD), lambda b, c: (0, 0))],
        out_specs=[pl.BlockSpec((L, HV * HD), lambda b, c: (b * NC + c, 0)),
                   pl.BlockSpec((1, HV, HD, HD), lambda b, c: (b, 0, 0, 0))],
        out_shape=[jax.ShapeDtypeStruct((Tseg, HV * HD), BF16),
                   jax.ShapeDtypeStruct((B, HV, HD, HD), F32)],
        scratch_shapes=[pltpu.VMEM((HG, HD, NH * HD), F32)],
        compiler_params=_cparams(("parallel", "arbitrary")),
        name="gdn_inter",
    )(uv, wqe, kt, qkd, egl, proj, s0, o_norm.reshape(1, HD))
    return o, s_new


FOX_NH = 2
FOX_TP = 512
LOG2E = 1.4426950408889634
FOX_VMEM_LIMIT = 56 * 1024 * 1024


def _fox_prompt_kernel(TQ, scale, q_ref, k_ref, v_ref, km_ref, vm_ref, z_ref, ck_ref, cm_ref, o_ref):
    qi = pl.program_id(1)
    HD = FOX_HEAD
    HS = range(FOX_NH)
    hs = lambda i: slice(i * HD, (i + 1) * HD)
    q = [(q_ref[:, hs(i)] * (scale * LOG2E)).astype(BF16) for i in HS]

    s = [_mm_nt(q[i], km_ref[:, hs(i)]) - cm_ref[i] for i in HS]
    m = [jnp.max(s[i], axis=-1, keepdims=True) for i in HS]
    p = [jnp.exp2(s[i] - m[i]) for i in HS]
    l = [jnp.sum(p[i], axis=-1, keepdims=True) for i in HS]
    acc = [_mm(p[i], vm_ref[:, hs(i)]) for i in HS]

    def step(kb, carry, diagonal):
        m, l, acc = carry
        off = pl.multiple_of(kb * TQ, TQ)
        s = [_mm_nt(q[i], k_ref[pl.ds(off, TQ), hs(i)]) - ck_ref[i, :, pl.ds(off, TQ)] for i in HS]
        if diagonal:
            row = lax.broadcasted_iota(jnp.int32, (TQ, TQ), 0)
            col = lax.broadcasted_iota(jnp.int32, (TQ, TQ), 1)
            s = [jnp.where(col <= row, s[i], -jnp.inf) for i in HS]
        m_new = [jnp.maximum(m[i], jnp.max(s[i], axis=-1, keepdims=True)) for i in HS]
        alpha = [jnp.exp2(m[i] - m_new[i]) for i in HS]
        p = [jnp.exp2(s[i] - m_new[i]) for i in HS]
        l = [alpha[i] * l[i] + jnp.sum(p[i], axis=-1, keepdims=True) for i in HS]
        acc = [alpha[i] * acc[i] + _mm(p[i], v_ref[pl.ds(off, TQ), hs(i)]) for i in HS]
        return tuple(m_new), tuple(l), tuple(acc)

    carry = lax.fori_loop(0, qi, lambda kb, cr: step(kb, cr, False), (tuple(m), tuple(l), tuple(acc)))
    m, l, acc = step(qi, carry, True)
    o_ref[...] = jnp.concatenate([(acc[i] / l[i]) * _silu(z_ref[:, hs(i)]) for i in HS], axis=1).astype(BF16)


def fox_prompt(qz, kv16, c_main, c_meta, T, meta_row0, n_meta, H, TQ=512):
    HD = FOX_HEAD
    W = FOX_NH * HD
    HP = H // FOX_NH
    TQ = min(TQ, T)
    assert T % TQ == 0 and meta_row0 % n_meta == 0 and H % FOX_NH == 0
    mb = meta_row0 // n_meta
    return pl.pallas_call(
        functools.partial(_fox_prompt_kernel, TQ, HD ** -0.5),
        grid=(HP, T // TQ),
        in_specs=[pl.BlockSpec((TQ, W), lambda h, i: (i, h)),
                  pl.BlockSpec((T, W), lambda h, i: (0, h)),
                  pl.BlockSpec((T, W), lambda h, i: (0, HP + h)),
                  pl.BlockSpec((n_meta, W), lambda h, i: (mb, h)),
                  pl.BlockSpec((n_meta, W), lambda h, i: (mb, HP + h)),
                  pl.BlockSpec((TQ, W), lambda h, i: (i, HP + h)),
                  pl.BlockSpec((FOX_NH, 1, T), lambda h, i: (h, 0, 0)),
                  pl.BlockSpec((FOX_NH, 1, n_meta), lambda h, i: (h, 0, 0))],
        out_specs=pl.BlockSpec((TQ, W), lambda h, i: (i, h)),
        out_shape=jax.ShapeDtypeStruct((T, H * HD), BF16),
        compiler_params=pltpu.CompilerParams(dimension_semantics=("parallel", "arbitrary"),
                                             vmem_limit_bytes=FOX_VMEM_LIMIT),
        name="fox_prompt",
    )(qz, kv16, kv16, kv16, kv16, qz, c_main, c_meta)


def _fox_seq_kernel(has_cache, scale, q_ref, k_ref, v_ref, z_ref, cn_ref, *rest):
    if has_cache:
        kc_ref, vc_ref, cc_ref, o_ref, m_scr, l_scr, a_scr = rest
    else:
        o_ref, m_scr, l_scr, a_scr = rest
    c = pl.program_id(1)
    TQ = q_ref.shape[0]
    HD = FOX_HEAD
    H = q_ref.shape[1] // HD
    HS = range(H)
    hs = lambda h: slice(h * HD, (h + 1) * HD)
    q = [(q_ref[:, hs(h)] * (scale * LOG2E)).astype(BF16) for h in HS]

    @pl.when(c == 0)
    def _():
        row = lax.broadcasted_iota(jnp.int32, (TQ, TQ), 0)
        col = lax.broadcasted_iota(jnp.int32, (TQ, TQ), 1)
        s = [jnp.where(col <= row, _mm_nt(q[h], k_ref[:, hs(h)]) - cn_ref[0, h], -jnp.inf) for h in HS]
        m = [jnp.max(s[h], axis=-1, keepdims=True) for h in HS]
        p = [jnp.exp2(s[h] - m[h]) for h in HS]
        for h in HS:
            m_scr[h] = m[h]
            l_scr[h] = jnp.sum(p[h], axis=-1, keepdims=True)
            a_scr[h] = _mm(p[h], v_ref[:, hs(h)])

    if has_cache:
        TP = cc_ref.shape[-1]
        s = [_mm_nt(q[h], kc_ref[pl.ds(h, TP, stride=H), :]) - cc_ref[0, h] for h in HS]
        m_old = [m_scr[h] for h in HS]
        l_old = [l_scr[h] for h in HS]
        a_old = [a_scr[h] for h in HS]
        m_new = [jnp.maximum(m_old[h], jnp.max(s[h], axis=-1, keepdims=True)) for h in HS]
        alpha = [jnp.exp2(m_old[h] - m_new[h]) for h in HS]
        p = [jnp.exp2(s[h] - m_new[h]) for h in HS]
        pv = [_mm(p[h], vc_ref[pl.ds(h, TP, stride=H), :]) for h in HS]
        for h in HS:
            m_scr[h] = m_new[h]
            l_scr[h] = alpha[h] * l_old[h] + jnp.sum(p[h], axis=-1, keepdims=True)
            a_scr[h] = alpha[h] * a_old[h] + pv[h]

    @pl.when(c == pl.num_programs(1) - 1)
    def _():
        o_ref[...] = jnp.concatenate([(a_scr[h] / l_scr[h]) * _silu(z_ref[:, hs(h)]) for h in HS],
                                     axis=1).astype(BF16)


def fox_seq(qz, kv16, c_new, row0, B, TQ, H, cache=None):
    HD = FOX_HEAD
    FD = H * HD
    assert row0 % TQ == 0
    rb0 = row0 // TQ
    tok = lambda sec: pl.BlockSpec((TQ, FD), lambda b, c: (rb0 + b, sec))
    in_specs = [tok(0), tok(0), tok(1), tok(1), pl.BlockSpec((1, H, 1, TQ), lambda b, c: (b, 0, 0, 0))]
    args = [qz, kv16, kv16, qz, c_new]
    nck = 1
    if cache is not None:
        kc, vc, cc, b0 = cache
        P = cc.shape[-1]
        TP = min(FOX_TP, P)
        assert P % TP == 0
        nck = P // TP
        blk = pl.BlockSpec((TP * H, HD), lambda b, c: ((b0 + b) * nck + c, 0))
        in_specs += [blk, blk, pl.BlockSpec((1, H, 1, TP), lambda b, c: (b, 0, 0, c))]
        args += [kc, vc, cc]
    return pl.pallas_call(
        functools.partial(_fox_seq_kernel, cache is not None, HD ** -0.5),
        grid=(B, nck),
        in_specs=in_specs,
        out_specs=pl.BlockSpec((TQ, FD), lambda b, c: (b, 0)),
        out_shape=jax.ShapeDtypeStruct((B * TQ, FD), BF16),
        scratch_shapes=[pltpu.VMEM((H, TQ, 1), F32), pltpu.VMEM((H, TQ, 1), F32), pltpu.VMEM((H, TQ, HD), F32)],
        compiler_params=_cparams(("parallel", "arbitrary")),
        name="fox_seq",
    )(*args)


WKV_NU = 2


def _wkv_intra_kernel(L, r_ref, k_ref, v_ref, wp_ref, ap_ref, kk_ref, ka_ref, rk_ref,
                      x_ref, u0_ref, o0_ref, arb_ref, vk_ref, bh_ref, gl_ref, bonus_ref):
    HD, NH = RWKV_HEAD, STACK
    R = NH * L
    W = NH * HD

    rl = lax.broadcasted_iota(jnp.int32, (L, L), 0)
    cl = lax.broadcasted_iota(jnp.int32, (L, L), 1)
    tril = jnp.where(cl <= rl, 1.0, 0.0).astype(BF16)
    row = lax.broadcasted_iota(jnp.int32, (R, R), 0)
    col = lax.broadcasted_iota(jnp.int32, (R, R), 1)
    diff = row ^ col
    causal = (diff < L) & (col <= row)
    strict = (diff < L) & (col < row)
    eye = jnp.where(row == col, 1.0, 0.0).astype(F32)

    heads = [slice(j * HD, (j + 1) * HD) for j in range(WKV_NU * NH)]
    r = r_ref[...]
    v = v_ref[...]
    k_raw = k_ref[...]
    wp = -wp_ref[...]
    w = -(jnp.maximum(wp, 0.0) + jnp.log(1.0 + jnp.exp(-jnp.abs(wp)))) - 0.5
    lw = -jnp.exp(w)
    a_sig = jax.nn.sigmoid(ap_ref[...])
    kk = k_raw * kk_ref[...]
    kk = jnp.concatenate(
        [kk[:, h] * lax.rsqrt(jnp.sum(kk[:, h] * kk[:, h], axis=-1, keepdims=True) + L2_EPS) for h in heads], axis=1)
    k = k_raw * (1.0 + (a_sig - 1.0) * ka_ref[...])
    a = -kk
    b = kk * a_sig
    rk = r * k * rk_ref[...]
    bonus_ref[...] = jnp.concatenate([jnp.sum(rk[:, h], axis=-1, keepdims=True) * v[:, h] for h in heads], axis=1)

    cum = _cumsum_rows(tril, lw)
    clast = cum[L - 1:L, :]
    e_neg = jnp.exp(-cum)
    e_tail = jnp.exp(clast - cum)
    rt = r * jnp.exp(cum)
    at = a * jnp.exp(cum - lw)
    kt = k * e_neg
    bt = b * e_neg
    khat = k * e_tail
    bhat = b * e_tail
    gl_ref[0] = jnp.exp(clast)

    U = range(WKV_NU)
    sl = lambda x, u: x[:, u * W:(u + 1) * W]
    st = lambda x, u: _stack_heads(sl(x, u), NH, HD)
    v_st = [st(v, u) for u in U]
    aa = [_mm_nt(jnp.concatenate([st(at, u), st(rt, u)], axis=0),
                 jnp.concatenate([st(bt, u), st(kt, u)], axis=0)) for u in U]
    a_ab = [jnp.where(strict, aa[u][:R, :R], 0.0) for u in U]
    a_ak = [jnp.where(strict, aa[u][:R, R:], 0.0) for u in U]
    a_rb = [jnp.where(causal, aa[u][R:, :R], 0.0) for u in U]
    a_rk = [jnp.where(causal, aa[u][R:, R:], 0.0) for u in U]
    av = [_mm(jnp.concatenate([a_ak[u], a_rk[u]], axis=0), v_st[u]) for u in U]
    vk = [_mm_tn(v_st[u], _expand_heads(sl(khat, u), NH, L, HD)) for u in U]
    tinv = _tri_inv([-x for x in a_ab], diff, eye, L)
    w_exp = [_mm(tinv[u], _expand_heads(sl(at, u), NH, L, HD)) for u in U]
    u0 = [_mm(tinv[u], av[u][:R]) for u in U]
    for u in U:
        x_ref[0, u] = jnp.concatenate([w_exp[u], _expand_heads(sl(rt, u), NH, L, HD)], axis=0).astype(BF16)
        u0_ref[0, u] = u0[u]
        o0_ref[0, u] = av[u][R:]
        arb_ref[0, u] = a_rb[u].astype(BF16)
        vk_ref[0, u] = vk[u]
        bh_ref[0, u] = _expand_heads(sl(bhat, u), NH, L, HD).astype(BF16)


def _wkv_inter_kernel(L, HG, x_ref, u0_ref, o0_ref, arb_ref, vk_ref, bh_ref, gl_ref, s0_ref,
                      bonus_ref, gate_ref, gnw_ref, gnb_ref, o_ref, so_ref, s_scr):
    c = pl.program_id(1)
    HD, NH = RWKV_HEAD, STACK
    R = NH * L
    W = NH * HD

    @pl.when(c == 0)
    def _():
        for g in range(HG):
            s_scr[g] = jnp.concatenate([s0_ref[0, NH * g + j] for j in range(NH)], axis=1)

    s_old = [s_scr[g] for g in range(HG)]
    d1 = [_mm_nt(x_ref[0, g], s_old[g]) for g in range(HG)]
    u_st = [u0_ref[0, g] + d1[g][:R] for g in range(HG)]
    o_st = [d1[g][R:] + o0_ref[0, g] + _dot(arb_ref[0, g], u_st[g].astype(BF16)) for g in range(HG)]
    for g in range(HG):
        s_scr[g] = s_old[g] * gl_ref[0, :, g * W:(g + 1) * W] + vk_ref[0, g] + _mm_tn(u_st[g], bh_ref[0, g])
    outs = []
    for g in range(HG):
        mean = jnp.mean(o_st[g], axis=-1, keepdims=True)
        d = o_st[g] - mean
        var = jnp.mean(d * d, axis=-1, keepdims=True)
        on = d * lax.rsqrt(var + RWKV_GN_EPS)
        outs += [on[j * L:(j + 1) * L] for j in range(NH)]
    o = jnp.concatenate(outs, axis=1) * gnw_ref[...] + gnb_ref[...] + bonus_ref[...]
    o_ref[...] = (o * _silu(gate_ref[...])).astype(BF16)

    @pl.when(c == pl.num_programs(1) - 1)
    def _():
        for g in range(HG):
            for j in range(NH):
                so_ref[0, NH * g + j] = s_scr[g][:, j * HD:(j + 1) * HD]


def wkv_mix(big, w_pre, a_pre, k_k, k_a, r_k, gn_w, gn_b, s0, row0, B, NC, L):
    HN = s0.shape[1]
    HD, NH, NU = RWKV_HEAD, STACK, WKV_NU
    HG = HN // NH
    W = NH * HD
    R = NH * L
    D = HN * HD
    Tseg = B * NC * L
    NCH = B * NC
    rb0 = row0 // L
    nsec = D // (NU * W)
    assert row0 % L == 0 and HG % NU == 0
    tok = lambda sec: pl.BlockSpec((L, NU * W), lambda bb, h, c: (rb0 + bb * NC + c, sec * nsec + h))
    par = pl.BlockSpec((1, NU * W), lambda bb, h, c: (0, h))
    unit = lambda w: pl.BlockSpec((1, NU, w[0], w[1]), lambda bb, h, c: (bb * NC + c, h, 0, 0))
    x, u0, o0, arb, vk, bh, gl, bonus = pl.pallas_call(
        functools.partial(_wkv_intra_kernel, L),
        grid=(B, HG // NU, NC),
        in_specs=[tok(0), tok(1), tok(2), tok(0), tok(0), par, par, par],
        out_specs=[unit((2 * R, W)), unit((R, HD)), unit((R, HD)), unit((R, R)), unit((HD, W)), unit((R, W)),
                   pl.BlockSpec((1, 1, NU * W), lambda bb, h, c: (bb * NC + c, 0, h)),
                   pl.BlockSpec((L, NU * W), lambda bb, h, c: (bb * NC + c, h))],
        out_shape=[jax.ShapeDtypeStruct((NCH, HG, 2 * R, W), BF16),
                   jax.ShapeDtypeStruct((NCH, HG, R, HD), F32),
                   jax.ShapeDtypeStruct((NCH, HG, R, HD), F32),
                   jax.ShapeDtypeStruct((NCH, HG, R, R), BF16),
                   jax.ShapeDtypeStruct((NCH, HG, HD, W), F32),
                   jax.ShapeDtypeStruct((NCH, HG, R, W), BF16),
                   jax.ShapeDtypeStruct((NCH, 1, D), F32),
                   jax.ShapeDtypeStruct((Tseg, D), F32)],
        compiler_params=_cparams(("parallel", "parallel", "parallel")),
        name="wkv_intra",
    )(big, big, big, w_pre, a_pre, k_k.reshape(1, D), k_a.reshape(1, D), r_k.reshape(1, D))

    allu = lambda w: pl.BlockSpec((1, HG, w[0], w[1]), lambda bb, c: (bb * NC + c, 0, 0, 0))
    o, s_new = pl.pallas_call(
        functools.partial(_wkv_inter_kernel, L, HG),
        grid=(B, NC),
        in_specs=[allu((2 * R, W)), allu((R, HD)), allu((R, HD)), allu((R, R)), allu((HD, W)), allu((R, W)),
                  pl.BlockSpec((1, 1, D), lambda bb, c: (bb * NC + c, 0, 0)),
                  pl.BlockSpec((1, HN, HD, HD), lambda bb, c: (bb, 0, 0, 0)),
                  pl.BlockSpec((L, D), lambda bb, c: (bb * NC + c, 0)),
                  pl.BlockSpec((L, D), lambda bb, c: (rb0 + bb * NC + c, 3)),
                  pl.BlockSpec((1, D), lambda bb, c: (0, 0)),
                  pl.BlockSpec((1, D), lambda bb, c: (0, 0))],
        out_specs=[pl.BlockSpec((L, D), lambda bb, c: (bb * NC + c, 0)),
                   pl.BlockSpec((1, HN, HD, HD), lambda bb, c: (bb, 0, 0, 0))],
        out_shape=[jax.ShapeDtypeStruct((Tseg, D), BF16),
                   jax.ShapeDtypeStruct((B, HN, HD, HD), F32)],
        scratch_shapes=[pltpu.VMEM((HG, HD, W), F32)],
        compiler_params=_cparams(("parallel", "arbitrary")),
        name="wkv_inter",
    )(x, u0, o0, arb, vk, bh, gl, s0, bonus, big, gn_w.reshape(1, D), gn_b.reshape(1, D))
    return o, s_new


def _pad_cols(w, n):
    return jnp.pad(w, ((0, 0), (0, n - w.shape[1])))


def _gdn_layer(X, seg, g_pre, g_post, w_in, conv_w, a_log, dt_bias, o_norm, w_out, st_S, st_conv):
    (T, BS, TS, NM) = seg
    HV = a_log.shape[0]
    CD = conv_w.shape[1]
    VD = HV * GDN_HEAD
    w_main = w_in[:, :CD + VD].astype(BF16)
    w_gate = _pad_cols(w_in[:, CD + VD:], LANES).astype(BF16)
    proj = norm_proj(X, g_pre, w_main)
    gates = norm_proj(X, g_pre, w_gate)
    beta = jax.nn.sigmoid(gates[:, :HV])
    g = -jnp.exp(a_log) * jax.nn.softplus(gates[:, HV:2 * HV] + dt_bias)

    r_s, r_m = T, T + BS * TS
    tail = GDN_CONV - 1
    c_m = proj[r_m + NM - tail:r_m + NM, :CD][None]
    c_p = proj[T - tail:T, :CD][None]
    c_s = proj[r_s:r_m, :CD].reshape(BS, TS, CD)[:, TS - tail:]
    z_conv = jnp.zeros((1, tail, CD), F32)
    z_S = jnp.zeros((1, HV, GDN_HEAD, GDN_HEAD), F32)
    o_m, S_m = gdn_mix(proj, g[r_m:], beta[r_m:], z_conv, z_S, conv_w, o_norm, r_m, 1, 1, NM)
    LP = 64
    o_p, S_p = gdn_mix(proj, g[:T], beta[:T], c_m, S_m, conv_w, o_norm, 0, 1, T // LP, LP)
    o_s, S_s = gdn_mix(proj, g[r_s:r_m], beta[r_s:r_m], st_conv, st_S, conv_w, o_norm, r_s, BS, 1, TS)
    o = jnp.concatenate([o_p, o_s, o_m], axis=0)
    X = out_proj(o, w_out.astype(BF16), X, g_post)
    return X, (S_p, c_p, S_s, c_s)


def _fox_layer(X, seg, g_pre, g_post, w_in, b_f, w_out, caches, j):
    (T, BS, TS, NM) = seg
    cache_k, cache_v, cache_logf = caches
    H = b_f.shape[0]
    FD = H * FOX_HEAD
    qz = norm_proj(X, g_pre, jnp.concatenate([w_in[:, :FD], w_in[:, 3 * FD:4 * FD]], axis=1).astype(BF16))
    kv, kv16 = norm_proj(X, g_pre, w_in[:, FD:3 * FD].astype(BF16), also_bf16=True)
    gates = norm_proj(X, g_pre, _pad_cols(w_in[:, 4 * FD:], LANES).astype(BF16))
    logf = jax.nn.log_sigmoid(gates[:, :H] + b_f)
    r_s, r_m = T, T + BS * TS

    c_p = jnp.cumsum(jnp.concatenate([logf[r_m:], logf[:T]], axis=0), axis=0) * LOG2E
    c_meta = c_p[:NM].T.reshape(H, 1, NM)
    c_main = c_p[NM:].T.reshape(H, 1, T)
    P = cache_k.shape[2]
    c_s = jnp.cumsum(jnp.concatenate([cache_logf[j], logf[r_s:r_m].reshape(BS, TS, H)], axis=1), axis=1) * LOG2E
    c_s = c_s.transpose(0, 2, 1)
    c_cache = c_s[:, :, :P].reshape(BS, H, 1, P)
    c_new = c_s[:, :, P:].reshape(BS, H, 1, TS)

    o_p = fox_prompt(qz, kv16, c_main, c_meta, T, r_m, NM, H)
    o_m = fox_seq(qz, kv16, c_meta.reshape(1, H, 1, NM), r_m, 1, NM, H)
    o_s = fox_seq(qz, kv16, c_new, r_s, BS, TS, H,
                  cache=(cache_k.reshape(-1, FOX_HEAD), cache_v.reshape(-1, FOX_HEAD), c_cache, j * BS))
    o = jnp.concatenate([o_p, o_s, o_m], axis=0)
    X = out_proj(o, w_out.astype(BF16), X, g_post)

    def seq(a, n):
        return jnp.concatenate([a[r_m:], a[:T]], axis=0).reshape(1, NM + T, H, n)

    k_all = kv[:, :FD]
    v_all = kv[:, FD:]
    outs = (seq(k_all, FOX_HEAD), seq(v_all, FOX_HEAD),
            jnp.concatenate([logf[r_m:], logf[:T]], axis=0).reshape(1, NM + T, H),
            k_all[r_s:r_m].reshape(BS, TS, H, FOX_HEAD), v_all[r_s:r_m].reshape(BS, TS, H, FOX_HEAD),
            logf[r_s:r_m].reshape(BS, TS, H))
    return X, outs


def _rwkv_layer(X, seg, g_pre, g_post, mu, w_r, w_k, w_v, w_g, w0, w1, w2, a0, a1, a2, k_k, k_a, r_k,
                gn_w, gn_b, w_out, st_S, st_shift):
    (T, BS, TS, NM) = seg
    D = X.shape[1]
    HN = r_k.shape[0]
    HD = RWKV_HEAD
    r_s, r_m = T, T + BS * TS
    h = norm_only(X, g_pre)
    h_s = h[r_s:r_m].reshape(BS, TS, D)
    prev = jnp.concatenate([
        h[r_m + NM - 1:r_m + NM], h[:T - 1],
        jnp.concatenate([st_shift[:, None, :], h_s[:, :-1]], axis=1).reshape(BS * TS, D),
        jnp.zeros((1, D), F32), h[r_m:r_m + NM - 1]], axis=0)

    big = mix_proj(h, prev, mu[jnp.array([0, 2, 3, 5])], jnp.stack([w_r, w_k, w_v, w_g]).astype(BF16))
    lora_w = jnp.stack([_pad_cols(w1, LANES), _pad_cols(a1, LANES)]).astype(BF16)
    lo = mix_proj(h, prev, mu[jnp.array([1, 4])], lora_w)
    pad_rows = lambda w: jnp.pad(w, ((0, LANES - w.shape[0]), (0, 0))).astype(BF16)
    w_pre = lora_out(lo, 0, pad_rows(w2), w0, True)
    a_pre = lora_out(lo, 1, pad_rows(a2), a0, False)

    zS = jnp.zeros((1, HN, HD, HD), F32)
    par = (k_k, k_a, r_k, gn_w, gn_b)
    o_m, S_m = wkv_mix(big, w_pre, a_pre, *par, zS, r_m, 1, 1, NM)
    LP = 64
    o_p, S_p = wkv_mix(big, w_pre, a_pre, *par, S_m, 0, 1, T // LP, LP)
    o_s, S_s = wkv_mix(big, w_pre, a_pre, *par, st_S, r_s, BS, 1, TS)
    o = jnp.concatenate([o_p, o_s, o_m], axis=0)
    X = out_proj(o, w_out.astype(BF16), X, g_post)
    return X, (S_p, h[T - 1:T], S_s, h_s[:, -1])


def kernel(x_prompt, x_sample, state_gdn_S, state_gdn_conv, cache_fox_k, cache_fox_v, cache_fox_logf, state_rwkv_S, state_rwkv_shift, meta, norm_pre, norm_post, gdn_w_in, gdn_conv_w, gdn_a_log, gdn_dt_bias, gdn_o_norm, gdn_w_out, fox_w_in, fox_b_f, fox_w_out, rwkv_mu, rwkv_w_r, rwkv_w_k, rwkv_w_v, rwkv_w_g, rwkv_w0, rwkv_w1, rwkv_w2, rwkv_a0, rwkv_a1, rwkv_a2, rwkv_k_k, rwkv_k_a, rwkv_r_k, rwkv_gn_w, rwkv_gn_b, rwkv_w_out):
    _, T, D = x_prompt.shape
    BS, TS, _ = x_sample.shape
    NM = meta.shape[0]
    depth = norm_pre.shape[0]
    assert x_prompt.shape[0] == 1
    seg = (T, BS, TS, NM)
    X = jnp.concatenate([x_prompt[0], x_sample.reshape(BS * TS, D), meta.astype(x_prompt.dtype)], axis=0)

    gdn_out, fox_out, rwkv_out = [], [], []
    for i in range(depth):
        kind, j = i % 3, i // 3
        if kind == 0:
            X, st = _gdn_layer(X, seg, norm_pre[i], norm_post[i], gdn_w_in[j], gdn_conv_w[j], gdn_a_log[j],
                               gdn_dt_bias[j], gdn_o_norm[j], gdn_w_out[j], state_gdn_S[j], state_gdn_conv[j])
            gdn_out.append(st)
        elif kind == 1:
            X, st = _fox_layer(X, seg, norm_pre[i], norm_post[i], fox_w_in[j], fox_b_f[j], fox_w_out[j],
                               (cache_fox_k, cache_fox_v, cache_fox_logf), j)
            fox_out.append(st)
        else:
            X, st = _rwkv_layer(X, seg, norm_pre[i], norm_post[i], rwkv_mu[j], rwkv_w_r[j], rwkv_w_k[j],
                                rwkv_w_v[j], rwkv_w_g[j], rwkv_w0[j], rwkv_w1[j], rwkv_w2[j], rwkv_a0[j],
                                rwkv_a1[j], rwkv_a2[j], rwkv_k_k[j], rwkv_k_a[j], rwkv_r_k[j], rwkv_gn_w[j],
                                rwkv_gn_b[j], rwkv_w_out[j], state_rwkv_S[j], state_rwkv_shift[j])
            rwkv_out.append(st)

    stack = lambda items, n: jnp.stack([it[n] for it in items])
    y_prompt = X[:T][None]
    y_sample = X[T:T + BS * TS].reshape(BS, TS, D)
    return (y_prompt, y_sample,
            stack(gdn_out, 0), stack(gdn_out, 1),
            stack(fox_out, 0), stack(fox_out, 1), stack(fox_out, 2),
            stack(rwkv_out, 0), stack(rwkv_out, 1),
            stack(gdn_out, 2), stack(gdn_out, 3),
            stack(fox_out, 3), stack(fox_out, 4), stack(fox_out, 5),
            stack(rwkv_out, 2), stack(rwkv_out, 3))
```

```python
import functools

import jax
import jax.numpy as jnp
from jax import lax
from jax.experimental import pallas as pl
from jax.experimental.pallas import tpu as pltpu

F32 = jnp.float32
BF16 = jnp.bfloat16

NORM_EPS = 1e-6
L2_EPS = 1e-6
RWKV_GN_EPS = 64e-5
GDN_HEAD = 128
GDN_CONV = 4
FOX_HEAD = 128
RWKV_HEAD = 64
LANES = 128
SUBLANES = 8
CONV_PAD = 8
INV_BASE = 16
STACK = 4
VMEM_LIMIT = 48 * 1024 * 1024
VMEM_LIMIT_BIG = 56 * 1024 * 1024


def _cparams(sem, limit=VMEM_LIMIT):
    return pltpu.CompilerParams(dimension_semantics=sem, vmem_limit_bytes=limit)


def _log2(n):
    assert n & (n - 1) == 0
    return n.bit_length() - 1


def _dot(a, b):
    return jnp.dot(a, b, preferred_element_type=F32)


def _mm(a, b):
    return _dot(a.astype(BF16), b.astype(BF16))


def _mm_nt(a, b):
    return lax.dot_general(a.astype(BF16), b.astype(BF16), (((1,), (1,)), ((), ())),
                           preferred_element_type=F32)


def _mm_tn(a, b):
    return lax.dot_general(a.astype(BF16), b.astype(BF16), (((0,), (0,)), ((), ())),
                           preferred_element_type=F32)


def _split3(a):
    hi = a.astype(BF16)
    r = a - hi.astype(F32)
    mid = r.astype(BF16)
    lo = (r - mid.astype(F32)).astype(BF16)
    return hi, mid, lo


def _cumsum_rows(tril, g):
    hi, mid, lo = _split3(g)
    return _dot(tril, hi) + (_dot(tril, mid) + _dot(tril, lo))


def _cumsum_cols_t(g, triu):
    hi, mid, lo = _split3(g)
    return _mm_tn(hi, triu) + (_mm_tn(mid, triu) + _mm_tn(lo, triu))


def _tri_inv(a_list, diff, eye, L):
    n1 = [jnp.where(diff < INV_BASE, -a, 0.0) for a in a_list]
    p = [eye + n for n in n1]
    npow = n1
    steps = 1
    while 2 * steps < INV_BASE:
        npow = [_mm(x, x) for x in npow]
        p = [pp + _mm(pp, x) for pp, x in zip(p, npow)]
        steps *= 2
    blk = INV_BASE
    while blk < L:
        e = [jnp.where((diff < 2 * blk) & (diff >= blk), a, 0.0) for a in a_list]
        pe = [_mm(pp, ee) for pp, ee in zip(p, e)]
        p = [pp - _mm(x, pp) for pp, x in zip(p, pe)]
        blk *= 2
    return p


def _stack_heads(x, n, w):
    return jnp.concatenate([x[:, j * w:(j + 1) * w] for j in range(n)], axis=0)


def _expand_heads(x, n, L, w):
    t = jnp.concatenate([x] * n, axis=0)
    rh = lax.shift_right_logical(lax.broadcasted_iota(jnp.int32, t.shape, 0), _log2(L))
    ch = lax.shift_right_logical(lax.broadcasted_iota(jnp.int32, t.shape, 1), _log2(w))
    return jnp.where(rh == ch, t, 0.0)


def _silu(x):
    return x * jax.nn.sigmoid(x)


NORM_PROJ_ROWS = 1600
ROW_TILE_CAP = 544


def _row_tile(T, cap):
    best = 0
    for t in range(16, cap + 1, 16):
        if T % t == 0:
            best = t
    return best if best >= cap // 4 else (cap // 16) * 16


def _norm_proj_kernel(x_ref, g_ref, w_ref, o_ref, *rest):
    h_scr = rest[-1]

    @pl.when(pl.program_id(1) == 0)
    def _():
        x = x_ref[...]
        ms = jnp.mean(x * x, axis=-1, keepdims=True)
        h_scr[...] = (x * lax.rsqrt(ms + NORM_EPS) * g_ref[...]).astype(BF16)

    y = _dot(h_scr[...], w_ref[...])
    o_ref[...] = y
    if len(rest) == 2:
        rest[0][...] = y.astype(BF16)


def norm_proj(x, g, w, tn=512, also_bf16=False):
    T, D = x.shape
    N = w.shape[1]
    tn = min(tn, N)
    tm = _row_tile(T, NORM_PROJ_ROWS)
    assert N % tn == 0
    out_spec = pl.BlockSpec((tm, tn), lambda i, j: (i, j))
    out_specs, out_shape = out_spec, jax.ShapeDtypeStruct((T, N), F32)
    if also_bf16:
        out_specs, out_shape = [out_spec, out_spec], [out_shape, jax.ShapeDtypeStruct((T, N), BF16)]
    return pl.pallas_call(
        _norm_proj_kernel,
        grid=(pl.cdiv(T, tm), N // tn),
        in_specs=[pl.BlockSpec((tm, D), lambda i, j: (i, 0)),
                  pl.BlockSpec((1, D), lambda i, j: (0, 0)),
                  pl.BlockSpec((D, tn), lambda i, j: (0, j))],
        out_specs=out_specs,
        out_shape=out_shape,
        scratch_shapes=[pltpu.VMEM((tm, D), BF16)],
        compiler_params=_cparams(("parallel", "arbitrary"), VMEM_LIMIT_BIG),
        name="norm_proj",
    )(x, g.reshape(1, D), w)


def _norm_kernel(x_ref, g_ref, o_ref):
    x = x_ref[...]
    ms = jnp.mean(x * x, axis=-1, keepdims=True)
    o_ref[...] = x * lax.rsqrt(ms + NORM_EPS) * g_ref[...]


def norm_only(x, g):
    T, D = x.shape
    tm = _row_tile(T, ROW_TILE_CAP)
    return pl.pallas_call(
        _norm_kernel,
        grid=(pl.cdiv(T, tm),),
        in_specs=[pl.BlockSpec((tm, D), lambda i: (i, 0)), pl.BlockSpec((1, D), lambda i: (0, 0))],
        out_specs=pl.BlockSpec((tm, D), lambda i: (i, 0)),
        out_shape=jax.ShapeDtypeStruct((T, D), F32),
        compiler_params=_cparams(("parallel",)),
        name="norm_only",
    )(x, g.reshape(1, D))


def _mix_proj_kernel(h_ref, p_ref, mu_ref, w_ref, o_ref, l_scr):
    @pl.when(pl.program_id(2) == 0)
    def _():
        h = h_ref[...]
        l_scr[...] = (h + (p_ref[...] - h) * mu_ref[0]).astype(BF16)

    o_ref[...] = _dot(l_scr[...], w_ref[0])


def mix_proj(h, prev, mu, w, tn=1024):
    T, D = h.shape
    tm = _row_tile(T, ROW_TILE_CAP)
    G, _, N = w.shape
    tn = min(tn, N)
    nj = N // tn
    return pl.pallas_call(
        _mix_proj_kernel,
        grid=(pl.cdiv(T, tm), G, nj),
        in_specs=[pl.BlockSpec((tm, D), lambda i, g, j: (i, 0)),
                  pl.BlockSpec((tm, D), lambda i, g, j: (i, 0)),
                  pl.BlockSpec((1, 1, D), lambda i, g, j: (g, 0, 0)),
                  pl.BlockSpec((1, D, tn), lambda i, g, j: (g, 0, j))],
        out_specs=pl.BlockSpec((tm, tn), lambda i, g, j: (i, g * nj + j)),
        out_shape=jax.ShapeDtypeStruct((T, G * N), F32),
        scratch_shapes=[pltpu.VMEM((tm, D), BF16)],
        compiler_params=_cparams(("parallel", "arbitrary", "arbitrary")),
        name="mix_proj",
    )(h, prev, mu.reshape(G, 1, D), w)


def _lora_kernel(use_tanh, x_ref, w_ref, b_ref, o_ref):
    x = x_ref[...]
    if use_tanh:
        x = jnp.tanh(x)
    o_ref[...] = b_ref[...] + _dot(x.astype(BF16), w_ref[...])


def lora_out(x, col_block, w, b, use_tanh):
    T = x.shape[0]
    tm = _row_tile(T, ROW_TILE_CAP)
    K, N = w.shape
    return pl.pallas_call(
        functools.partial(_lora_kernel, use_tanh),
        grid=(pl.cdiv(T, tm),),
        in_specs=[pl.BlockSpec((tm, K), lambda i: (i, col_block)),
                  pl.BlockSpec((K, N), lambda i: (0, 0)),
                  pl.BlockSpec((1, N), lambda i: (0, 0))],
        out_specs=pl.BlockSpec((tm, N), lambda i: (i, 0)),
        out_shape=jax.ShapeDtypeStruct((T, N), F32),
        compiler_params=_cparams(("parallel",)),
        name="lora_out",
    )(x, w, b.reshape(1, N))


def _out_proj_kernel(tn, a_ref, w_ref, x_ref, g_ref, o_ref, y_scr):
    j = pl.program_id(1)
    off = pl.multiple_of(j * tn, tn)
    y_scr[:, pl.ds(off, tn)] = _dot(a_ref[...], w_ref[...])

    @pl.when(j == pl.num_programs(1) - 1)
    def _():
        y = y_scr[...]
        ms = jnp.mean(y * y, axis=-1, keepdims=True)
        o_ref[...] = x_ref[...] + y * lax.rsqrt(ms + NORM_EPS) * g_ref[...]


def out_proj(a, w, x, g, tn=512):
    T, K = a.shape
    tm = _row_tile(T, ROW_TILE_CAP)
    D = w.shape[1]
    return pl.pallas_call(
        functools.partial(_out_proj_kernel, tn),
        grid=(pl.cdiv(T, tm), D // tn),
        in_specs=[pl.BlockSpec((tm, K), lambda i, j: (i, 0)),
                  pl.BlockSpec((K, tn), lambda i, j: (0, j)),
                  pl.BlockSpec((tm, D), lambda i, j: (i, 0)),
                  pl.BlockSpec((1, D), lambda i, j: (0, 0))],
        out_specs=pl.BlockSpec((tm, D), lambda i, j: (i, 0)),
        out_shape=jax.ShapeDtypeStruct((T, D), F32),
        scratch_shapes=[pltpu.VMEM((tm, D), F32)],
        compiler_params=_cparams(("parallel", "arbitrary")),
        name="out_proj",
    )(a, w, x, g.reshape(1, D))


GDN_NU = 4


def _gdn_intra_kernel(L, qn_ref, kn_ref, vn_ref, qp_ref, kp_ref, vp_ref, cq_ref, ck_ref, cv_ref,
                      wq_ref, wk_ref, wv_ref, gt_ref, al_ref, dt_ref,
                      uv_ref, wqe_ref, kt_ref, qkd_ref, egl_ref, qbuf, kbuf, vbuf):
    c = pl.program_id(2)
    HD, NH = GDN_HEAD, STACK
    R = NH * L
    first = c == 0

    def conv_silu(buf, u_ref, p_ref, st_ref, w_ref):
        buf[CONV_PAD - 3:CONV_PAD, :] = jnp.where(first, st_ref[0], p_ref[SUBLANES - 3:SUBLANES, :])
        buf[CONV_PAD:CONV_PAD + L, :] = u_ref[...]
        acc = buf[CONV_PAD - 3:CONV_PAD - 3 + L, :] * w_ref[0:1, :]
        acc = acc + buf[CONV_PAD - 2:CONV_PAD - 2 + L, :] * w_ref[1:2, :]
        acc = acc + buf[CONV_PAD - 1:CONV_PAD - 1 + L, :] * w_ref[2:3, :]
        acc = acc + buf[CONV_PAD:CONV_PAD + L, :] * w_ref[3:4, :]
        return _silu(acc)

    q = conv_silu(qbuf, qn_ref, qp_ref, cq_ref, wq_ref)
    k = conv_silu(kbuf, kn_ref, kp_ref, ck_ref, wk_ref)
    v = conv_silu(vbuf, vn_ref, vp_ref, cv_ref, wv_ref)

    rl = lax.broadcasted_iota(jnp.int32, (L, L), 0)
    cl = lax.broadcasted_iota(jnp.int32, (L, L), 1)
    tril = jnp.where(cl <= rl, 1.0, 0.0).astype(BF16)
    triu = jnp.where(rl <= cl, 1.0, 0.0).astype(BF16)
    row = lax.broadcasted_iota(jnp.int32, (R, R), 0)
    col = lax.broadcasted_iota(jnp.int32, (R, R), 1)
    diff = row ^ col
    causal = (diff < L) & (col <= row)
    strict = (diff < L) & (col < row)
    eye = jnp.where(row == col, 1.0, 0.0).astype(F32)

    def l2n(x):
        return x * lax.rsqrt(jnp.sum(x * x, axis=-1, keepdims=True) + L2_EPS)

    def col_stack(x, first_lane):
        return jnp.concatenate([x[:, first_lane + j:first_lane + j + 1] for j in range(NH)], axis=0)

    kst, qst, vst, gst, bst, glast_st, grow_st, glast = [], [], [], [], [], [], [], []
    for u in range(GDN_NU):
        qh = [l2n(q[:, (2 * u + i) * HD:(2 * u + i + 1) * HD]) * (HD ** -0.5) for i in range(2)]
        kh = [l2n(k[:, (2 * u + i) * HD:(2 * u + i + 1) * HD]) for i in range(2)]
        kst.append(jnp.concatenate([kh[0], kh[0], kh[1], kh[1]], axis=0))
        qst.append(jnp.concatenate([qh[0], qh[0], qh[1], qh[1]], axis=0))
        vst.append(_stack_heads(v[:, u * NH * HD:(u + 1) * NH * HD], NH, HD))
        lanes = slice(u * LANES, (u + 1) * LANES)
        gu = gt_ref[:, lanes]
        beta = jax.nn.sigmoid(gu)
        sp = gu + dt_ref[:, lanes]
        sp = jnp.maximum(sp, 0.0) + jnp.log(1.0 + jnp.exp(-jnp.abs(sp)))
        g = -jnp.exp(al_ref[:, lanes]) * sp
        gcol = _cumsum_rows(tril, g)
        grow = _cumsum_cols_t(g, triu)
        gl = gcol[L - 1:L, :]
        glast.append(gl)
        gst.append(col_stack(gcol, NH))
        bst.append(col_stack(beta, 0))
        glast_st.append(jnp.concatenate([jnp.broadcast_to(gl[:, NH + j:NH + j + 1], (L, 1)) for j in range(NH)],
                                        axis=0))
        grow_st.append(jnp.concatenate([grow[NH + j:NH + j + 1, :] for j in range(NH)], axis=1))

    U = range(GDN_NU)
    kk = [_mm_nt(kst[u], kst[u]) for u in U]
    qk = [_mm_nt(qst[u], kst[u]) for u in U]
    decay = [jnp.where(causal, jnp.exp(jnp.where(causal, gst[u] - grow_st[u], 0.0)), 0.0) for u in U]
    a = [jnp.where(strict, kk[u] * decay[u] * bst[u], 0.0) for u in U]
    tinv = _tri_inv(a, diff, eye, L)
    rhs = [jnp.concatenate([vst[u] * bst[u], kst[u] * (bst[u] * jnp.exp(gst[u]))], axis=1) for u in U]
    sol = [_mm(tinv[u], rhs[u]) for u in U]
    for u in U:
        uv_ref[0, u] = sol[u][:, :HD]
        wk = sol[u][:, HD:]
        qe = qst[u] * jnp.exp(gst[u])
        wqe_ref[0, u] = jnp.concatenate(
            [x[j * L:(j + 1) * L] for j in range(NH) for x in (wk, qe)], axis=0).astype(BF16)
        kt_ref[0, u] = (kst[u] * jnp.exp(glast_st[u] - gst[u])).astype(BF16)
        qkd_ref[0, u] = (qk[u] * decay[u]).astype(BF16)
        egl_ref[0, u] = jnp.exp(glast[u])


def _gdn_inter_kernel(L, HG, uv_ref, wqe_ref, kt_ref, qkd_ref, egl_ref, z_ref, s0_ref, on_ref,
                      o_ref, so_ref, s_scr):
    c = pl.program_id(1)
    HD, NH = GDN_HEAD, STACK
    R = NH * L

    @pl.when(c == 0)
    def _():
        for g in range(HG):
            s_scr[g] = jnp.concatenate([s0_ref[0, NH * g + j] for j in range(NH)], axis=1)

    rh = lax.shift_right_logical(lax.broadcasted_iota(jnp.int32, (R, NH * HD), 0), _log2(L))
    ch = lax.shift_right_logical(lax.broadcasted_iota(jnp.int32, (R, NH * HD), 1), _log2(HD))
    own = rh == ch

    s_old = [s_scr[g] for g in range(HG)]
    d1 = [[_dot(wqe_ref[0, g, 2 * L * j:2 * L * (j + 1), :], s_old[g][:, j * HD:(j + 1) * HD].astype(BF16))
           for j in range(NH)] for g in range(HG)]
    v_new = [uv_ref[0, g] - jnp.concatenate([d1[g][j][:L] for j in range(NH)], axis=0) for g in range(HG)]
    o_st = [jnp.concatenate([d1[g][j][L:] for j in range(NH)], axis=0) + _dot(qkd_ref[0, g], v_new[g].astype(BF16))
            for g in range(HG)]
    for g in range(HG):
        v_exp = jnp.where(own, jnp.concatenate([v_new[g]] * NH, axis=1), 0.0)
        egl = egl_ref[0, g]
        e_exp = jnp.concatenate([jnp.broadcast_to(egl[:, NH + j:NH + j + 1], (1, HD)) for j in range(NH)], axis=1)
        s_scr[g] = s_old[g] * e_exp + _mm_tn(kt_ref[0, g], v_exp)

    outs = []
    for g in range(HG):
        for j in range(NH):
            o = o_st[g][j * L:(j + 1) * L]
            zj = z_ref[:, (NH * g + j) * HD:(NH * g + j + 1) * HD]
            ms = jnp.mean(o * o, axis=-1, keepdims=True)
            outs.append((o * lax.rsqrt(ms + NORM_EPS) * on_ref[...] * _silu(zj)).astype(BF16))
    o_ref[...] = jnp.concatenate(outs, axis=1)

    @pl.when(c == pl.num_programs(1) - 1)
    def _():
        for g in range(HG):
            for j in range(NH):
                so_ref[0, NH * g + j] = s_scr[g][:, j * HD:(j + 1) * HD]


def gdn_mix(proj, gates, a_par, dt_par, conv_state, s0, conv_w, o_norm, row0, B, NC, L):
    HV = s0.shape[1]
    HD, NH, NU = GDN_HEAD, STACK, GDN_NU
    HG = HV // NH
    QW = NU * (NH // 2) * HD
    VW = NU * NH * HD
    nq = (HV // 2) * HD // QW
    nv = HV * HD // VW
    Tseg = B * NC * L
    R = NH * L
    rb0 = row0 // L
    assert row0 % L == 0 and L % SUBLANES == 0 and HG % NU == 0

    cq = conv_state[:, :, :nq * QW]
    ck = conv_state[:, :, nq * QW:2 * nq * QW]
    cv = conv_state[:, :, 2 * nq * QW:]
    wq = conv_w[:, :nq * QW]
    wk = conv_w[:, nq * QW:2 * nq * QW]
    wv = conv_w[:, 2 * nq * QW:]

    chunk = lambda b, h, c: rb0 + b * NC + c
    before = lambda b, h, c: jnp.maximum(chunk(b, h, c) * (L // SUBLANES) - 1, 0)
    intra_in = [
        pl.BlockSpec((L, QW), lambda b, h, c: (chunk(b, h, c), h)),
        pl.BlockSpec((L, QW), lambda b, h, c: (chunk(b, h, c), nq + h)),
        pl.BlockSpec((L, VW), lambda b, h, c: (chunk(b, h, c), nv + h)),
        pl.BlockSpec((SUBLANES, QW), lambda b, h, c: (before(b, h, c), h)),
        pl.BlockSpec((SUBLANES, QW), lambda b, h, c: (before(b, h, c), nq + h)),
        pl.BlockSpec((SUBLANES, VW), lambda b, h, c: (before(b, h, c), nv + h)),
        pl.BlockSpec((1, GDN_CONV - 1, QW), lambda b, h, c: (b, 0, h)),
        pl.BlockSpec((1, GDN_CONV - 1, QW), lambda b, h, c: (b, 0, h)),
        pl.BlockSpec((1, GDN_CONV - 1, VW), lambda b, h, c: (b, 0, h)),
        pl.BlockSpec((GDN_CONV, QW), lambda b, h, c: (0, h)),
        pl.BlockSpec((GDN_CONV, QW), lambda b, h, c: (0, h)),
        pl.BlockSpec((GDN_CONV, VW), lambda b, h, c: (0, h)),
        pl.BlockSpec((L, NU * LANES), lambda b, h, c: (chunk(b, h, c), h)),
        pl.BlockSpec((1, NU * LANES), lambda b, h, c: (0, h)),
        pl.BlockSpec((1, NU * LANES), lambda b, h, c: (0, h)),
    ]
    unit = lambda w: pl.BlockSpec((1, NU, w[0], w[1]), lambda b, h, c: (b * NC + c, h, 0, 0))
    NCH = B * NC
    uv, wqe, kt, qkd, egl = pl.pallas_call(
        functools.partial(_gdn_intra_kernel, L),
        grid=(B, HG // NU, NC),
        in_specs=intra_in,
        out_specs=[unit((R, HD)), unit((2 * R, HD)), unit((R, HD)), unit((R, R)), unit((1, LANES))],
        out_shape=[jax.ShapeDtypeStruct((NCH, HG, R, HD), F32),
                   jax.ShapeDtypeStruct((NCH, HG, 2 * R, HD), BF16),
                   jax.ShapeDtypeStruct((NCH, HG, R, HD), BF16),
                   jax.ShapeDtypeStruct((NCH, HG, R, R), BF16),
                   jax.ShapeDtypeStruct((NCH, HG, 1, LANES), F32)],
        scratch_shapes=[pltpu.VMEM((CONV_PAD + L, QW), F32),
                        pltpu.VMEM((CONV_PAD + L, QW), F32),
                        pltpu.VMEM((CONV_PAD + L, VW), F32)],
        compiler_params=_cparams(("parallel", "parallel", "parallel")),
        name="gdn_intra",
    )(proj, proj, proj, proj, proj, proj, cq, ck, cv, wq, wk, wv, gates, a_par, dt_par)

    allu = lambda w: pl.BlockSpec((1, HG, w[0], w[1]), lambda b, c: (b * NC + c, 0, 0, 0))
    zblk = 2 * nv * VW // (HV * HD)
    o, s_new = pl.pallas_call(
        functools.partial(_gdn_inter_kernel, L, HG),
        grid=(B, NC),
        in_specs=[allu((R, HD)), allu((2 * R, HD)), allu((R, HD)), allu((R, R)), allu((1, LANES)),
                  pl.BlockSpec((L, HV * HD), lambda b, c: (rb0 + b * NC + c, zblk)),
                  pl.BlockSpec((1, HV, HD, HD), lambda b, c: (b, 0, 0, 0)),
                  pl.BlockSpec((1, HD), lambda b, c: (0, 0))],
        out_specs=[pl.BlockSpec((L, HV * HD), lambda b, c: (b * NC + c, 0)),
                   pl.BlockSpec((1, HV, HD, HD), lambda b, c: (b, 0, 0, 0))],
        out_shape=[jax.ShapeDtypeStruct((Tseg, HV * HD), BF16),
                   jax.ShapeDtypeStruct((B, HV, HD, HD), F32)],
        scratch_shapes=[pltpu.VMEM((HG, HD, NH * HD), F32)],
        compiler_params=_cparams(("parallel", "arbitrary")),
        name="gdn_inter",
    )(uv, wqe, kt, qkd, egl, proj, s0, o_norm.reshape(1, HD))
    return o, s_new


FOX_NH = 2
FOX_TP = 512
LOG2E = 1.4426950408889634


def _fox_prompt_kernel(TQ, scale, q_ref, k_ref, v_ref, km_ref, vm_ref, z_ref, ck_ref, cm_ref, o_ref):
    qi = pl.program_id(1)
    HD = FOX_HEAD
    HS = range(FOX_NH)
    hs = lambda i: slice(i * HD, (i + 1) * HD)
    q = [(q_ref[:, hs(i)] * (scale * LOG2E)).astype(BF16) for i in HS]

    s = [_mm_nt(q[i], km_ref[:, hs(i)]) - cm_ref[i] for i in HS]
    m = [jnp.max(s[i], axis=-1, keepdims=True) for i in HS]
    p = [jnp.exp2(s[i] - m[i]) for i in HS]
    l = [jnp.sum(p[i], axis=-1, keepdims=True) for i in HS]
    acc = [_mm(p[i], vm_ref[:, hs(i)]) for i in HS]

    def step(kb, carry, diagonal):
        m, l, acc = carry
        off = pl.multiple_of(kb * TQ, TQ)
        s = [_mm_nt(q[i], k_ref[pl.ds(off, TQ), hs(i)]) - ck_ref[i, :, pl.ds(off, TQ)] for i in HS]
        if diagonal:
            row = lax.broadcasted_iota(jnp.int32, (TQ, TQ), 0)
            col = lax.broadcasted_iota(jnp.int32, (TQ, TQ), 1)
            s = [jnp.where(col <= row, s[i], -jnp.inf) for i in HS]
        m_new = [jnp.maximum(m[i], jnp.max(s[i], axis=-1, keepdims=True)) for i in HS]
        alpha = [jnp.exp2(m[i] - m_new[i]) for i in HS]
        p = [jnp.exp2(s[i] - m_new[i]) for i in HS]
        l = [alpha[i] * l[i] + jnp.sum(p[i], axis=-1, keepdims=True) for i in HS]
        acc = [alpha[i] * acc[i] + _mm(p[i], v_ref[pl.ds(off, TQ), hs(i)]) for i in HS]
        return tuple(m_new), tuple(l), tuple(acc)

    carry = lax.fori_loop(0, qi, lambda kb, cr: step(kb, cr, False), (tuple(m), tuple(l), tuple(acc)))
    m, l, acc = step(qi, carry, True)
    o_ref[...] = jnp.concatenate([(acc[i] / l[i]) * _silu(z_ref[:, hs(i)]) for i in HS], axis=1).astype(BF16)


def fox_prompt(qz, kv16, c_main, c_meta, T, meta_row0, n_meta, H, TQ=512):
    HD = FOX_HEAD
    W = FOX_NH * HD
    HP = H // FOX_NH
    TQ = min(TQ, T)
    assert T % TQ == 0 and meta_row0 % n_meta == 0 and H % FOX_NH == 0
    mb = meta_row0 // n_meta
    return pl.pallas_call(
        functools.partial(_fox_prompt_kernel, TQ, HD ** -0.5),
        grid=(HP, T // TQ),
        in_specs=[pl.BlockSpec((TQ, W), lambda h, i: (i, h)),
                  pl.BlockSpec((T, W), lambda h, i: (0, h)),
                  pl.BlockSpec((T, W), lambda h, i: (0, HP + h)),
                  pl.BlockSpec((n_meta, W), lambda h, i: (mb, h)),
                  pl.BlockSpec((n_meta, W), lambda h, i: (mb, HP + h)),
                  pl.BlockSpec((TQ, W), lambda h, i: (i, HP + h)),
                  pl.BlockSpec((FOX_NH, 1, T), lambda h, i: (h, 0, 0)),
                  pl.BlockSpec((FOX_NH, 1, n_meta), lambda h, i: (h, 0, 0))],
        out_specs=pl.BlockSpec((TQ, W), lambda h, i: (i, h)),
        out_shape=jax.ShapeDtypeStruct((T, H * HD), BF16),
        compiler_params=_cparams(("parallel", "arbitrary"), VMEM_LIMIT_BIG),
        name="fox_prompt",
    )(qz, kv16, kv16, kv16, kv16, qz, c_main, c_meta)


def _fox_seq_kernel(has_cache, scale, q_ref, k_ref, v_ref, z_ref, cn_ref, *rest):
    if has_cache:
        kc_ref, vc_ref, cc_ref, o_ref, m_scr, l_scr, a_scr = rest
    else:
        o_ref, m_scr, l_scr, a_scr = rest
    c = pl.program_id(1)
    TQ = q_ref.shape[0]
    HD = FOX_HEAD
    H = q_ref.shape[1] // HD
    HS = range(H)
    hs = lambda h: slice(h * HD, (h + 1) * HD)
    q = [(q_ref[:, hs(h)] * (scale * LOG2E)).astype(BF16) for h in HS]

    @pl.when(c == 0)
    def _():
        row = lax.broadcasted_iota(jnp.int32, (TQ, TQ), 0)
        col = lax.broadcasted_iota(jnp.int32, (TQ, TQ), 1)
        s = [jnp.where(col <= row, _mm_nt(q[h], k_ref[:, hs(h)]) - cn_ref[0, h], -jnp.inf) for h in HS]
        m = [jnp.max(s[h], axis=-1, keepdims=True) for h in HS]
        p = [jnp.exp2(s[h] - m[h]) for h in HS]
        for h in HS:
            m_scr[h] = m[h]
            l_scr[h] = jnp.sum(p[h], axis=-1, keepdims=True)
            a_scr[h] = _mm(p[h], v_ref[:, hs(h)])

    if has_cache:
        TP = cc_ref.shape[-1]
        s = [_mm_nt(q[h], kc_ref[pl.ds(h, TP, stride=H), :]) - cc_ref[0, h] for h in HS]
        m_old = [m_scr[h] for h in HS]
        l_old = [l_scr[h] for h in HS]
        a_old = [a_scr[h] for h in HS]
        m_new = [jnp.maximum(m_old[h], jnp.max(s[h], axis=-1, keepdims=True)) for h in HS]
        alpha = [jnp.exp2(m_old[h] - m_new[h]) for h in HS]
        p = [jnp.exp2(s[h] - m_new[h]) for h in HS]
        pv = [_mm(p[h], vc_ref[pl.ds(h, TP, stride=H), :]) for h in HS]
        for h in HS:
            m_scr[h] = m_new[h]
            l_scr[h] = alpha[h] * l_old[h] + jnp.sum(p[h], axis=-1, keepdims=True)
            a_scr[h] = alpha[h] * a_old[h] + pv[h]

    @pl.when(c == pl.num_programs(1) - 1)
    def _():
        o_ref[...] = jnp.concatenate([(a_scr[h] / l_scr[h]) * _silu(z_ref[:, hs(h)]) for h in HS],
                                     axis=1).astype(BF16)


def fox_seq(qz, kv16, c_new, row0, B, TQ, H, cache=None):
    HD = FOX_HEAD
    FD = H * HD
    assert row0 % TQ == 0
    rb0 = row0 // TQ
    tok = lambda sec: pl.BlockSpec((TQ, FD), lambda b, c: (rb0 + b, sec))
    in_specs = [tok(0), tok(0), tok(1), tok(1), pl.BlockSpec((1, H, 1, TQ), lambda b, c: (b, 0, 0, 0))]
    args = [qz, kv16, kv16, qz, c_new]
    nck = 1
    if cache is not None:
        kc, vc, cc, b0 = cache
        P = cc.shape[-1]
        TP = min(FOX_TP, P)
        assert P % TP == 0
        nck = P // TP
        blk = pl.BlockSpec((TP * H, HD), lambda b, c: ((b0 + b) * nck + c, 0))
        in_specs += [blk, blk, pl.BlockSpec((1, H, 1, TP), lambda b, c: (b, 0, 0, c))]
        args += [kc, vc, cc]
    return pl.pallas_call(
        functools.partial(_fox_seq_kernel, cache is not None, HD ** -0.5),
        grid=(B, nck),
        in_specs=in_specs,
        out_specs=pl.BlockSpec((TQ, FD), lambda b, c: (b, 0)),
        out_shape=jax.ShapeDtypeStruct((B * TQ, FD), BF16),
        scratch_shapes=[pltpu.VMEM((H, TQ, 1), F32), pltpu.VMEM((H, TQ, 1), F32), pltpu.VMEM((H, TQ, HD), F32)],
        compiler_params=_cparams(("parallel", "arbitrary")),
        name="fox_seq",
    )(*args)


WKV_NU = 4


def _wkv_intra_kernel(L, r_ref, k_ref, v_ref, wp_ref, ap_ref, kk_ref, ka_ref, rk_ref,
                      x_ref, u0_ref, o0_ref, arb_ref, vk_ref, bh_ref, gl_ref, bonus_ref):
    HD, NH = RWKV_HEAD, STACK
    R = NH * L
    W = NH * HD

    rl = lax.broadcasted_iota(jnp.int32, (L, L), 0)
    cl = lax.broadcasted_iota(jnp.int32, (L, L), 1)
    tril = jnp.where(cl <= rl, 1.0, 0.0).astype(BF16)
    row = lax.broadcasted_iota(jnp.int32, (R, R), 0)
    col = lax.broadcasted_iota(jnp.int32, (R, R), 1)
    diff = row ^ col
    causal = (diff < L) & (col <= row)
    strict = (diff < L) & (col < row)
    eye = jnp.where(row == col, 1.0, 0.0).astype(F32)

    heads = [slice(j * HD, (j + 1) * HD) for j in range(WKV_NU * NH)]
    r = r_ref[...]
    v = v_ref[...]
    k_raw = k_ref[...]
    wp = -wp_ref[...]
    w = -(jnp.maximum(wp, 0.0) + jnp.log(1.0 + jnp.exp(-jnp.abs(wp)))) - 0.5
    lw = -jnp.exp(w)
    a_sig = jax.nn.sigmoid(ap_ref[...])
    kk = k_raw * kk_ref[...]
    kk = jnp.concatenate(
        [kk[:, h] * lax.rsqrt(jnp.sum(kk[:, h] * kk[:, h], axis=-1, keepdims=True) + L2_EPS) for h in heads], axis=1)
    k = k_raw * (1.0 + (a_sig - 1.0) * ka_ref[...])
    a = -kk
    b = kk * a_sig
    rk = r * k * rk_ref[...]
    bonus_ref[...] = jnp.concatenate([jnp.sum(rk[:, h], axis=-1, keepdims=True) * v[:, h] for h in heads], axis=1)

    cum = _cumsum_rows(tril, lw)
    clast = cum[L - 1:L, :]
    e_neg = jnp.exp(-cum)
    e_tail = jnp.exp(clast - cum)
    rt = r * jnp.exp(cum)
    at = a * jnp.exp(cum - lw)
    kt = k * e_neg
    bt = b * e_neg
    khat = k * e_tail
    bhat = b * e_tail
    gl_ref[0] = jnp.exp(clast)

    U = range(WKV_NU)
    sl = lambda x, u: x[:, u * W:(u + 1) * W]
    st = lambda x, u: _stack_heads(sl(x, u), NH, HD)
    v_st = [st(v, u) for u in U]
    aa = [_mm_nt(jnp.concatenate([st(at, u), st(rt, u)], axis=0),
                 jnp.concatenate([st(bt, u), st(kt, u)], axis=0)) for u in U]
    a_ab = [jnp.where(strict, aa[u][:R, :R], 0.0) for u in U]
    a_ak = [jnp.where(strict, aa[u][:R, R:], 0.0) for u in U]
    a_rb = [jnp.where(causal, aa[u][R:, :R], 0.0) for u in U]
    a_rk = [jnp.where(causal, aa[u][R:, R:], 0.0) for u in U]
    av = [_mm(jnp.concatenate([a_ak[u], a_rk[u]], axis=0), v_st[u]) for u in U]
    vk = [_mm_tn(v_st[u], _expand_heads(sl(khat, u), NH, L, HD)) for u in U]
    tinv = _tri_inv([-x for x in a_ab], diff, eye, L)
    w_exp = [_mm(tinv[u], _expand_heads(sl(at, u), NH, L, HD)) for u in U]
    u0 = [_mm(tinv[u], av[u][:R]) for u in U]
    for u in U:
        x_ref[0, u] = jnp.concatenate([w_exp[u], _expand_heads(sl(rt, u), NH, L, HD)], axis=0).astype(BF16)
        u0_ref[0, u] = u0[u]
        o0_ref[0, u] = av[u][R:]
        arb_ref[0, u] = a_rb[u].astype(BF16)
        vk_ref[0, u] = vk[u]
        bh_ref[0, u] = _expand_heads(sl(bhat, u), NH, L, HD).astype(BF16)


def _wkv_inter_kernel(L, HG, x_ref, u0_ref, o0_ref, arb_ref, vk_ref, bh_ref, gl_ref, s0_ref,
                      bonus_ref, gate_ref, gnw_ref, gnb_ref, o_ref, so_ref, s_scr):
    c = pl.program_id(1)
    HD, NH = RWKV_HEAD, STACK
    R = NH * L
    W = NH * HD

    @pl.when(c == 0)
    def _():
        for g in range(HG):
            s_scr[g] = jnp.concatenate([s0_ref[0, NH * g + j] for j in range(NH)], axis=1)

    s_old = [s_scr[g] for g in range(HG)]
    d1 = [_mm_nt(x_ref[0, g], s_old[g]) for g in range(HG)]
    u_st = [u0_ref[0, g] + d1[g][:R] for g in range(HG)]
    o_st = [d1[g][R:] + o0_ref[0, g] + _dot(arb_ref[0, g], u_st[g].astype(BF16)) for g in range(HG)]
    for g in range(HG):
        s_scr[g] = s_old[g] * gl_ref[0, :, g * W:(g + 1) * W] + vk_ref[0, g] + _mm_tn(u_st[g], bh_ref[0, g])
    outs = []
    for g in range(HG):
        mean = jnp.mean(o_st[g], axis=-1, keepdims=True)
        d = o_st[g] - mean
        var = jnp.mean(d * d, axis=-1, keepdims=True)
        on = d * lax.rsqrt(var + RWKV_GN_EPS)
        outs += [on[j * L:(j + 1) * L] for j in range(NH)]
    o = jnp.concatenate(outs, axis=1) * gnw_ref[...] + gnb_ref[...] + bonus_ref[...]
    o_ref[...] = (o * _silu(gate_ref[...])).astype(BF16)

    @pl.when(c == pl.num_programs(1) - 1)
    def _():
        for g in range(HG):
            for j in range(NH):
                so_ref[0, NH * g + j] = s_scr[g][:, j * HD:(j + 1) * HD]


def wkv_mix(big, w_pre, a_pre, k_k, k_a, r_k, gn_w, gn_b, s0, row0, B, NC, L):
    HN = s0.shape[1]
    HD, NH, NU = RWKV_HEAD, STACK, WKV_NU
    HG = HN // NH
    W = NH * HD
    R = NH * L
    D = HN * HD
    Tseg = B * NC * L
    NCH = B * NC
    rb0 = row0 // L
    nsec = D // (NU * W)
    assert row0 % L == 0 and HG % NU == 0
    tok = lambda sec: pl.BlockSpec((L, NU * W), lambda bb, h, c: (rb0 + bb * NC + c, sec * nsec + h))
    par = pl.BlockSpec((1, NU * W), lambda bb, h, c: (0, h))
    unit = lambda w: pl.BlockSpec((1, NU, w[0], w[1]), lambda bb, h, c: (bb * NC + c, h, 0, 0))
    x, u0, o0, arb, vk, bh, gl, bonus = pl.pallas_call(
        functools.partial(_wkv_intra_kernel, L),
        grid=(B, HG // NU, NC),
        in_specs=[tok(0), tok(1), tok(2), tok(0), tok(0), par, par, par],
        out_specs=[unit((2 * R, W)), unit((R, HD)), unit((R, HD)), unit((R, R)), unit((HD, W)), unit((R, W)),
                   pl.BlockSpec((1, 1, NU * W), lambda bb, h, c: (bb * NC + c, 0, h)),
                   pl.BlockSpec((L, NU * W), lambda bb, h, c: (bb * NC + c, h))],
        out_shape=[jax.ShapeDtypeStruct((NCH, HG, 2 * R, W), BF16),
                   jax.ShapeDtypeStruct((NCH, HG, R, HD), F32),
                   jax.ShapeDtypeStruct((NCH, HG, R, HD), F32),
                   jax.ShapeDtypeStruct((NCH, HG, R, R), BF16),
                   jax.ShapeDtypeStruct((NCH, HG, HD, W), F32),
                   jax.ShapeDtypeStruct((NCH, HG, R, W), BF16),
                   jax.ShapeDtypeStruct((NCH, 1, D), F32),
                   jax.ShapeDtypeStruct((Tseg, D), F32)],
        compiler_params=_cparams(("parallel", "parallel", "parallel")),
        name="wkv_intra",
    )(big, big, big, w_pre, a_pre, k_k.reshape(1, D), k_a.reshape(1, D), r_k.reshape(1, D))

    allu = lambda w: pl.BlockSpec((1, HG, w[0], w[1]), lambda bb, c: (bb * NC + c, 0, 0, 0))
    o, s_new = pl.pallas_call(
        functools.partial(_wkv_inter_kernel, L, HG),
        grid=(B, NC),
        in_specs=[allu((2 * R, W)), allu((R, HD)), allu((R, HD)), allu((R, R)), allu((HD, W)), allu((R, W)),
                  pl.BlockSpec((1, 1, D), lambda bb, c: (bb * NC + c, 0, 0)),
                  pl.BlockSpec((1, HN, HD, HD), lambda bb, c: (bb, 0, 0, 0)),
                  pl.BlockSpec((L, D), lambda bb, c: (bb * NC + c, 0)),
                  pl.BlockSpec((L, D), lambda bb, c: (rb0 + bb * NC + c, 3)),
                  pl.BlockSpec((1, D), lambda bb, c: (0, 0)),
                  pl.BlockSpec((1, D), lambda bb, c: (0, 0))],
        out_specs=[pl.BlockSpec((L, D), lambda bb, c: (bb * NC + c, 0)),
                   pl.BlockSpec((1, HN, HD, HD), lambda bb, c: (bb, 0, 0, 0))],
        out_shape=[jax.ShapeDtypeStruct((Tseg, D), BF16),
                   jax.ShapeDtypeStruct((B, HN, HD, HD), F32)],
        scratch_shapes=[pltpu.VMEM((HG, HD, W), F32)],
        compiler_params=_cparams(("parallel", "arbitrary")),
        name="wkv_inter",
    )(x, u0, o0, arb, vk, bh, gl, s0, bonus, big, gn_w.reshape(1, D), gn_b.reshape(1, D))
    return o, s_new


def _pad_cols(w, n):
    return jnp.pad(w, ((0, 0), (0, n - w.shape[1])))


def _gdn_layer(X, seg, g_pre, g_post, w_in, conv_w, a_log, dt_bias, o_norm, w_out, st_S, st_conv):
    (T, BS, TS, NM) = seg
    HV = a_log.shape[0]
    CD = conv_w.shape[1]
    VD = HV * GDN_HEAD
    D = w_in.shape[0]
    NH = STACK
    HG = HV // NH
    w_main = w_in[:, :CD + VD].astype(BF16)
    wb = w_in[:, CD + VD:CD + VD + HV].reshape(D, HG, NH)
    wa = w_in[:, CD + VD + HV:].reshape(D, HG, NH)
    w_gate = jnp.pad(jnp.concatenate([wb, wa], axis=2), ((0, 0), (0, 0), (0, LANES - 2 * NH)))
    decay_lanes = lambda p: jnp.pad(p.reshape(HG, NH), ((0, 0), (NH, LANES - 2 * NH))).reshape(1, HG * LANES)
    a_par, dt_par = decay_lanes(a_log), decay_lanes(dt_bias)
    proj = norm_proj(X, g_pre, w_main)
    gates = norm_proj(X, g_pre, w_gate.reshape(D, HG * LANES).astype(BF16))
    gp = (gates, a_par, dt_par)

    r_s, r_m = T, T + BS * TS
    tail = GDN_CONV - 1
    c_m = proj[r_m + NM - tail:r_m + NM, :CD][None]
    c_p = proj[T - tail:T, :CD][None]
    c_s = proj[r_s:r_m, :CD].reshape(BS, TS, CD)[:, TS - tail:]
    z_conv = jnp.zeros((1, tail, CD), F32)
    z_S = jnp.zeros((1, HV, GDN_HEAD, GDN_HEAD), F32)
    o_m, S_m = gdn_mix(proj, *gp, z_conv, z_S, conv_w, o_norm, r_m, 1, 1, NM)
    LP = 64
    o_p, S_p = gdn_mix(proj, *gp, c_m, S_m, conv_w, o_norm, 0, 1, T // LP, LP)
    o_s, S_s = gdn_mix(proj, *gp, st_conv, st_S, conv_w, o_norm, r_s, BS, 1, TS)
    o = jnp.concatenate([o_p, o_s, o_m], axis=0)
    X = out_proj(o, w_out.astype(BF16), X, g_post)
    return X, (S_p, c_p, S_s, c_s)


def _fox_layer(X, seg, g_pre, g_post, w_in, b_f, w_out, caches, j):
    (T, BS, TS, NM) = seg
    cache_k, cache_v, cache_logf = caches
    H = b_f.shape[0]
    FD = H * FOX_HEAD
    qz = norm_proj(X, g_pre, jnp.concatenate([w_in[:, :FD], w_in[:, 3 * FD:4 * FD]], axis=1).astype(BF16))
    kv, kv16 = norm_proj(X, g_pre, w_in[:, FD:3 * FD].astype(BF16), also_bf16=True)
    gates = norm_proj(X, g_pre, _pad_cols(w_in[:, 4 * FD:], LANES).astype(BF16))
    logf = jax.nn.log_sigmoid(gates[:, :H] + b_f)
    r_s, r_m = T, T + BS * TS

    c_p = jnp.cumsum(jnp.concatenate([logf[r_m:], logf[:T]], axis=0), axis=0) * LOG2E
    c_meta = c_p[:NM].T.reshape(H, 1, NM)
    c_main = c_p[NM:].T.reshape(H, 1, T)
    P = cache_k.shape[2]
    c_s = jnp.cumsum(jnp.concatenate([cache_logf[j], logf[r_s:r_m].reshape(BS, TS, H)], axis=1), axis=1) * LOG2E
    c_s = c_s.transpose(0, 2, 1)
    c_cache = c_s[:, :, :P].reshape(BS, H, 1, P)
    c_new = c_s[:, :, P:].reshape(BS, H, 1, TS)

    o_p = fox_prompt(qz, kv16, c_main, c_meta, T, r_m, NM, H)
    o_m = fox_seq(qz, kv16, c_meta.reshape(1, H, 1, NM), r_m, 1, NM, H)
    o_s = fox_seq(qz, kv16, c_new, r_s, BS, TS, H,
                  cache=(cache_k.reshape(-1, FOX_HEAD), cache_v.reshape(-1, FOX_HEAD), c_cache, j * BS))
    o = jnp.concatenate([o_p, o_s, o_m], axis=0)
    X = out_proj(o, w_out.astype(BF16), X, g_post)

    def seq(a, n):
        return jnp.concatenate([a[r_m:], a[:T]], axis=0).reshape(1, NM + T, H, n)

    k_all = kv[:, :FD]
    v_all = kv[:, FD:]
    outs = (seq(k_all, FOX_HEAD), seq(v_all, FOX_HEAD),
            jnp.concatenate([logf[r_m:], logf[:T]], axis=0).reshape(1, NM + T, H),
            k_all[r_s:r_m].reshape(BS, TS, H, FOX_HEAD), v_all[r_s:r_m].reshape(BS, TS, H, FOX_HEAD),
            logf[r_s:r_m].reshape(BS, TS, H))
    return X, outs


def _rwkv_layer(X, seg, g_pre, g_post, mu, w_r, w_k, w_v, w_g, w0, w1, w2, a0, a1, a2, k_k, k_a, r_k,
                gn_w, gn_b, w_out, st_S, st_shift):
    (T, BS, TS, NM) = seg
    D = X.shape[1]
    HN = r_k.shape[0]
    HD = RWKV_HEAD
    r_s, r_m = T, T + BS * TS
    h = norm_only(X, g_pre)
    h_s = h[r_s:r_m].reshape(BS, TS, D)
    prev = jnp.concatenate([
        h[r_m + NM - 1:r_m + NM], h[:T - 1],
        jnp.concatenate([st_shift[:, None, :], h_s[:, :-1]], axis=1).reshape(BS * TS, D),
        jnp.zeros((1, D), F32), h[r_m:r_m + NM - 1]], axis=0)

    big = mix_proj(h, prev, mu[jnp.array([0, 2, 3, 5])], jnp.stack([w_r, w_k, w_v, w_g]).astype(BF16))
    lora_w = jnp.stack([_pad_cols(w1, LANES), _pad_cols(a1, LANES)]).astype(BF16)
    lo = mix_proj(h, prev, mu[jnp.array([1, 4])], lora_w)
    pad_rows = lambda w: jnp.pad(w, ((0, LANES - w.shape[0]), (0, 0))).astype(BF16)
    w_pre = lora_out(lo, 0, pad_rows(w2), w0, True)
    a_pre = lora_out(lo, 1, pad_rows(a2), a0, False)

    zS = jnp.zeros((1, HN, HD, HD), F32)
    par = (k_k, k_a, r_k, gn_w, gn_b)
    o_m, S_m = wkv_mix(big, w_pre, a_pre, *par, zS, r_m, 1, 1, NM)
    LP = 64
    o_p, S_p = wkv_mix(big, w_pre, a_pre, *par, S_m, 0, 1, T // LP, LP)
    o_s, S_s = wkv_mix(big, w_pre, a_pre, *par, st_S, r_s, BS, 1, TS)
    o = jnp.concatenate([o_p, o_s, o_m], axis=0)
    X = out_proj(o, w_out.astype(BF16), X, g_post)
    return X, (S_p, h[T - 1:T], S_s, h_s[:, -1])


def kernel(x_prompt, x_sample, state_gdn_S, state_gdn_conv, cache_fox_k, cache_fox_v, cache_fox_logf, state_rwkv_S, state_rwkv_shift, meta, norm_pre, norm_post, gdn_w_in, gdn_conv_w, gdn_a_log, gdn_dt_bias, gdn_o_norm, gdn_w_out, fox_w_in, fox_b_f, fox_w_out, rwkv_mu, rwkv_w_r, rwkv_w_k, rwkv_w_v, rwkv_w_g, rwkv_w0, rwkv_w1, rwkv_w2, rwkv_a0, rwkv_a1, rwkv_a2, rwkv_k_k, rwkv_k_a, rwkv_r_k, rwkv_gn_w, rwkv_gn_b, rwkv_w_out):
    _, T, D = x_prompt.shape
    BS, TS, _ = x_sample.shape
    NM = meta.shape[0]
    depth = norm_pre.shape[0]
    assert x_prompt.shape[0] == 1
    seg = (T, BS, TS, NM)
    X = jnp.concatenate([x_prompt[0], x_sample.reshape(BS * TS, D), meta.astype(x_prompt.dtype)], axis=0)

    gdn_out, fox_out, rwkv_out = [], [], []
    for i in range(depth):
        kind, j = i % 3, i // 3
        if kind == 0:
            X, st = _gdn_layer(X, seg, norm_pre[i], norm_post[i], gdn_w_in[j], gdn_conv_w[j], gdn_a_log[j],
                               gdn_dt_bias[j], gdn_o_norm[j], gdn_w_out[j], state_gdn_S[j], state_gdn_conv[j])
            gdn_out.append(st)
        elif kind == 1:
            X, st = _fox_layer(X, seg, norm_pre[i], norm_post[i], fox_w_in[j], fox_b_f[j], fox_w_out[j],
                               (cache_fox_k, cache_fox_v, cache_fox_logf), j)
            fox_out.append(st)
        else:
            X, st = _rwkv_layer(X, seg, norm_pre[i], norm_post[i], rwkv_mu[j], rwkv_w_r[j], rwkv_w_k[j],
                                rwkv_w_v[j], rwkv_w_g[j], rwkv_w0[j], rwkv_w1[j], rwkv_w2[j], rwkv_a0[j],
                                rwkv_a1[j], rwkv_a2[j], rwkv_k_k[j], rwkv_k_a[j], rwkv_r_k[j], rwkv_gn_w[j],
                                rwkv_gn_b[j], rwkv_w_out[j], state_rwkv_S[j], state_rwkv_shift[j])
            rwkv_out.append(st)

    stack = lambda items, n: jnp.stack([it[n] for it in items])
    y_prompt = X[:T][None]
    y_sample = X[T:T + BS * TS].reshape(BS, TS, D)
    return (y_prompt, y_sample,
            stack(gdn_out, 0), stack(gdn_out, 1),
            stack(fox_out, 0), stack(fox_out, 1), stack(fox_out, 2),
            stack(rwkv_out, 0), stack(rwkv_out, 1),
            stack(gdn_out, 2), stack(gdn_out, 3),
            stack(fox_out, 3), stack(fox_out, 4), stack(fox_out, 5),
            stack(rwkv_out, 2), stack(rwkv_out, 3))
```

```python
import functools

import jax
import jax.numpy as jnp
from jax import lax
from jax.experimental import pallas as pl
from jax.experimental.pallas import tpu as pltpu

F32 = jnp.float32
BF16 = jnp.bfloat16

NORM_EPS = 1e-6
L2_EPS = 1e-6
RWKV_GN_EPS = 64e-5
GDN_HEAD = 128
GDN_CONV = 4
FOX_HEAD = 128
RWKV_HEAD = 64
LANES = 128
SUBLANES = 8
CONV_PAD = 8
INV_BASE = 16
STACK = 4
VMEM_LIMIT = 48 * 1024 * 1024
VMEM_LIMIT_BIG = 56 * 1024 * 1024


def _cparams(sem, limit=VMEM_LIMIT):
    return pltpu.CompilerParams(dimension_semantics=sem, vmem_limit_bytes=limit)


def _log2(n):
    assert n & (n - 1) == 0
    return n.bit_length() - 1


def _dot(a, b):
    return jnp.dot(a, b, preferred_element_type=F32)


def _mm(a, b):
    return _dot(a.astype(BF16), b.astype(BF16))


def _mm_nt(a, b):
    return lax.dot_general(a.astype(BF16), b.astype(BF16), (((1,), (1,)), ((), ())),
                           preferred_element_type=F32)


def _mm_tn(a, b):
    return lax.dot_general(a.astype(BF16), b.astype(BF16), (((0,), (0,)), ((), ())),
                           preferred_element_type=F32)


def _split3(a):
    hi = a.astype(BF16)
    r = a - hi.astype(F32)
    mid = r.astype(BF16)
    lo = (r - mid.astype(F32)).astype(BF16)
    return hi, mid, lo


def _cumsum_rows(tril, g):
    hi, mid, lo = _split3(g)
    return _dot(tril, hi) + (_dot(tril, mid) + _dot(tril, lo))


def _cumsum_cols_t(g, triu):
    hi, mid, lo = _split3(g)
    return _mm_tn(hi, triu) + (_mm_tn(mid, triu) + _mm_tn(lo, triu))


def _tri_inv(a_list, diff, eye, L):
    n1 = [jnp.where(diff < INV_BASE, -a, 0.0) for a in a_list]
    p = [eye + n for n in n1]
    npow = n1
    steps = 1
    while 2 * steps < INV_BASE:
        npow = [_mm(x, x) for x in npow]
        p = [pp + _mm(pp, x) for pp, x in zip(p, npow)]
        steps *= 2
    blk = INV_BASE
    while blk < L:
        e = [jnp.where((diff < 2 * blk) & (diff >= blk), a, 0.0) for a in a_list]
        pe = [_mm(pp, ee) for pp, ee in zip(p, e)]
        p = [pp - _mm(x, pp) for pp, x in zip(p, pe)]
        blk *= 2
    return p


def _stack_heads(x, n, w):
    return jnp.concatenate([x[:, j * w:(j + 1) * w] for j in range(n)], axis=0)


def _expand_heads(x, n, L, w):
    t = jnp.concatenate([x] * n, axis=0)
    rh = lax.shift_right_logical(lax.broadcasted_iota(jnp.int32, t.shape, 0), _log2(L))
    ch = lax.shift_right_logical(lax.broadcasted_iota(jnp.int32, t.shape, 1), _log2(w))
    return jnp.where(rh == ch, t, 0.0)


def _silu(x):
    return x * jax.nn.sigmoid(x)


NORM_PROJ_ROWS = 1600
ROW_TILE_CAP = 544


def _row_tile(T, cap, exact=False):
    best = 0
    for t in range(16, cap + 1, 16):
        if T % t == 0:
            best = t
    assert best > 0
    return best if exact or best >= cap // 4 else (cap // 16) * 16


def _norm_proj_kernel(x_ref, g_ref, w_ref, o_ref, *rest):
    h_scr = rest[-1]

    @pl.when(pl.program_id(1) == 0)
    def _():
        x = x_ref[...]
        ms = jnp.mean(x * x, axis=-1, keepdims=True)
        h_scr[...] = (x * lax.rsqrt(ms + NORM_EPS) * g_ref[...]).astype(BF16)

    y = _dot(h_scr[...], w_ref[...])
    o_ref[...] = y
    if len(rest) == 2:
        rest[0][...] = y.astype(BF16)


def norm_proj(x, g, w, tn=512, also_bf16=False):
    T, D = x.shape
    N = w.shape[1]
    tn = min(tn, N)
    tm = _row_tile(T, NORM_PROJ_ROWS)
    assert N % tn == 0
    out_spec = pl.BlockSpec((tm, tn), lambda i, j: (i, j))
    out_specs, out_shape = out_spec, jax.ShapeDtypeStruct((T, N), F32)
    if also_bf16:
        out_specs, out_shape = [out_spec, out_spec], [out_shape, jax.ShapeDtypeStruct((T, N), BF16)]
    return pl.pallas_call(
        _norm_proj_kernel,
        grid=(pl.cdiv(T, tm), N // tn),
        in_specs=[pl.BlockSpec((tm, D), lambda i, j: (i, 0)),
                  pl.BlockSpec((1, D), lambda i, j: (0, 0)),
                  pl.BlockSpec((D, tn), lambda i, j: (0, j))],
        out_specs=out_specs,
        out_shape=out_shape,
        scratch_shapes=[pltpu.VMEM((tm, D), BF16)],
        compiler_params=_cparams(("parallel", "arbitrary"), VMEM_LIMIT_BIG),
        name="norm_proj",
    )(x, g.reshape(1, D), w)


def _norm_shift_kernel(x_ref, g_ref, h_ref, p_ref, last_scr):
    tm = x_ref.shape[0]

    @pl.when(pl.program_id(0) == 0)
    def _():
        last_scr[...] = jnp.zeros_like(last_scr)

    x = x_ref[...]
    ms = jnp.mean(x * x, axis=-1, keepdims=True)
    h = x * lax.rsqrt(ms + NORM_EPS) * g_ref[...]
    h_ref[...] = h
    p_ref[0:1, :] = last_scr[...]
    p_ref[1:tm, :] = h[0:tm - 1, :]
    last_scr[...] = h[tm - 1:tm, :]


def norm_shift(x, g):
    T, D = x.shape
    tm = _row_tile(T, ROW_TILE_CAP, exact=True)
    spec = pl.BlockSpec((tm, D), lambda i: (i, 0))
    return pl.pallas_call(
        _norm_shift_kernel,
        grid=(T // tm,),
        in_specs=[spec, pl.BlockSpec((1, D), lambda i: (0, 0))],
        out_specs=[spec, spec],
        out_shape=[jax.ShapeDtypeStruct((T, D), F32), jax.ShapeDtypeStruct((T, D), F32)],
        scratch_shapes=[pltpu.VMEM((1, D), F32)],
        compiler_params=_cparams(("arbitrary",)),
        name="norm_shift",
    )(x, g.reshape(1, D))


CUMSUM_ROWS = 512


def _cumsum_kernel(log_sigmoid, x_ref, b_ref, init_ref, f_ref, c_ref, carry):
    tp = x_ref.shape[0]

    @pl.when(pl.program_id(0) == 0)
    def _():
        carry[...] = init_ref[...]

    x = x_ref[...]
    if log_sigmoid:
        x = -(x + b_ref[...])
        x = -(jnp.maximum(x, 0.0) + jnp.log(1.0 + jnp.exp(-jnp.abs(x))))
    f_ref[...] = x
    r = lax.broadcasted_iota(jnp.int32, (tp, tp), 0)
    c = lax.broadcasted_iota(jnp.int32, (tp, tp), 1)
    y = _cumsum_rows(jnp.where(c <= r, 1.0, 0.0).astype(BF16), x) + carry[...]
    carry[...] = y[tp - 1:tp, :]
    c_ref[...] = y


def cumsum_rows(x, row0, n, init, bias=None):
    C = x.shape[1]
    tp = min(CUMSUM_ROWS, n)
    assert n % tp == 0 and row0 % tp == 0
    rb0 = row0 // tp
    b = jnp.zeros((1, C), F32) if bias is None else bias
    return pl.pallas_call(
        functools.partial(_cumsum_kernel, bias is not None),
        grid=(n // tp,),
        in_specs=[pl.BlockSpec((tp, C), lambda i: (rb0 + i, 0)),
                  pl.BlockSpec((1, C), lambda i: (0, 0)),
                  pl.BlockSpec((1, C), lambda i: (0, 0))],
        out_specs=[pl.BlockSpec((tp, C), lambda i: (i, 0)), pl.BlockSpec((tp, C), lambda i: (i, 0))],
        out_shape=[jax.ShapeDtypeStruct((n, C), F32), jax.ShapeDtypeStruct((n, C), F32)],
        scratch_shapes=[pltpu.VMEM((1, C), F32)],
        compiler_params=_cparams(("arbitrary",)),
        name="cumsum_rows",
    )(x, b, init)


def _mix_proj_kernel(h_ref, p_ref, mu_ref, w_ref, o_ref, l_scr):
    @pl.when(pl.program_id(2) == 0)
    def _():
        h = h_ref[...]
        l_scr[...] = (h + (p_ref[...] - h) * mu_ref[0]).astype(BF16)

    o_ref[...] = _dot(l_scr[...], w_ref[0])


def mix_proj(h, prev, mu, w, tn=1024):
    T, D = h.shape
    tm = _row_tile(T, ROW_TILE_CAP)
    G, _, N = w.shape
    tn = min(tn, N)
    nj = N // tn
    return pl.pallas_call(
        _mix_proj_kernel,
        grid=(pl.cdiv(T, tm), G, nj),
        in_specs=[pl.BlockSpec((tm, D), lambda i, g, j: (i, 0)),
                  pl.BlockSpec((tm, D), lambda i, g, j: (i, 0)),
                  pl.BlockSpec((1, 1, D), lambda i, g, j: (g, 0, 0)),
                  pl.BlockSpec((1, D, tn), lambda i, g, j: (g, 0, j))],
        out_specs=pl.BlockSpec((tm, tn), lambda i, g, j: (i, g * nj + j)),
        out_shape=jax.ShapeDtypeStruct((T, G * N), F32),
        scratch_shapes=[pltpu.VMEM((tm, D), BF16)],
        compiler_params=_cparams(("parallel", "arbitrary", "arbitrary")),
        name="mix_proj",
    )(h, prev, mu.reshape(G, 1, D), w)


def _lora_kernel(use_tanh, x_ref, w_ref, b_ref, o_ref):
    x = x_ref[...]
    if use_tanh:
        x = jnp.tanh(x)
    o_ref[...] = b_ref[...] + _dot(x.astype(BF16), w_ref[...])


def lora_out(x, col_block, w, b, use_tanh):
    T = x.shape[0]
    tm = _row_tile(T, ROW_TILE_CAP)
    K, N = w.shape
    return pl.pallas_call(
        functools.partial(_lora_kernel, use_tanh),
        grid=(pl.cdiv(T, tm),),
        in_specs=[pl.BlockSpec((tm, K), lambda i: (i, col_block)),
                  pl.BlockSpec((K, N), lambda i: (0, 0)),
                  pl.BlockSpec((1, N), lambda i: (0, 0))],
        out_specs=pl.BlockSpec((tm, N), lambda i: (i, 0)),
        out_shape=jax.ShapeDtypeStruct((T, N), F32),
        compiler_params=_cparams(("parallel",)),
        name="lora_out",
    )(x, w, b.reshape(1, N))


def _out_proj_kernel(tn, a_ref, w_ref, x_ref, g_ref, o_ref, y_scr):
    j = pl.program_id(1)
    off = pl.multiple_of(j * tn, tn)
    y_scr[:, pl.ds(off, tn)] = _dot(a_ref[...], w_ref[...])

    @pl.when(j == pl.num_programs(1) - 1)
    def _():
        y = y_scr[...]
        ms = jnp.mean(y * y, axis=-1, keepdims=True)
        o_ref[...] = x_ref[...] + y * lax.rsqrt(ms + NORM_EPS) * g_ref[...]


def out_proj(a, w, x, g, tn=512):
    T, K = a.shape
    tm = _row_tile(T, ROW_TILE_CAP)
    D = w.shape[1]
    return pl.pallas_call(
        functools.partial(_out_proj_kernel, tn),
        grid=(pl.cdiv(T, tm), D // tn),
        in_specs=[pl.BlockSpec((tm, K), lambda i, j: (i, 0)),
                  pl.BlockSpec((K, tn), lambda i, j: (0, j)),
                  pl.BlockSpec((tm, D), lambda i, j: (i, 0)),
                  pl.BlockSpec((1, D), lambda i, j: (0, 0))],
        out_specs=pl.BlockSpec((tm, D), lambda i, j: (i, 0)),
        out_shape=jax.ShapeDtypeStruct((T, D), F32),
        scratch_shapes=[pltpu.VMEM((tm, D), F32)],
        compiler_params=_cparams(("parallel", "arbitrary")),
        name="out_proj",
    )(a, w, x, g.reshape(1, D))


GDN_NU = 4


def _gdn_intra_kernel(L, qn_ref, kn_ref, vn_ref, qp_ref, kp_ref, vp_ref, cq_ref, ck_ref, cv_ref,
                      wq_ref, wk_ref, wv_ref, gt_ref, al_ref, dt_ref,
                      uv_ref, wqe_ref, kt_ref, qkd_ref, egl_ref, qbuf, kbuf, vbuf):
    c = pl.program_id(2)
    HD, NH = GDN_HEAD, STACK
    R = NH * L
    first = c == 0

    def conv_silu(buf, u_ref, p_ref, st_ref, w_ref):
        buf[CONV_PAD - 3:CONV_PAD, :] = jnp.where(first, st_ref[0], p_ref[SUBLANES - 3:SUBLANES, :])
        buf[CONV_PAD:CONV_PAD + L, :] = u_ref[...]
        acc = buf[CONV_PAD - 3:CONV_PAD - 3 + L, :] * w_ref[0:1, :]
        acc = acc + buf[CONV_PAD - 2:CONV_PAD - 2 + L, :] * w_ref[1:2, :]
        acc = acc + buf[CONV_PAD - 1:CONV_PAD - 1 + L, :] * w_ref[2:3, :]
        acc = acc + buf[CONV_PAD:CONV_PAD + L, :] * w_ref[3:4, :]
        return _silu(acc)

    q = conv_silu(qbuf, qn_ref, qp_ref, cq_ref, wq_ref)
    k = conv_silu(kbuf, kn_ref, kp_ref, ck_ref, wk_ref)
    v = conv_silu(vbuf, vn_ref, vp_ref, cv_ref, wv_ref)

    rl = lax.broadcasted_iota(jnp.int32, (L, L), 0)
    cl = lax.broadcasted_iota(jnp.int32, (L, L), 1)
    tril = jnp.where(cl <= rl, 1.0, 0.0).astype(BF16)
    triu = jnp.where(rl <= cl, 1.0, 0.0).astype(BF16)
    row = lax.broadcasted_iota(jnp.int32, (R, R), 0)
    col = lax.broadcasted_iota(jnp.int32, (R, R), 1)
    diff = row ^ col
    causal = (diff < L) & (col <= row)
    strict = (diff < L) & (col < row)
    eye = jnp.where(row == col, 1.0, 0.0).astype(F32)

    def l2n(x):
        return x * lax.rsqrt(jnp.sum(x * x, axis=-1, keepdims=True) + L2_EPS)

    def col_stack(x, first_lane):
        return jnp.concatenate([x[:, first_lane + j:first_lane + j + 1] for j in range(NH)], axis=0)

    kst, qst, vst, gst, bst, glast_st, grow_st, glast = [], [], [], [], [], [], [], []
    for u in range(GDN_NU):
        qh = [l2n(q[:, (2 * u + i) * HD:(2 * u + i + 1) * HD]) * (HD ** -0.5) for i in range(2)]
        kh = [l2n(k[:, (2 * u + i) * HD:(2 * u + i + 1) * HD]) for i in range(2)]
        kst.append(jnp.concatenate([kh[0], kh[0], kh[1], kh[1]], axis=0))
        qst.append(jnp.concatenate([qh[0], qh[0], qh[1], qh[1]], axis=0))
        vst.append(_stack_heads(v[:, u * NH * HD:(u + 1) * NH * HD], NH, HD))
        lanes = slice(u * LANES, (u + 1) * LANES)
        gu = gt_ref[:, lanes]
        beta = jax.nn.sigmoid(gu)
        sp = gu + dt_ref[:, lanes]
        sp = jnp.maximum(sp, 0.0) + jnp.log(1.0 + jnp.exp(-jnp.abs(sp)))
        g = -jnp.exp(al_ref[:, lanes]) * sp
        gcol = _cumsum_rows(tril, g)
        grow = _cumsum_cols_t(g, triu)
        gl = gcol[L - 1:L, :]
        glast.append(gl)
        gst.append(col_stack(gcol, NH))
        bst.append(col_stack(beta, 0))
        glast_st.append(jnp.concatenate([jnp.broadcast_to(gl[:, NH + j:NH + j + 1], (L, 1)) for j in range(NH)],
                                        axis=0))
        grow_st.append(jnp.concatenate([grow[NH + j:NH + j + 1, :] for j in range(NH)], axis=1))

    U = range(GDN_NU)
    kk = [_mm_nt(kst[u], kst[u]) for u in U]
    qk = [_mm_nt(qst[u], kst[u]) for u in U]
    decay = [jnp.where(causal, jnp.exp(jnp.where(causal, gst[u] - grow_st[u], 0.0)), 0.0) for u in U]
    a = [jnp.where(strict, kk[u] * decay[u] * bst[u], 0.0) for u in U]
    tinv = _tri_inv(a, diff, eye, L)
    rhs = [jnp.concatenate([vst[u] * bst[u], kst[u] * (bst[u] * jnp.exp(gst[u]))], axis=1) for u in U]
    sol = [_mm(tinv[u], rhs[u]) for u in U]
    for u in U:
        uv_ref[0, u] = sol[u][:, :HD]
        wk = sol[u][:, HD:]
        qe = qst[u] * jnp.exp(gst[u])
        wqe_ref[0, u] = jnp.concatenate(
            [x[j * L:(j + 1) * L] for j in range(NH) for x in (wk, qe)], axis=0).astype(BF16)
        kt_ref[0, u] = (kst[u] * jnp.exp(glast_st[u] - gst[u])).astype(BF16)
        qkd_ref[0, u] = (qk[u] * decay[u]).astype(BF16)
        egl_ref[0, u] = jnp.exp(glast[u])


def _gdn_inter_kernel(L, HG, uv_ref, wqe_ref, kt_ref, qkd_ref, egl_ref, z_ref, s0_ref, on_ref,
                      o_ref, so_ref, s_scr):
    c = pl.program_id(1)
    HD, NH = GDN_HEAD, STACK
    R = NH * L

    @pl.when(c == 0)
    def _():
        for g in range(HG):
            s_scr[g] = jnp.concatenate([s0_ref[0, NH * g + j] for j in range(NH)], axis=1)

    rh = lax.shift_right_logical(lax.broadcasted_iota(jnp.int32, (R, NH * HD), 0), _log2(L))
    ch = lax.shift_right_logical(lax.broadcasted_iota(jnp.int32, (R, NH * HD), 1), _log2(HD))
    own = rh == ch

    s_old = [s_scr[g] for g in range(HG)]
    d1 = [[_dot(wqe_ref[0, g, 2 * L * j:2 * L * (j + 1), :], s_old[g][:, j * HD:(j + 1) * HD].astype(BF16))
           for j in range(NH)] for g in range(HG)]
    v_new = [uv_ref[0, g] - jnp.concatenate([d1[g][j][:L] for j in range(NH)], axis=0) for g in range(HG)]
    o_st = [jnp.concatenate([d1[g][j][L:] for j in range(NH)], axis=0) + _dot(qkd_ref[0, g], v_new[g].astype(BF16))
            for g in range(HG)]
    for g in range(HG):
        v_exp = jnp.where(own, jnp.concatenate([v_new[g]] * NH, axis=1), 0.0)
        egl = egl_ref[0, g]
        e_exp = jnp.concatenate([jnp.broadcast_to(egl[:, NH + j:NH + j + 1], (1, HD)) for j in range(NH)], axis=1)
        s_scr[g] = s_old[g] * e_exp + _mm_tn(kt_ref[0, g], v_exp)

    outs = []
    for g in range(HG):
        for j in range(NH):
            o = o_st[g][j * L:(j + 1) * L]
            zj = z_ref[:, (NH * g + j) * HD:(NH * g + j + 1) * HD]
            ms = jnp.mean(o * o, axis=-1, keepdims=True)
            outs.append((o * lax.rsqrt(ms + NORM_EPS) * on_ref[...] * _silu(zj)).astype(BF16))
    o_ref[...] = jnp.concatenate(outs, axis=1)

    @pl.when(c == pl.num_programs(1) - 1)
    def _():
        for g in range(HG):
            for j in range(NH):
                so_ref[0, NH * g + j] = s_scr[g][:, j * HD:(j + 1) * HD]


def gdn_mix(proj, gates, a_par, dt_par, conv_state, s0, conv_w, o_norm, row0, B, NC, L):
    HV = s0.shape[1]
    HD, NH, NU = GDN_HEAD, STACK, GDN_NU
    HG = HV // NH
    QW = NU * (NH // 2) * HD
    VW = NU * NH * HD
    nq = (HV // 2) * HD // QW
    nv = HV * HD // VW
    Tseg = B * NC * L
    R = NH * L
    rb0 = row0 // L
    assert row0 % L == 0 and L % SUBLANES == 0 and HG % NU == 0

    cq = conv_state[:, :, :nq * QW]
    ck = conv_state[:, :, nq * QW:2 * nq * QW]
    cv = conv_state[:, :, 2 * nq * QW:]
    wq = conv_w[:, :nq * QW]
    wk = conv_w[:, nq * QW:2 * nq * QW]
    wv = conv_w[:, 2 * nq * QW:]

    chunk = lambda b, h, c: rb0 + b * NC + c
    before = lambda b, h, c: jnp.maximum(chunk(b, h, c) * (L // SUBLANES) - 1, 0)
    intra_in = [
        pl.BlockSpec((L, QW), lambda b, h, c: (chunk(b, h, c), h)),
        pl.BlockSpec((L, QW), lambda b, h, c: (chunk(b, h, c), nq + h)),
        pl.BlockSpec((L, VW), lambda b, h, c: (chunk(b, h, c), nv + h)),
        pl.BlockSpec((SUBLANES, QW), lambda b, h, c: (before(b, h, c), h)),
        pl.BlockSpec((SUBLANES, QW), lambda b, h, c: (before(b, h, c), nq + h)),
        pl.BlockSpec((SUBLANES, VW), lambda b, h, c: (before(b, h, c), nv + h)),
        pl.BlockSpec((1, GDN_CONV - 1, QW), lambda b, h, c: (b, 0, h)),
        pl.BlockSpec((1, GDN_CONV - 1, QW), lambda b, h, c: (b, 0, h)),
        pl.BlockSpec((1, GDN_CONV - 1, VW), lambda b, h, c: (b, 0, h)),
        pl.BlockSpec((GDN_CONV, QW), lambda b, h, c: (0, h)),
        pl.BlockSpec((GDN_CONV, QW), lambda b, h, c: (0, h)),
        pl.BlockSpec((GDN_CONV, VW), lambda b, h, c: (0, h)),
        pl.BlockSpec((L, NU * LANES), lambda b, h, c: (chunk(b, h, c), h)),
        pl.BlockSpec((1, NU * LANES), lambda b, h, c: (0, h)),
        pl.BlockSpec((1, NU * LANES), lambda b, h, c: (0, h)),
    ]
    unit = lambda w: pl.BlockSpec((1, NU, w[0], w[1]), lambda b, h, c: (b * NC + c, h, 0, 0))
    NCH = B * NC
    uv, wqe, kt, qkd, egl = pl.pallas_call(
        functools.partial(_gdn_intra_kernel, L),
        grid=(B, HG // NU, NC),
        in_specs=intra_in,
        out_specs=[unit((R, HD)), unit((2 * R, HD)), unit((R, HD)), unit((R, R)), unit((1, LANES))],
        out_shape=[jax.ShapeDtypeStruct((NCH, HG, R, HD), F32),
                   jax.ShapeDtypeStruct((NCH, HG, 2 * R, HD), BF16),
                   jax.ShapeDtypeStruct((NCH, HG, R, HD), BF16),
                   jax.ShapeDtypeStruct((NCH, HG, R, R), BF16),
                   jax.ShapeDtypeStruct((NCH, HG, 1, LANES), F32)],
        scratch_shapes=[pltpu.VMEM((CONV_PAD + L, QW), F32),
                        pltpu.VMEM((CONV_PAD + L, QW), F32),
                        pltpu.VMEM((CONV_PAD + L, VW), F32)],
        compiler_params=_cparams(("parallel", "parallel", "parallel")),
        name="gdn_intra",
    )(proj, proj, proj, proj, proj, proj, cq, ck, cv, wq, wk, wv, gates, a_par, dt_par)

    allu = lambda w: pl.BlockSpec((1, HG, w[0], w[1]), lambda b, c: (b * NC + c, 0, 0, 0))
    zblk = 2 * nv * VW // (HV * HD)
    o, s_new = pl.pallas_call(
        functools.partial(_gdn_inter_kernel, L, HG),
        grid=(B, NC),
        in_specs=[allu((R, HD)), allu((2 * R, HD)), allu((R, HD)), allu((R, R)), allu((1, LANES)),
                  pl.BlockSpec((L, HV * HD), lambda b, c: (rb0 + b * NC + c, zblk)),
                  pl.BlockSpec((1, HV, HD, HD), lambda b, c: (b, 0, 0, 0)),
                  pl.BlockSpec((1, HD), lambda b, c: (0, 0))],
        out_specs=[pl.BlockSpec((L, HV * HD), lambda b, c: (b * NC + c, 0)),
                   pl.BlockSpec((1, HV, HD, HD), lambda b, c: (b, 0, 0, 0))],
        out_shape=[jax.ShapeDtypeStruct((Tseg, HV * HD), BF16),
                   jax.ShapeDtypeStruct((B, HV, HD, HD), F32)],
        scratch_shapes=[pltpu.VMEM((HG, HD, NH * HD), F32)],
        compiler_params=_cparams(("parallel", "arbitrary")),
        name="gdn_inter",
    )(uv, wqe, kt, qkd, egl, proj, s0, o_norm.reshape(1, HD))
    return o, s_new


FOX_NH = 2
FOX_TP = 512
LOG2E = 1.4426950408889634


def _fox_prompt_kernel(TQ, scale, q_ref, k_ref, v_ref, km_ref, vm_ref, z_ref, ck_ref, cm_ref, o_ref):
    qi = pl.program_id(1)
    HD = FOX_HEAD
    HS = range(FOX_NH)
    hs = lambda i: slice(i * HD, (i + 1) * HD)
    q = [(q_ref[:, hs(i)] * (scale * LOG2E)).astype(BF16) for i in HS]

    s = [_mm_nt(q[i], km_ref[:, hs(i)]) - cm_ref[i] for i in HS]
    m = [jnp.max(s[i], axis=-1, keepdims=True) for i in HS]
    p = [jnp.exp2(s[i] - m[i]) for i in HS]
    l = [jnp.sum(p[i], axis=-1, keepdims=True) for i in HS]
    acc = [_mm(p[i], vm_ref[:, hs(i)]) for i in HS]

    def step(kb, carry, diagonal):
        m, l, acc = carry
        off = pl.multiple_of(kb * TQ, TQ)
        s = [_mm_nt(q[i], k_ref[pl.ds(off, TQ), hs(i)]) - ck_ref[i, :, pl.ds(off, TQ)] for i in HS]
        if diagonal:
            row = lax.broadcasted_iota(jnp.int32, (TQ, TQ), 0)
            col = lax.broadcasted_iota(jnp.int32, (TQ, TQ), 1)
            s = [jnp.where(col <= row, s[i], -jnp.inf) for i in HS]
        m_new = [jnp.maximum(m[i], jnp.max(s[i], axis=-1, keepdims=True)) for i in HS]
        alpha = [jnp.exp2(m[i] - m_new[i]) for i in HS]
        p = [jnp.exp2(s[i] - m_new[i]) for i in HS]
        l = [alpha[i] * l[i] + jnp.sum(p[i], axis=-1, keepdims=True) for i in HS]
        acc = [alpha[i] * acc[i] + _mm(p[i], v_ref[pl.ds(off, TQ), hs(i)]) for i in HS]
        return tuple(m_new), tuple(l), tuple(acc)

    carry = lax.fori_loop(0, qi, lambda kb, cr: step(kb, cr, False), (tuple(m), tuple(l), tuple(acc)))
    m, l, acc = step(qi, carry, True)
    o_ref[...] = jnp.concatenate([(acc[i] / l[i]) * _silu(z_ref[:, hs(i)]) for i in HS], axis=1).astype(BF16)


def fox_prompt(qz, kv16, c_main, c_meta, T, meta_row0, n_meta, H, TQ=512):
    HD = FOX_HEAD
    W = FOX_NH * HD
    HP = H // FOX_NH
    TQ = min(TQ, T)
    assert T % TQ == 0 and meta_row0 % n_meta == 0 and H % FOX_NH == 0
    mb = meta_row0 // n_meta
    return pl.pallas_call(
        functools.partial(_fox_prompt_kernel, TQ, HD ** -0.5),
        grid=(HP, T // TQ),
        in_specs=[pl.BlockSpec((TQ, W), lambda h, i: (i, h)),
                  pl.BlockSpec((T, W), lambda h, i: (0, h)),
                  pl.BlockSpec((T, W), lambda h, i: (0, HP + h)),
                  pl.BlockSpec((n_meta, W), lambda h, i: (mb, h)),
                  pl.BlockSpec((n_meta, W), lambda h, i: (mb, HP + h)),
                  pl.BlockSpec((TQ, W), lambda h, i: (i, HP + h)),
                  pl.BlockSpec((FOX_NH, 1, T), lambda h, i: (h, 0, 0)),
                  pl.BlockSpec((FOX_NH, 1, n_meta), lambda h, i: (h, 0, 0))],
        out_specs=pl.BlockSpec((TQ, W), lambda h, i: (i, h)),
        out_shape=jax.ShapeDtypeStruct((T, H * HD), BF16),
        compiler_params=_cparams(("parallel", "arbitrary"), VMEM_LIMIT_BIG),
        name="fox_prompt",
    )(qz, kv16, kv16, kv16, kv16, qz, c_main, c_meta)


def _fox_seq_kernel(has_cache, scale, q_ref, k_ref, v_ref, z_ref, cn_ref, *rest):
    if has_cache:
        kc_ref, vc_ref, cc_ref, o_ref, m_scr, l_scr, a_scr = rest
    else:
        o_ref, m_scr, l_scr, a_scr = rest
    c = pl.program_id(1)
    TQ = q_ref.shape[0]
    HD = FOX_HEAD
    H = q_ref.shape[1] // HD
    HS = range(H)
    hs = lambda h: slice(h * HD, (h + 1) * HD)
    q = [(q_ref[:, hs(h)] * (scale * LOG2E)).astype(BF16) for h in HS]

    @pl.when(c == 0)
    def _():
        row = lax.broadcasted_iota(jnp.int32, (TQ, TQ), 0)
        col = lax.broadcasted_iota(jnp.int32, (TQ, TQ), 1)
        s = [jnp.where(col <= row, _mm_nt(q[h], k_ref[:, hs(h)]) - cn_ref[0, h], -jnp.inf) for h in HS]
        m = [jnp.max(s[h], axis=-1, keepdims=True) for h in HS]
        p = [jnp.exp2(s[h] - m[h]) for h in HS]
        for h in HS:
            m_scr[h] = m[h]
            l_scr[h] = jnp.sum(p[h], axis=-1, keepdims=True)
            a_scr[h] = _mm(p[h], v_ref[:, hs(h)])

    if has_cache:
        TP = cc_ref.shape[-1]
        s = [_mm_nt(q[h], kc_ref[pl.ds(h, TP, stride=H), :]) - cc_ref[0, h] for h in HS]
        m_old = [m_scr[h] for h in HS]
        l_old = [l_scr[h] for h in HS]
        a_old = [a_scr[h] for h in HS]
        m_new = [jnp.maximum(m_old[h], jnp.max(s[h], axis=-1, keepdims=True)) for h in HS]
        alpha = [jnp.exp2(m_old[h] - m_new[h]) for h in HS]
        p = [jnp.exp2(s[h] - m_new[h]) for h in HS]
        pv = [_mm(p[h], vc_ref[pl.ds(h, TP, stride=H), :]) for h in HS]
        for h in HS:
            m_scr[h] = m_new[h]
            l_scr[h] = alpha[h] * l_old[h] + jnp.sum(p[h], axis=-1, keepdims=True)
            a_scr[h] = alpha[h] * a_old[h] + pv[h]

    @pl.when(c == pl.num_programs(1) - 1)
    def _():
        o_ref[...] = jnp.concatenate([(a_scr[h] / l_scr[h]) * _silu(z_ref[:, hs(h)]) for h in HS],
                                     axis=1).astype(BF16)


def fox_seq(qz, kv16, c_new, row0, B, TQ, H, cache=None):
    HD = FOX_HEAD
    FD = H * HD
    assert row0 % TQ == 0
    rb0 = row0 // TQ
    tok = lambda sec: pl.BlockSpec((TQ, FD), lambda b, c: (rb0 + b, sec))
    in_specs = [tok(0), tok(0), tok(1), tok(1), pl.BlockSpec((1, H, 1, TQ), lambda b, c: (b, 0, 0, 0))]
    args = [qz, kv16, kv16, qz, c_new]
    nck = 1
    if cache is not None:
        kc, vc, cc, b0 = cache
        P = cc.shape[-1]
        TP = min(FOX_TP, P)
        assert P % TP == 0
        nck = P // TP
        blk = pl.BlockSpec((TP * H, HD), lambda b, c: ((b0 + b) * nck + c, 0))
        in_specs += [blk, blk, pl.BlockSpec((1, H, 1, TP), lambda b, c: (b, 0, 0, c))]
        args += [kc, vc, cc]
    return pl.pallas_call(
        functools.partial(_fox_seq_kernel, cache is not None, HD ** -0.5),
        grid=(B, nck),
        in_specs=in_specs,
        out_specs=pl.BlockSpec((TQ, FD), lambda b, c: (b, 0)),
        out_shape=jax.ShapeDtypeStruct((B * TQ, FD), BF16),
        scratch_shapes=[pltpu.VMEM((H, TQ, 1), F32), pltpu.VMEM((H, TQ, 1), F32), pltpu.VMEM((H, TQ, HD), F32)],
        compiler_params=_cparams(("parallel", "arbitrary")),
        name="fox_seq",
    )(*args)


WKV_NU = 4


def _wkv_intra_kernel(L, r_ref, k_ref, v_ref, wp_ref, ap_ref, kk_ref, ka_ref, rk_ref,
                      x_ref, u0_ref, o0_ref, arb_ref, vk_ref, bh_ref, gl_ref, bonus_ref):
    HD, NH = RWKV_HEAD, STACK
    R = NH * L
    W = NH * HD

    rl = lax.broadcasted_iota(jnp.int32, (L, L), 0)
    cl = lax.broadcasted_iota(jnp.int32, (L, L), 1)
    tril = jnp.where(cl <= rl, 1.0, 0.0).astype(BF16)
    row = lax.broadcasted_iota(jnp.int32, (R, R), 0)
    col = lax.broadcasted_iota(jnp.int32, (R, R), 1)
    diff = row ^ col
    causal = (diff < L) & (col <= row)
    strict = (diff < L) & (col < row)
    eye = jnp.where(row == col, 1.0, 0.0).astype(F32)

    heads = [slice(j * HD, (j + 1) * HD) for j in range(WKV_NU * NH)]
    r = r_ref[...]
    v = v_ref[...]
    k_raw = k_ref[...]
    wp = -wp_ref[...]
    w = -(jnp.maximum(wp, 0.0) + jnp.log(1.0 + jnp.exp(-jnp.abs(wp)))) - 0.5
    lw = -jnp.exp(w)
    a_sig = jax.nn.sigmoid(ap_ref[...])
    kk = k_raw * kk_ref[...]
    kk = jnp.concatenate(
        [kk[:, h] * lax.rsqrt(jnp.sum(kk[:, h] * kk[:, h], axis=-1, keepdims=True) + L2_EPS) for h in heads], axis=1)
    k = k_raw * (1.0 + (a_sig - 1.0) * ka_ref[...])
    a = -kk
    b = kk * a_sig
    rk = r * k * rk_ref[...]
    bonus_ref[...] = jnp.concatenate([jnp.sum(rk[:, h], axis=-1, keepdims=True) * v[:, h] for h in heads], axis=1)

    cum = _cumsum_rows(tril, lw)
    clast = cum[L - 1:L, :]
    e_neg = jnp.exp(-cum)
    e_tail = jnp.exp(clast - cum)
    rt = r * jnp.exp(cum)
    at = a * jnp.exp(cum - lw)
    kt = k * e_neg
    bt = b * e_neg
    khat = k * e_tail
    bhat = b * e_tail
    gl_ref[0] = jnp.exp(clast)

    U = range(WKV_NU)
    sl = lambda x, u: x[:, u * W:(u + 1) * W]
    st = lambda x, u: _stack_heads(sl(x, u), NH, HD)
    v_st = [st(v, u) for u in U]
    aa = [_mm_nt(jnp.concatenate([st(at, u), st(rt, u)], axis=0),
                 jnp.concatenate([st(bt, u), st(kt, u)], axis=0)) for u in U]
    a_ab = [jnp.where(strict, aa[u][:R, :R], 0.0) for u in U]
    a_ak = [jnp.where(strict, aa[u][:R, R:], 0.0) for u in U]
    a_rb = [jnp.where(causal, aa[u][R:, :R], 0.0) for u in U]
    a_rk = [jnp.where(causal, aa[u][R:, R:], 0.0) for u in U]
    av = [_mm(jnp.concatenate([a_ak[u], a_rk[u]], axis=0), v_st[u]) for u in U]
    vk = [_mm_tn(v_st[u], _expand_heads(sl(khat, u), NH, L, HD)) for u in U]
    tinv = _tri_inv([-x for x in a_ab], diff, eye, L)
    w_exp = [_mm(tinv[u], _expand_heads(sl(at, u), NH, L, HD)) for u in U]
    u0 = [_mm(tinv[u], av[u][:R]) for u in U]
    for u in U:
        x_ref[0, u] = jnp.concatenate([w_exp[u], _expand_heads(sl(rt, u), NH, L, HD)], axis=0).astype(BF16)
        u0_ref[0, u] = u0[u]
        o0_ref[0, u] = av[u][R:]
        arb_ref[0, u] = a_rb[u].astype(BF16)
        vk_ref[0, u] = vk[u]
        bh_ref[0, u] = _expand_heads(sl(bhat, u), NH, L, HD).astype(BF16)


def _wkv_inter_kernel(L, HG, x_ref, u0_ref, o0_ref, arb_ref, vk_ref, bh_ref, gl_ref, s0_ref,
                      bonus_ref, gate_ref, gnw_ref, gnb_ref, o_ref, so_ref, s_scr):
    c = pl.program_id(1)
    HD, NH = RWKV_HEAD, STACK
    R = NH * L
    W = NH * HD

    @pl.when(c == 0)
    def _():
        for g in range(HG):
            s_scr[g] = jnp.concatenate([s0_ref[0, NH * g + j] for j in range(NH)], axis=1)

    s_old = [s_scr[g] for g in range(HG)]
    d1 = [_mm_nt(x_ref[0, g], s_old[g]) for g in range(HG)]
    u_st = [u0_ref[0, g] + d1[g][:R] for g in range(HG)]
    o_st = [d1[g][R:] + o0_ref[0, g] + _dot(arb_ref[0, g], u_st[g].astype(BF16)) for g in range(HG)]
    for g in range(HG):
        s_scr[g] = s_old[g] * gl_ref[0, :, g * W:(g + 1) * W] + vk_ref[0, g] + _mm_tn(u_st[g], bh_ref[0, g])
    outs = []
    for g in range(HG):
        mean = jnp.mean(o_st[g], axis=-1, keepdims=True)
        d = o_st[g] - mean
        var = jnp.mean(d * d, axis=-1, keepdims=True)
        on = d * lax.rsqrt(var + RWKV_GN_EPS)
        outs += [on[j * L:(j + 1) * L] for j in range(NH)]
    o = jnp.concatenate(outs, axis=1) * gnw_ref[...] + gnb_ref[...] + bonus_ref[...]
    o_ref[...] = (o * _silu(gate_ref[...])).astype(BF16)

    @pl.when(c == pl.num_programs(1) - 1)
    def _():
        for g in range(HG):
            for j in range(NH):
                so_ref[0, NH * g + j] = s_scr[g][:, j * HD:(j + 1) * HD]


def wkv_mix(big, w_pre, a_pre, k_k, k_a, r_k, gn_w, gn_b, s0, row0, B, NC, L):
    HN = s0.shape[1]
    HD, NH, NU = RWKV_HEAD, STACK, WKV_NU
    HG = HN // NH
    W = NH * HD
    R = NH * L
    D = HN * HD
    Tseg = B * NC * L
    NCH = B * NC
    rb0 = row0 // L
    nsec = D // (NU * W)
    assert row0 % L == 0 and HG % NU == 0
    tok = lambda sec: pl.BlockSpec((L, NU * W), lambda bb, h, c: (rb0 + bb * NC + c, sec * nsec + h))
    par = pl.BlockSpec((1, NU * W), lambda bb, h, c: (0, h))
    unit = lambda w: pl.BlockSpec((1, NU, w[0], w[1]), lambda bb, h, c: (bb * NC + c, h, 0, 0))
    x, u0, o0, arb, vk, bh, gl, bonus = pl.pallas_call(
        functools.partial(_wkv_intra_kernel, L),
        grid=(B, HG // NU, NC),
        in_specs=[tok(0), tok(1), tok(2), tok(0), tok(0), par, par, par],
        out_specs=[unit((2 * R, W)), unit((R, HD)), unit((R, HD)), unit((R, R)), unit((HD, W)), unit((R, W)),
                   pl.BlockSpec((1, 1, NU * W), lambda bb, h, c: (bb * NC + c, 0, h)),
                   pl.BlockSpec((L, NU * W), lambda bb, h, c: (bb * NC + c, h))],
        out_shape=[jax.ShapeDtypeStruct((NCH, HG, 2 * R, W), BF16),
                   jax.ShapeDtypeStruct((NCH, HG, R, HD), F32),
                   jax.ShapeDtypeStruct((NCH, HG, R, HD), F32),
                   jax.ShapeDtypeStruct((NCH, HG, R, R), BF16),
                   jax.ShapeDtypeStruct((NCH, HG, HD, W), F32),
                   jax.ShapeDtypeStruct((NCH, HG, R, W), BF16),
                   jax.ShapeDtypeStruct((NCH, 1, D), F32),
                   jax.ShapeDtypeStruct((Tseg, D), F32)],
        compiler_params=_cparams(("parallel", "parallel", "parallel")),
        name="wkv_intra",
    )(big, big, big, w_pre, a_pre, k_k.reshape(1, D), k_a.reshape(1, D), r_k.reshape(1, D))

    allu = lambda w: pl.BlockSpec((1, HG, w[0], w[1]), lambda bb, c: (bb * NC + c, 0, 0, 0))
    o, s_new = pl.pallas_call(
        functools.partial(_wkv_inter_kernel, L, HG),
        grid=(B, NC),
        in_specs=[allu((2 * R, W)), allu((R, HD)), allu((R, HD)), allu((R, R)), allu((HD, W)), allu((R, W)),
                  pl.BlockSpec((1, 1, D), lambda bb, c: (bb * NC + c, 0, 0)),
                  pl.BlockSpec((1, HN, HD, HD), lambda bb, c: (bb, 0, 0, 0)),
                  pl.BlockSpec((L, D), lambda bb, c: (bb * NC + c, 0)),
                  pl.BlockSpec((L, D), lambda bb, c: (rb0 + bb * NC + c, 3)),
                  pl.BlockSpec((1, D), lambda bb, c: (0, 0)),
                  pl.BlockSpec((1, D), lambda bb, c: (0, 0))],
        out_specs=[pl.BlockSpec((L, D), lambda bb, c: (bb * NC + c, 0)),
                   pl.BlockSpec((1, HN, HD, HD), lambda bb, c: (bb, 0, 0, 0))],
        out_shape=[jax.ShapeDtypeStruct((Tseg, D), BF16),
                   jax.ShapeDtypeStruct((B, HN, HD, HD), F32)],
        scratch_shapes=[pltpu.VMEM((HG, HD, W), F32)],
        compiler_params=_cparams(("parallel", "arbitrary")),
        name="wkv_inter",
    )(x, u0, o0, arb, vk, bh, gl, s0, bonus, big, gn_w.reshape(1, D), gn_b.reshape(1, D))
    return o, s_new


def _pad_cols(w, n):
    return jnp.pad(w, ((0, 0), (0, n - w.shape[1])))


def _gdn_layer(X, seg, g_pre, g_post, w_in, conv_w, a_log, dt_bias, o_norm, w_out, st_S, st_conv):
    (T, BS, TS, NM) = seg
    HV = a_log.shape[0]
    CD = conv_w.shape[1]
    VD = HV * GDN_HEAD
    D = w_in.shape[0]
    NH = STACK
    HG = HV // NH
    w_main = w_in[:, :CD + VD].astype(BF16)
    wb = w_in[:, CD + VD:CD + VD + HV].reshape(D, HG, NH)
    wa = w_in[:, CD + VD + HV:].reshape(D, HG, NH)
    w_gate = jnp.pad(jnp.concatenate([wb, wa], axis=2), ((0, 0), (0, 0), (0, LANES - 2 * NH)))
    decay_lanes = lambda p: jnp.pad(p.reshape(HG, NH), ((0, 0), (NH, LANES - 2 * NH))).reshape(1, HG * LANES)
    a_par, dt_par = decay_lanes(a_log), decay_lanes(dt_bias)
    proj = norm_proj(X, g_pre, w_main)
    gates = norm_proj(X, g_pre, w_gate.reshape(D, HG * LANES).astype(BF16))
    gp = (gates, a_par, dt_par)

    r_s, r_m = T, T + BS * TS
    tail = GDN_CONV - 1
    c_m = proj[r_m + NM - tail:r_m + NM, :CD][None]
    c_p = proj[T - tail:T, :CD][None]
    c_s = proj[r_s:r_m, :CD].reshape(BS, TS, CD)[:, TS - tail:]
    z_conv = jnp.zeros((1, tail, CD), F32)
    z_S = jnp.zeros((1, HV, GDN_HEAD, GDN_HEAD), F32)
    o_m, S_m = gdn_mix(proj, *gp, z_conv, z_S, conv_w, o_norm, r_m, 1, 1, NM)
    LP = 64
    o_p, S_p = gdn_mix(proj, *gp, c_m, S_m, conv_w, o_norm, 0, 1, T // LP, LP)
    o_s, S_s = gdn_mix(proj, *gp, st_conv, st_S, conv_w, o_norm, r_s, BS, 1, TS)
    o = jnp.concatenate([o_p, o_s, o_m], axis=0)
    X = out_proj(o, w_out.astype(BF16), X, g_post)
    return X, (S_p, c_p, S_s, c_s)


def _fox_layer(X, seg, g_pre, g_post, w_in, b_f, w_out, caches, j):
    (T, BS, TS, NM) = seg
    cache_k, cache_v, cache_logf = caches
    H = b_f.shape[0]
    FD = H * FOX_HEAD
    qz = norm_proj(X, g_pre, jnp.concatenate([w_in[:, :FD], w_in[:, 3 * FD:4 * FD]], axis=1).astype(BF16))
    kv, kv16 = norm_proj(X, g_pre, w_in[:, FD:3 * FD].astype(BF16), also_bf16=True)
    gates = norm_proj(X, g_pre, _pad_cols(w_in[:, 4 * FD:], LANES).astype(BF16))
    r_s, r_m = T, T + BS * TS
    P = cache_k.shape[2]

    bias = _pad_cols(b_f.reshape(1, H), LANES)
    zero = jnp.zeros((1, LANES), F32)
    lf_m, cs_m = cumsum_rows(gates, r_m, NM, zero, bias)
    lf_p, cs_p = cumsum_rows(gates, 0, T, cs_m[NM - 1:NM], bias)
    lf_s, _ = cumsum_rows(gates, r_s, BS * TS, zero, bias)
    lanes_to_heads = lambda c, n: (c[:, :H] * LOG2E).T.reshape(H, 1, n)
    c_meta, c_main = lanes_to_heads(cs_m, NM), lanes_to_heads(cs_p, T)
    _, cs_c = cumsum_rows(cache_logf[j].transpose(1, 0, 2).reshape(P, BS * H), 0, P, jnp.zeros((1, BS * H), F32))
    lf_new = lf_s[:, :H].reshape(BS, TS, H).transpose(1, 0, 2).reshape(TS, BS * H)
    _, cs_n = cumsum_rows(lf_new, 0, TS, cs_c[P - 1:P])
    pos_last = lambda c, n: (c * LOG2E).reshape(n, BS, H).transpose(1, 2, 0).reshape(BS, H, 1, n)
    c_cache, c_new = pos_last(cs_c, P), pos_last(cs_n, TS)
    logf = jnp.concatenate([lf_p[:, :H], lf_s[:, :H], lf_m[:, :H]], axis=0)

    o_p = fox_prompt(qz, kv16, c_main, c_meta, T, r_m, NM, H)
    o_m = fox_seq(qz, kv16, c_meta.reshape(1, H, 1, NM), r_m, 1, NM, H)
    o_s = fox_seq(qz, kv16, c_new, r_s, BS, TS, H,
                  cache=(cache_k.reshape(-1, FOX_HEAD), cache_v.reshape(-1, FOX_HEAD), c_cache, j * BS))
    o = jnp.concatenate([o_p, o_s, o_m], axis=0)
    X = out_proj(o, w_out.astype(BF16), X, g_post)

    def seq(a, n):
        return jnp.concatenate([a[r_m:], a[:T]], axis=0).reshape(1, NM + T, H, n)

    k_all = kv[:, :FD]
    v_all = kv[:, FD:]
    outs = (seq(k_all, FOX_HEAD), seq(v_all, FOX_HEAD),
            jnp.concatenate([logf[r_m:], logf[:T]], axis=0).reshape(1, NM + T, H),
            k_all[r_s:r_m].reshape(BS, TS, H, FOX_HEAD), v_all[r_s:r_m].reshape(BS, TS, H, FOX_HEAD),
            logf[r_s:r_m].reshape(BS, TS, H))
    return X, outs


def _rwkv_layer(X, seg, g_pre, g_post, mu, w_r, w_k, w_v, w_g, w0, w1, w2, a0, a1, a2, k_k, k_a, r_k,
                gn_w, gn_b, w_out, st_S, st_shift):
    (T, BS, TS, NM) = seg
    D = X.shape[1]
    HN = r_k.shape[0]
    HD = RWKV_HEAD
    r_s, r_m = T, T + BS * TS
    h, prev = norm_shift(X, g_pre)
    h_s = h[r_s:r_m].reshape(BS, TS, D)
    starts = jnp.concatenate([jnp.zeros((1,), jnp.int32), r_s + TS * jnp.arange(BS, dtype=jnp.int32),
                              jnp.full((1,), r_m, jnp.int32)])
    before = jnp.concatenate([h[r_m + NM - 1:r_m + NM], st_shift, jnp.zeros((1, D), F32)], axis=0)
    prev = prev.at[starts].set(before)

    big = mix_proj(h, prev, mu[jnp.array([0, 2, 3, 5])], jnp.stack([w_r, w_k, w_v, w_g]).astype(BF16))
    lora_w = jnp.stack([_pad_cols(w1, LANES), _pad_cols(a1, LANES)]).astype(BF16)
    lo = mix_proj(h, prev, mu[jnp.array([1, 4])], lora_w)
    pad_rows = lambda w: jnp.pad(w, ((0, LANES - w.shape[0]), (0, 0))).astype(BF16)
    w_pre = lora_out(lo, 0, pad_rows(w2), w0, True)
    a_pre = lora_out(lo, 1, pad_rows(a2), a0, False)

    zS = jnp.zeros((1, HN, HD, HD), F32)
    par = (k_k, k_a, r_k, gn_w, gn_b)
    o_m, S_m = wkv_mix(big, w_pre, a_pre, *par, zS, r_m, 1, 1, NM)
    LP = 64
    o_p, S_p = wkv_mix(big, w_pre, a_pre, *par, S_m, 0, 1, T // LP, LP)
    o_s, S_s = wkv_mix(big, w_pre, a_pre, *par, st_S, r_s, BS, 1, TS)
    o = jnp.concatenate([o_p, o_s, o_m], axis=0)
    X = out_proj(o, w_out.astype(BF16), X, g_post)
    return X, (S_p, h[T - 1:T], S_s, h_s[:, -1])


def kernel(x_prompt, x_sample, state_gdn_S, state_gdn_conv, cache_fox_k, cache_fox_v, cache_fox_logf, state_rwkv_S, state_rwkv_shift, meta, norm_pre, norm_post, gdn_w_in, gdn_conv_w, gdn_a_log, gdn_dt_bias, gdn_o_norm, gdn_w_out, fox_w_in, fox_b_f, fox_w_out, rwkv_mu, rwkv_w_r, rwkv_w_k, rwkv_w_v, rwkv_w_g, rwkv_w0, rwkv_w1, rwkv_w2, rwkv_a0, rwkv_a1, rwkv_a2, rwkv_k_k, rwkv_k_a, rwkv_r_k, rwkv_gn_w, rwkv_gn_b, rwkv_w_out):
    _, T, D = x_prompt.shape
    BS, TS, _ = x_sample.shape
    NM = meta.shape[0]
    depth = norm_pre.shape[0]
    assert x_prompt.shape[0] == 1
    seg = (T, BS, TS, NM)
    X = jnp.concatenate([x_prompt[0], x_sample.reshape(BS * TS, D), meta.astype(x_prompt.dtype)], axis=0)

    gdn_out, fox_out, rwkv_out = [], [], []
    for i in range(depth):
        kind, j = i % 3, i // 3
        if kind == 0:
            X, st = _gdn_layer(X, seg, norm_pre[i], norm_post[i], gdn_w_in[j], gdn_conv_w[j], gdn_a_log[j],
                               gdn_dt_bias[j], gdn_o_norm[j], gdn_w_out[j], state_gdn_S[j], state_gdn_conv[j])
            gdn_out.append(st)
        elif kind == 1:
            X, st = _fox_layer(X, seg, norm_pre[i], norm_post[i], fox_w_in[j], fox_b_f[j], fox_w_out[j],
                               (cache_fox_k, cache_fox_v, cache_fox_logf), j)
            fox_out.append(st)
        else:
            X, st = _rwkv_layer(X, seg, norm_pre[i], norm_post[i], rwkv_mu[j], rwkv_w_r[j], rwkv_w_k[j],
                                rwkv_w_v[j], rwkv_w_g[j], rwkv_w0[j], rwkv_w1[j], rwkv_w2[j], rwkv_a0[j],
                                rwkv_a1[j], rwkv_a2[j], rwkv_k_k[j], rwkv_k_a[j], rwkv_r_k[j], rwkv_gn_w[j],
                                rwkv_gn_b[j], rwkv_w_out[j], state_rwkv_S[j], state_rwkv_shift[j])
            rwkv_out.append(st)

    stack = lambda items, n: jnp.stack([it[n] for it in items])
    y_prompt = X[:T][None]
    y_sample = X[T:T + BS * TS].reshape(BS, TS, D)
    return (y_prompt, y_sample,
            stack(gdn_out, 0), stack(gdn_out, 1),
            stack(fox_out, 0), stack(fox_out, 1), stack(fox_out, 2),
            stack(rwkv_out, 0), stack(rwkv_out, 1),
            stack(gdn_out, 2), stack(gdn_out, 3),
            stack(fox_out, 3), stack(fox_out, 4), stack(fox_out, 5),
            stack(rwkv_out, 2), stack(rwkv_out, 3))
```

```python
import functools

import jax
import jax.numpy as jnp
from jax import lax
from jax.experimental import pallas as pl
from jax.experimental.pallas import tpu as pltpu

F32 = jnp.float32
BF16 = jnp.bfloat16

NORM_EPS = 1e-6
L2_EPS = 1e-6
RWKV_GN_EPS = 64e-5
GDN_HEAD = 128
GDN_CONV = 4
FOX_HEAD = 128
RWKV_HEAD = 64
LANES = 128
SUBLANES = 8
CONV_PAD = 8
INV_BASE = 16
STACK = 4
VMEM_LIMIT = 48 * 1024 * 1024
VMEM_LIMIT_BIG = 56 * 1024 * 1024


def _cparams(sem, limit=VMEM_LIMIT):
    return pltpu.CompilerParams(dimension_semantics=sem, vmem_limit_bytes=limit)


def _log2(n):
    assert n & (n - 1) == 0
    return n.bit_length() - 1


def _dot(a, b):
    return jnp.dot(a, b, preferred_element_type=F32)


def _mm(a, b):
    return _dot(a.astype(BF16), b.astype(BF16))


def _mm_nt(a, b):
    return lax.dot_general(a.astype(BF16), b.astype(BF16), (((1,), (1,)), ((), ())),
                           preferred_element_type=F32)


def _mm_tn(a, b):
    return lax.dot_general(a.astype(BF16), b.astype(BF16), (((0,), (0,)), ((), ())),
                           preferred_element_type=F32)


def _split3(a):
    hi = a.astype(BF16)
    r = a - hi.astype(F32)
    mid = r.astype(BF16)
    lo = (r - mid.astype(F32)).astype(BF16)
    return hi, mid, lo


def _cumsum_rows(tril, g):
    hi, mid, lo = _split3(g)
    return _dot(tril, hi) + (_dot(tril, mid) + _dot(tril, lo))


def _cumsum_cols_t(g, triu):
    hi, mid, lo = _split3(g)
    return _mm_tn(hi, triu) + (_mm_tn(mid, triu) + _mm_tn(lo, triu))


def _tri_inv(a_list, diff, eye, L):
    base = diff < INV_BASE
    n1 = [jnp.where(base, -a, 0.0) for a in a_list]
    p = [eye + n for n in n1]
    npow = n1
    steps = 1
    while 2 * steps < INV_BASE:
        npow = [_mm(x, x) for x in npow]
        p = [pp + _mm(pp, x) for pp, x in zip(p, npow)]
        steps *= 2
    blk = INV_BASE
    while blk < L:
        off_diag = (diff < 2 * blk) & (diff >= blk)
        e = [jnp.where(off_diag, a, 0.0) for a in a_list]
        pe = [_mm(pp, ee) for pp, ee in zip(p, e)]
        p = [pp - _mm(x, pp) for pp, x in zip(p, pe)]
        blk *= 2
    return p


def _stack_heads(x, n, w):
    return jnp.concatenate([x[:, j * w:(j + 1) * w] for j in range(n)], axis=0)


def _expand_heads(x, n, L, w):
    t = jnp.concatenate([x] * n, axis=0)
    rh = lax.shift_right_logical(lax.broadcasted_iota(jnp.int32, t.shape, 0), _log2(L))
    ch = lax.shift_right_logical(lax.broadcasted_iota(jnp.int32, t.shape, 1), _log2(w))
    return jnp.where(rh == ch, t, 0.0)


def _silu(x):
    return x * jax.nn.sigmoid(x)


NORM_PROJ_ROWS = 1600
ROW_TILE_CAP = 544


def _row_tile(T, cap, exact=False):
    best = 0
    for t in range(16, cap + 1, 16):
        if T % t == 0:
            best = t
    assert best > 0
    return best if exact or best >= cap // 4 else (cap // 16) * 16


def _norm_proj_kernel(x_ref, g_ref, w_ref, o_ref, *rest):
    h_scr = rest[-1]

    @pl.when(pl.program_id(1) == 0)
    def _():
        x = x_ref[...]
        ms = jnp.mean(x * x, axis=-1, keepdims=True)
        h_scr[...] = (x * lax.rsqrt(ms + NORM_EPS) * g_ref[...]).astype(BF16)

    y = _dot(h_scr[...], w_ref[...])
    o_ref[...] = y
    if len(rest) == 2:
        rest[0][...] = y.astype(BF16)


def norm_proj(x, g, w, tn=512, also_bf16=False):
    T, D = x.shape
    N = w.shape[1]
    tn = min(tn, N)
    tm = _row_tile(T, NORM_PROJ_ROWS)
    assert N % tn == 0
    out_spec = pl.BlockSpec((tm, tn), lambda i, j: (i, j))
    out_specs, out_shape = out_spec, jax.ShapeDtypeStruct((T, N), F32)
    if also_bf16:
        out_specs, out_shape = [out_spec, out_spec], [out_shape, jax.ShapeDtypeStruct((T, N), BF16)]
    return pl.pallas_call(
        _norm_proj_kernel,
        grid=(pl.cdiv(T, tm), N // tn),
        in_specs=[pl.BlockSpec((tm, D), lambda i, j: (i, 0)),
                  pl.BlockSpec((1, D), lambda i, j: (0, 0)),
                  pl.BlockSpec((D, tn), lambda i, j: (0, j))],
        out_specs=out_specs,
        out_shape=out_shape,
        scratch_shapes=[pltpu.VMEM((tm, D), BF16)],
        compiler_params=_cparams(("parallel", "arbitrary"), VMEM_LIMIT_BIG),
        name="norm_proj",
    )(x, g.reshape(1, D), w)


def _norm_shift_kernel(x_ref, g_ref, h_ref, p_ref, last_scr):
    tm = x_ref.shape[0]

    @pl.when(pl.program_id(0) == 0)
    def _():
        last_scr[...] = jnp.zeros_like(last_scr)

    x = x_ref[...]
    ms = jnp.mean(x * x, axis=-1, keepdims=True)
    h = x * lax.rsqrt(ms + NORM_EPS) * g_ref[...]
    h_ref[...] = h
    p_ref[0:1, :] = last_scr[...]
    p_ref[1:tm, :] = h[0:tm - 1, :]
    last_scr[...] = h[tm - 1:tm, :]


def norm_shift(x, g):
    T, D = x.shape
    tm = _row_tile(T, ROW_TILE_CAP, exact=True)
    spec = pl.BlockSpec((tm, D), lambda i: (i, 0))
    return pl.pallas_call(
        _norm_shift_kernel,
        grid=(T // tm,),
        in_specs=[spec, pl.BlockSpec((1, D), lambda i: (0, 0))],
        out_specs=[spec, spec],
        out_shape=[jax.ShapeDtypeStruct((T, D), F32), jax.ShapeDtypeStruct((T, D), F32)],
        scratch_shapes=[pltpu.VMEM((1, D), F32)],
        compiler_params=_cparams(("arbitrary",)),
        name="norm_shift",
    )(x, g.reshape(1, D))


CUMSUM_ROWS = 512


def _cumsum_kernel(log_sigmoid, x_ref, b_ref, init_ref, f_ref, c_ref, carry):
    tp = x_ref.shape[0]

    @pl.when(pl.program_id(0) == 0)
    def _():
        carry[...] = init_ref[...]

    x = x_ref[...]
    if log_sigmoid:
        x = -(x + b_ref[...])
        x = -(jnp.maximum(x, 0.0) + jnp.log(1.0 + jnp.exp(-jnp.abs(x))))
    f_ref[...] = x
    r = lax.broadcasted_iota(jnp.int32, (tp, tp), 0)
    c = lax.broadcasted_iota(jnp.int32, (tp, tp), 1)
    y = _cumsum_rows(jnp.where(c <= r, 1.0, 0.0).astype(BF16), x) + carry[...]
    carry[...] = y[tp - 1:tp, :]
    c_ref[...] = y


def cumsum_rows(x, row0, n, init, bias=None):
    C = x.shape[1]
    tp = min(CUMSUM_ROWS, n)
    assert n % tp == 0 and row0 % tp == 0
    rb0 = row0 // tp
    b = jnp.zeros((1, C), F32) if bias is None else bias
    return pl.pallas_call(
        functools.partial(_cumsum_kernel, bias is not None),
        grid=(n // tp,),
        in_specs=[pl.BlockSpec((tp, C), lambda i: (rb0 + i, 0)),
                  pl.BlockSpec((1, C), lambda i: (0, 0)),
                  pl.BlockSpec((1, C), lambda i: (0, 0))],
        out_specs=[pl.BlockSpec((tp, C), lambda i: (i, 0)), pl.BlockSpec((tp, C), lambda i: (i, 0))],
        out_shape=[jax.ShapeDtypeStruct((n, C), F32), jax.ShapeDtypeStruct((n, C), F32)],
        scratch_shapes=[pltpu.VMEM((1, C), F32)],
        compiler_params=_cparams(("arbitrary",)),
        name="cumsum_rows",
    )(x, b, init)


def _mix_proj_kernel(h_ref, p_ref, mu_ref, w_ref, o_ref, l_scr):
    @pl.when(pl.program_id(2) == 0)
    def _():
        h = h_ref[...]
        l_scr[...] = (h + (p_ref[...] - h) * mu_ref[0]).astype(BF16)

    o_ref[...] = _dot(l_scr[...], w_ref[0])


def mix_proj(h, prev, mu, w, tn=1024):
    T, D = h.shape
    tm = _row_tile(T, ROW_TILE_CAP)
    G, _, N = w.shape
    tn = min(tn, N)
    nj = N // tn
    return pl.pallas_call(
        _mix_proj_kernel,
        grid=(pl.cdiv(T, tm), G, nj),
        in_specs=[pl.BlockSpec((tm, D), lambda i, g, j: (i, 0)),
                  pl.BlockSpec((tm, D), lambda i, g, j: (i, 0)),
                  pl.BlockSpec((1, 1, D), lambda i, g, j: (g, 0, 0)),
                  pl.BlockSpec((1, D, tn), lambda i, g, j: (g, 0, j))],
        out_specs=pl.BlockSpec((tm, tn), lambda i, g, j: (i, g * nj + j)),
        out_shape=jax.ShapeDtypeStruct((T, G * N), F32),
        scratch_shapes=[pltpu.VMEM((tm, D), BF16)],
        compiler_params=_cparams(("parallel", "arbitrary", "arbitrary")),
        name="mix_proj",
    )(h, prev, mu.reshape(G, 1, D), w)


def _lora_kernel(use_tanh, x_ref, w_ref, b_ref, o_ref):
    x = x_ref[...]
    if use_tanh:
        x = jnp.tanh(x)
    o_ref[...] = b_ref[...] + _dot(x.astype(BF16), w_ref[...])


def lora_out(x, col_block, w, b, use_tanh):
    T = x.shape[0]
    tm = _row_tile(T, ROW_TILE_CAP)
    K, N = w.shape
    return pl.pallas_call(
        functools.partial(_lora_kernel, use_tanh),
        grid=(pl.cdiv(T, tm),),
        in_specs=[pl.BlockSpec((tm, K), lambda i: (i, col_block)),
                  pl.BlockSpec((K, N), lambda i: (0, 0)),
                  pl.BlockSpec((1, N), lambda i: (0, 0))],
        out_specs=pl.BlockSpec((tm, N), lambda i: (i, 0)),
        out_shape=jax.ShapeDtypeStruct((T, N), F32),
        compiler_params=_cparams(("parallel",)),
        name="lora_out",
    )(x, w, b.reshape(1, N))


def _out_proj_kernel(tn, a_ref, w_ref, x_ref, g_ref, o_ref, y_scr):
    j = pl.program_id(1)
    off = pl.multiple_of(j * tn, tn)
    y_scr[:, pl.ds(off, tn)] = _dot(a_ref[...], w_ref[...])

    @pl.when(j == pl.num_programs(1) - 1)
    def _():
        y = y_scr[...]
        ms = jnp.mean(y * y, axis=-1, keepdims=True)
        o_ref[...] = x_ref[...] + y * lax.rsqrt(ms + NORM_EPS) * g_ref[...]


def out_proj(a, w, x, g, tn=512):
    T, K = a.shape
    tm = _row_tile(T, ROW_TILE_CAP)
    D = w.shape[1]
    return pl.pallas_call(
        functools.partial(_out_proj_kernel, tn),
        grid=(pl.cdiv(T, tm), D // tn),
        in_specs=[pl.BlockSpec((tm, K), lambda i, j: (i, 0)),
                  pl.BlockSpec((K, tn), lambda i, j: (0, j)),
                  pl.BlockSpec((tm, D), lambda i, j: (i, 0)),
                  pl.BlockSpec((1, D), lambda i, j: (0, 0))],
        out_specs=pl.BlockSpec((tm, D), lambda i, j: (i, 0)),
        out_shape=jax.ShapeDtypeStruct((T, D), F32),
        scratch_shapes=[pltpu.VMEM((tm, D), F32)],
        compiler_params=_cparams(("parallel", "arbitrary")),
        name="out_proj",
    )(a, w, x, g.reshape(1, D))


GDN_NU = 4


def _gdn_intra_kernel(L, qn_ref, kn_ref, vn_ref, qp_ref, kp_ref, vp_ref, cq_ref, ck_ref, cv_ref,
                      wq_ref, wk_ref, wv_ref, gt_ref, al_ref, dt_ref,
                      uv_ref, wqe_ref, kt_ref, qkd_ref, egl_ref, qbuf, kbuf, vbuf):
    c = pl.program_id(2)
    HD, NH = GDN_HEAD, STACK
    R = NH * L
    first = c == 0

    def conv_silu(buf, u_ref, p_ref, st_ref, w_ref):
        buf[CONV_PAD - 3:CONV_PAD, :] = jnp.where(first, st_ref[0], p_ref[SUBLANES - 3:SUBLANES, :])
        buf[CONV_PAD:CONV_PAD + L, :] = u_ref[...]
        acc = buf[CONV_PAD - 3:CONV_PAD - 3 + L, :] * w_ref[0:1, :]
        acc = acc + buf[CONV_PAD - 2:CONV_PAD - 2 + L, :] * w_ref[1:2, :]
        acc = acc + buf[CONV_PAD - 1:CONV_PAD - 1 + L, :] * w_ref[2:3, :]
        acc = acc + buf[CONV_PAD:CONV_PAD + L, :] * w_ref[3:4, :]
        return _silu(acc)

    q = conv_silu(qbuf, qn_ref, qp_ref, cq_ref, wq_ref)
    k = conv_silu(kbuf, kn_ref, kp_ref, ck_ref, wk_ref)
    v = conv_silu(vbuf, vn_ref, vp_ref, cv_ref, wv_ref)

    rl = lax.broadcasted_iota(jnp.int32, (L, L), 0)
    cl = lax.broadcasted_iota(jnp.int32, (L, L), 1)
    tril = jnp.where(cl <= rl, 1.0, 0.0).astype(BF16)
    triu = jnp.where(rl <= cl, 1.0, 0.0).astype(BF16)
    row = lax.broadcasted_iota(jnp.int32, (R, R), 0)
    col = lax.broadcasted_iota(jnp.int32, (R, R), 1)
    diff = row ^ col
    causal = (diff < L) & (col <= row)
    strict = (diff < L) & (col < row)
    eye = jnp.where(row == col, 1.0, 0.0).astype(F32)

    def l2n(x):
        return x * lax.rsqrt(jnp.sum(x * x, axis=-1, keepdims=True) + L2_EPS)

    def col_stack(x, first_lane):
        return jnp.concatenate([x[:, first_lane + j:first_lane + j + 1] for j in range(NH)], axis=0)

    kst, qst, vst, gst, bst, glast_st, grow_st, glast = [], [], [], [], [], [], [], []
    for u in range(GDN_NU):
        qh = [l2n(q[:, (2 * u + i) * HD:(2 * u + i + 1) * HD]) * (HD ** -0.5) for i in range(2)]
        kh = [l2n(k[:, (2 * u + i) * HD:(2 * u + i + 1) * HD]) for i in range(2)]
        kst.append(jnp.concatenate([kh[0], kh[0], kh[1], kh[1]], axis=0))
        qst.append(jnp.concatenate([qh[0], qh[0], qh[1], qh[1]], axis=0))
        vst.append(_stack_heads(v[:, u * NH * HD:(u + 1) * NH * HD], NH, HD))
        lanes = slice(u * LANES, (u + 1) * LANES)
        gu = gt_ref[:, lanes]
        beta = jax.nn.sigmoid(gu)
        sp = gu + dt_ref[:, lanes]
        sp = jnp.maximum(sp, 0.0) + jnp.log(1.0 + jnp.exp(-jnp.abs(sp)))
        g = -jnp.exp(al_ref[:, lanes]) * sp
        gcol = _cumsum_rows(tril, g)
        grow = _cumsum_cols_t(g, triu)
        gl = gcol[L - 1:L, :]
        glast.append(gl)
        gst.append(col_stack(gcol, NH))
        bst.append(col_stack(beta, 0))
        glast_st.append(jnp.concatenate([jnp.broadcast_to(gl[:, NH + j:NH + j + 1], (L, 1)) for j in range(NH)],
                                        axis=0))
        grow_st.append(jnp.concatenate([grow[NH + j:NH + j + 1, :] for j in range(NH)], axis=1))

    U = range(GDN_NU)
    kk = [_mm_nt(kst[u], kst[u]) for u in U]
    qk = [_mm_nt(qst[u], kst[u]) for u in U]
    decay = [jnp.where(causal, jnp.exp(jnp.where(causal, gst[u] - grow_st[u], 0.0)), 0.0) for u in U]
    a = [jnp.where(strict, kk[u] * decay[u] * bst[u], 0.0) for u in U]
    tinv = _tri_inv(a, diff, eye, L)
    rhs = [jnp.concatenate([vst[u] * bst[u], kst[u] * (bst[u] * jnp.exp(gst[u]))], axis=1) for u in U]
    sol = [_mm(tinv[u], rhs[u]) for u in U]
    for u in U:
        uv_ref[0, u] = sol[u][:, :HD]
        wk = sol[u][:, HD:]
        qe = qst[u] * jnp.exp(gst[u])
        wqe_ref[0, u] = jnp.concatenate(
            [x[j * L:(j + 1) * L] for j in range(NH) for x in (wk, qe)], axis=0).astype(BF16)
        kt_ref[0, u] = (kst[u] * jnp.exp(glast_st[u] - gst[u])).astype(BF16)
        qkd_ref[0, u] = (qk[u] * decay[u]).astype(BF16)
        egl_ref[0, u] = jnp.exp(glast[u])


def _gdn_inter_kernel(L, HG, uv_ref, wqe_ref, kt_ref, qkd_ref, egl_ref, z_ref, s0_ref, on_ref,
                      o_ref, so_ref, s_scr):
    c = pl.program_id(1)
    HD, NH = GDN_HEAD, STACK
    R = NH * L

    @pl.when(c == 0)
    def _():
        for g in range(HG):
            s_scr[g] = jnp.concatenate([s0_ref[0, NH * g + j] for j in range(NH)], axis=1)

    rh = lax.shift_right_logical(lax.broadcasted_iota(jnp.int32, (R, NH * HD), 0), _log2(L))
    ch = lax.shift_right_logical(lax.broadcasted_iota(jnp.int32, (R, NH * HD), 1), _log2(HD))
    own = rh == ch

    s_old = [s_scr[g] for g in range(HG)]
    d1 = [[_dot(wqe_ref[0, g, 2 * L * j:2 * L * (j + 1), :], s_old[g][:, j * HD:(j + 1) * HD].astype(BF16))
           for j in range(NH)] for g in range(HG)]
    v_new = [uv_ref[0, g] - jnp.concatenate([d1[g][j][:L] for j in range(NH)], axis=0) for g in range(HG)]
    o_st = [jnp.concatenate([d1[g][j][L:] for j in range(NH)], axis=0) + _dot(qkd_ref[0, g], v_new[g].astype(BF16))
            for g in range(HG)]
    for g in range(HG):
        v_exp = jnp.where(own, jnp.concatenate([v_new[g]] * NH, axis=1), 0.0)
        egl = egl_ref[0, g]
        e_exp = jnp.concatenate([jnp.broadcast_to(egl[:, NH + j:NH + j + 1], (1, HD)) for j in range(NH)], axis=1)
        s_scr[g] = s_old[g] * e_exp + _mm_tn(kt_ref[0, g], v_exp)

    outs = []
    for g in range(HG):
        for j in range(NH):
            o = o_st[g][j * L:(j + 1) * L]
            zj = z_ref[:, (NH * g + j) * HD:(NH * g + j + 1) * HD]
            ms = jnp.mean(o * o, axis=-1, keepdims=True)
            outs.append((o * lax.rsqrt(ms + NORM_EPS) * on_ref[...] * _silu(zj)).astype(BF16))
    o_ref[...] = jnp.concatenate(outs, axis=1)

    @pl.when(c == pl.num_programs(1) - 1)
    def _():
        for g in range(HG):
            for j in range(NH):
                so_ref[0, NH * g + j] = s_scr[g][:, j * HD:(j + 1) * HD]


def gdn_mix(proj, gates, a_par, dt_par, conv_state, s0, conv_w, o_norm, row0, B, NC, L):
    HV = s0.shape[1]
    HD, NH, NU = GDN_HEAD, STACK, GDN_NU
    HG = HV // NH
    QW = NU * (NH // 2) * HD
    VW = NU * NH * HD
    nq = (HV // 2) * HD // QW
    nv = HV * HD // VW
    Tseg = B * NC * L
    R = NH * L
    rb0 = row0 // L
    assert row0 % L == 0 and L % SUBLANES == 0 and HG % NU == 0

    cq = conv_state[:, :, :nq * QW]
    ck = conv_state[:, :, nq * QW:2 * nq * QW]
    cv = conv_state[:, :, 2 * nq * QW:]
    wq = conv_w[:, :nq * QW]
    wk = conv_w[:, nq * QW:2 * nq * QW]
    wv = conv_w[:, 2 * nq * QW:]

    chunk = lambda b, h, c: rb0 + b * NC + c
    before = lambda b, h, c: jnp.maximum(chunk(b, h, c) * (L // SUBLANES) - 1, 0)
    intra_in = [
        pl.BlockSpec((L, QW), lambda b, h, c: (chunk(b, h, c), h)),
        pl.BlockSpec((L, QW), lambda b, h, c: (chunk(b, h, c), nq + h)),
        pl.BlockSpec((L, VW), lambda b, h, c: (chunk(b, h, c), nv + h)),
        pl.BlockSpec((SUBLANES, QW), lambda b, h, c: (before(b, h, c), h)),
        pl.BlockSpec((SUBLANES, QW), lambda b, h, c: (before(b, h, c), nq + h)),
        pl.BlockSpec((SUBLANES, VW), lambda b, h, c: (before(b, h, c), nv + h)),
        pl.BlockSpec((1, GDN_CONV - 1, QW), lambda b, h, c: (b, 0, h)),
        pl.BlockSpec((1, GDN_CONV - 1, QW), lambda b, h, c: (b, 0, h)),
        pl.BlockSpec((1, GDN_CONV - 1, VW), lambda b, h, c: (b, 0, h)),
        pl.BlockSpec((GDN_CONV, QW), lambda b, h, c: (0, h)),
        pl.BlockSpec((GDN_CONV, QW), lambda b, h, c: (0, h)),
        pl.BlockSpec((GDN_CONV, VW), lambda b, h, c: (0, h)),
        pl.BlockSpec((L, NU * LANES), lambda b, h, c: (chunk(b, h, c), h)),
        pl.BlockSpec((1, NU * LANES), lambda b, h, c: (0, h)),
        pl.BlockSpec((1, NU * LANES), lambda b, h, c: (0, h)),
    ]
    unit = lambda w: pl.BlockSpec((1, NU, w[0], w[1]), lambda b, h, c: (b * NC + c, h, 0, 0))
    NCH = B * NC
    uv, wqe, kt, qkd, egl = pl.pallas_call(
        functools.partial(_gdn_intra_kernel, L),
        grid=(B, HG // NU, NC),
        in_specs=intra_in,
        out_specs=[unit((R, HD)), unit((2 * R, HD)), unit((R, HD)), unit((R, R)), unit((1, LANES))],
        out_shape=[jax.ShapeDtypeStruct((NCH, HG, R, HD), F32),
                   jax.ShapeDtypeStruct((NCH, HG, 2 * R, HD), BF16),
                   jax.ShapeDtypeStruct((NCH, HG, R, HD), BF16),
                   jax.ShapeDtypeStruct((NCH, HG, R, R), BF16),
                   jax.ShapeDtypeStruct((NCH, HG, 1, LANES), F32)],
        scratch_shapes=[pltpu.VMEM((CONV_PAD + L, QW), F32),
                        pltpu.VMEM((CONV_PAD + L, QW), F32),
                        pltpu.VMEM((CONV_PAD + L, VW), F32)],
        compiler_params=_cparams(("parallel", "parallel", "parallel")),
        name="gdn_intra",
    )(proj, proj, proj, proj, proj, proj, cq, ck, cv, wq, wk, wv, gates, a_par, dt_par)

    allu = lambda w: pl.BlockSpec((1, HG, w[0], w[1]), lambda b, c: (b * NC + c, 0, 0, 0))
    zblk = 2 * nv * VW // (HV * HD)
    o, s_new = pl.pallas_call(
        functools.partial(_gdn_inter_kernel, L, HG),
        grid=(B, NC),
        in_specs=[allu((R, HD)), allu((2 * R, HD)), allu((R, HD)), allu((R, R)), allu((1, LANES)),
                  pl.BlockSpec((L, HV * HD), lambda b, c: (rb0 + b * NC + c, zblk)),
                  pl.BlockSpec((1, HV, HD, HD), lambda b, c: (b, 0, 0, 0)),
                  pl.BlockSpec((1, HD), lambda b, c: (0, 0))],
        out_specs=[pl.BlockSpec((L, HV * HD), lambda b, c: (b * NC + c, 0)),
                   pl.BlockSpec((1, HV, HD, HD), lambda b, c: (b, 0, 0, 0))],
        out_shape=[jax.ShapeDtypeStruct((Tseg, HV * HD), BF16),
                   jax.ShapeDtypeStruct((B, HV, HD, HD), F32)],
        scratch_shapes=[pltpu.VMEM((HG, HD, NH * HD), F32)],
        compiler_params=_cparams(("parallel", "arbitrary")),
        name="gdn_inter",
    )(uv, wqe, kt, qkd, egl, proj, s0, o_norm.reshape(1, HD))
    return o, s_new


FOX_NH = 1
FOX_TP = 512
LOG2E = 1.4426950408889634


FOX_BIAS_PIECES = 3


def fox_augment(kv16, c_rows, H):
    R = kv16.shape[0]
    HD = FOX_HEAD
    k = kv16[:, :H * HD].reshape(R, H, HD)
    v = kv16[:, H * HD:].reshape(R, H, HD)
    rest = -c_rows
    pieces = []
    for _ in range(FOX_BIAS_PIECES):
        top = lax.bitcast_convert_type(lax.bitcast_convert_type(rest, jnp.uint32) & jnp.uint32(0xFFFF0000), F32)
        pieces.append(top.astype(BF16))
        rest = rest - top
    kz = jnp.zeros((R, H, HD - FOX_BIAS_PIECES), BF16)
    vz = jnp.zeros((R, H, HD - 1), BF16)
    k_aug = jnp.concatenate([k] + [pc[:, :, None] for pc in pieces] + [kz], axis=2).reshape(R, 2 * H * HD)
    v_aug = jnp.concatenate([v, jnp.ones((R, H, 1), BF16), vz], axis=2).reshape(R, 2 * H * HD)
    return k_aug, v_aug


def _fox_prompt_kernel(TQ, scale, q_ref, k_ref, v_ref, km_ref, vm_ref, z_ref, o_ref, sa_scr, sb_scr):
    qi = pl.program_id(1)
    HD = FOX_HEAD
    HS = range(FOX_NH)
    hs = lambda i: slice(i * HD, (i + 1) * HD)
    hs2 = lambda i: slice(2 * i * HD, 2 * (i + 1) * HD)
    ones = jnp.where(lax.broadcasted_iota(jnp.int32, (TQ, HD), 1) < FOX_BIAS_PIECES, 1.0, 0.0).astype(BF16)
    q = [jnp.concatenate([(q_ref[:, hs(i)] * (scale * LOG2E)).astype(BF16), ones], axis=1) for i in HS]

    s = [_mm_nt(q[i], km_ref[:, hs2(i)]) for i in HS]
    m = [jnp.max(s[i], axis=-1, keepdims=True) for i in HS]
    acc = [_mm(jnp.exp2(s[i] - m[i]), vm_ref[:, hs2(i)]) for i in HS]

    def scores_into(scr, kb):
        off = pl.multiple_of(kb * TQ, TQ)
        for i in HS:
            scr[i] = _mm_nt(q[i], k_ref[pl.ds(off, TQ), hs2(i)])

    def fold(scr, kb, m, acc, diagonal):
        off = pl.multiple_of(kb * TQ, TQ)
        s = [scr[i] for i in HS]
        if diagonal:
            row = lax.broadcasted_iota(jnp.int32, (TQ, TQ), 0)
            col = lax.broadcasted_iota(jnp.int32, (TQ, TQ), 1)
            s = [jnp.where(col <= row, s[i], -jnp.inf) for i in HS]
        m_new = [jnp.maximum(m[i], jnp.max(s[i], axis=-1, keepdims=True)) for i in HS]
        p = [jnp.exp2(s[i] - m_new[i]) for i in HS]
        acc = [jnp.exp2(m[i] - m_new[i]) * acc[i] + _mm(p[i], v_ref[pl.ds(off, TQ), hs2(i)]) for i in HS]
        return tuple(m_new), tuple(acc)

    scores_into(sa_scr, 0)

    def pair(j, carry):
        m, acc = carry
        scores_into(sb_scr, 2 * j + 1)
        m, acc = fold(sa_scr, 2 * j, m, acc, False)
        scores_into(sa_scr, 2 * j + 2)
        return fold(sb_scr, 2 * j + 1, m, acc, False)

    m, acc = lax.fori_loop(0, qi // 2, pair, (tuple(m), tuple(acc)))

    def last_even(m, acc):
        return fold(sa_scr, qi, m, acc, True)

    def last_odd(m, acc):
        scores_into(sb_scr, qi)
        m, acc = fold(sa_scr, qi - 1, m, acc, False)
        return fold(sb_scr, qi, m, acc, True)

    m, acc = lax.cond(qi % 2 == 0, last_even, last_odd, m, acc)
    o_ref[...] = jnp.concatenate([(acc[i][:, :HD] / acc[i][:, HD:HD + 1]) * _silu(z_ref[:, hs(i)]) for i in HS],
                                 axis=1).astype(BF16)


def fox_prompt(qz, k_aug, v_aug, T, meta_row0, n_meta, H, TQ=512):
    HD = FOX_HEAD
    W = FOX_NH * HD
    HP = H // FOX_NH
    TQ = min(TQ, T)
    assert T % TQ == 0 and meta_row0 % n_meta == 0 and H % FOX_NH == 0
    mb = meta_row0 // n_meta
    resident = lambda: pl.BlockSpec((T, 2 * W), lambda h, i: (0, h))
    return pl.pallas_call(
        functools.partial(_fox_prompt_kernel, TQ, HD ** -0.5),
        grid=(HP, T // TQ),
        in_specs=[pl.BlockSpec((TQ, W), lambda h, i: (i, h)),
                  resident(),
                  resident(),
                  pl.BlockSpec((n_meta, 2 * W), lambda h, i: (mb, h)),
                  pl.BlockSpec((n_meta, 2 * W), lambda h, i: (mb, h)),
                  pl.BlockSpec((TQ, W), lambda h, i: (i, HP + h))],
        out_specs=pl.BlockSpec((TQ, W), lambda h, i: (i, h)),
        out_shape=jax.ShapeDtypeStruct((T, H * HD), BF16),
        scratch_shapes=[pltpu.VMEM((FOX_NH, TQ, TQ), F32), pltpu.VMEM((FOX_NH, TQ, TQ), F32)],
        compiler_params=_cparams(("parallel", "arbitrary"), VMEM_LIMIT_BIG),
        name="fox_prompt",
    )(qz, k_aug, v_aug, k_aug, v_aug, qz)


def _fox_seq_kernel(has_cache, scale, q_ref, k_ref, v_ref, z_ref, cn_ref, *rest):
    if has_cache:
        kc_ref, vc_ref, cc_ref, o_ref, m_scr, l_scr, a_scr = rest
    else:
        o_ref, m_scr, l_scr, a_scr = rest
    c = pl.program_id(1)
    TQ = q_ref.shape[0]
    HD = FOX_HEAD
    H = q_ref.shape[1] // HD
    HS = range(H)
    hs = lambda h: slice(h * HD, (h + 1) * HD)
    q = [(q_ref[:, hs(h)] * (scale * LOG2E)).astype(BF16) for h in HS]

    @pl.when(c == 0)
    def _():
        row = lax.broadcasted_iota(jnp.int32, (TQ, TQ), 0)
        col = lax.broadcasted_iota(jnp.int32, (TQ, TQ), 1)
        s = [jnp.where(col <= row, _mm_nt(q[h], k_ref[:, hs(h)]) - cn_ref[0, h], -jnp.inf) for h in HS]
        m = [jnp.max(s[h], axis=-1, keepdims=True) for h in HS]
        p = [jnp.exp2(s[h] - m[h]) for h in HS]
        for h in HS:
            m_scr[h] = m[h]
            l_scr[h] = jnp.sum(p[h], axis=-1, keepdims=True)
            a_scr[h] = _mm(p[h], v_ref[:, hs(h)])

    if has_cache:
        TP = cc_ref.shape[-1]
        s = [_mm_nt(q[h], kc_ref[pl.ds(h, TP, stride=H), :]) - cc_ref[0, h] for h in HS]
        m_old = [m_scr[h] for h in HS]
        l_old = [l_scr[h] for h in HS]
        a_old = [a_scr[h] for h in HS]
        m_new = [jnp.maximum(m_old[h], jnp.max(s[h], axis=-1, keepdims=True)) for h in HS]
        alpha = [jnp.exp2(m_old[h] - m_new[h]) for h in HS]
        p = [jnp.exp2(s[h] - m_new[h]) for h in HS]
        pv = [_mm(p[h], vc_ref[pl.ds(h, TP, stride=H), :]) for h in HS]
        for h in HS:
            m_scr[h] = m_new[h]
            l_scr[h] = alpha[h] * l_old[h] + jnp.sum(p[h], axis=-1, keepdims=True)
            a_scr[h] = alpha[h] * a_old[h] + pv[h]

    @pl.when(c == pl.num_programs(1) - 1)
    def _():
        o_ref[...] = jnp.concatenate([(a_scr[h] / l_scr[h]) * _silu(z_ref[:, hs(h)]) for h in HS],
                                     axis=1).astype(BF16)


def fox_seq(qz, kv16, c_new, row0, B, TQ, H, cache=None):
    HD = FOX_HEAD
    FD = H * HD
    assert row0 % TQ == 0
    rb0 = row0 // TQ
    tok = lambda sec: pl.BlockSpec((TQ, FD), lambda b, c: (rb0 + b, sec))
    in_specs = [tok(0), tok(0), tok(1), tok(1), pl.BlockSpec((1, H, 1, TQ), lambda b, c: (b, 0, 0, 0))]
    args = [qz, kv16, kv16, qz, c_new]
    nck = 1
    if cache is not None:
        kc, vc, cc, b0 = cache
        P = cc.shape[-1]
        TP = min(FOX_TP, P)
        assert P % TP == 0
        nck = P // TP
        blk = pl.BlockSpec((TP * H, HD), lambda b, c: ((b0 + b) * nck + c, 0))
        in_specs += [blk, blk, pl.BlockSpec((1, H, 1, TP), lambda b, c: (b, 0, 0, c))]
        args += [kc, vc, cc]
    return pl.pallas_call(
        functools.partial(_fox_seq_kernel, cache is not None, HD ** -0.5),
        grid=(B, nck),
        in_specs=in_specs,
        out_specs=pl.BlockSpec((TQ, FD), lambda b, c: (b, 0)),
        out_shape=jax.ShapeDtypeStruct((B * TQ, FD), BF16),
        scratch_shapes=[pltpu.VMEM((H, TQ, 1), F32), pltpu.VMEM((H, TQ, 1), F32), pltpu.VMEM((H, TQ, HD), F32)],
        compiler_params=_cparams(("parallel", "arbitrary")),
        name="fox_seq",
    )(*args)


WKV_NU = 4


def _wkv_intra_kernel(L, r_ref, k_ref, v_ref, wp_ref, ap_ref, kk_ref, ka_ref, rk_ref,
                      x_ref, u0_ref, o0_ref, arb_ref, vk_ref, bh_ref, gl_ref, bonus_ref):
    HD, NH = RWKV_HEAD, STACK
    R = NH * L
    W = NH * HD

    rl = lax.broadcasted_iota(jnp.int32, (L, L), 0)
    cl = lax.broadcasted_iota(jnp.int32, (L, L), 1)
    tril = jnp.where(cl <= rl, 1.0, 0.0).astype(BF16)
    row = lax.broadcasted_iota(jnp.int32, (R, R), 0)
    col = lax.broadcasted_iota(jnp.int32, (R, R), 1)
    diff = row ^ col
    causal = (diff < L) & (col <= row)
    strict = (diff < L) & (col < row)
    eye = jnp.where(row == col, 1.0, 0.0).astype(F32)

    heads = [slice(j * HD, (j + 1) * HD) for j in range(WKV_NU * NH)]
    r = r_ref[...]
    v = v_ref[...]
    k_raw = k_ref[...]
    wp = -wp_ref[...]
    w = -(jnp.maximum(wp, 0.0) + jnp.log(1.0 + jnp.exp(-jnp.abs(wp)))) - 0.5
    lw = -jnp.exp(w)
    a_sig = jax.nn.sigmoid(ap_ref[...])
    kk = k_raw * kk_ref[...]
    kk = jnp.concatenate(
        [kk[:, h] * lax.rsqrt(jnp.sum(kk[:, h] * kk[:, h], axis=-1, keepdims=True) + L2_EPS) for h in heads], axis=1)
    k = k_raw * (1.0 + (a_sig - 1.0) * ka_ref[...])
    a = -kk
    b = kk * a_sig
    rk = r * k * rk_ref[...]
    bonus_ref[...] = jnp.concatenate([jnp.sum(rk[:, h], axis=-1, keepdims=True) * v[:, h] for h in heads], axis=1)

    cum = _cumsum_rows(tril, lw)
    clast = cum[L - 1:L, :]
    e_neg = jnp.exp(-cum)
    e_tail = jnp.exp(clast - cum)
    rt = r * jnp.exp(cum)
    at = a * jnp.exp(cum - lw)
    kt = k * e_neg
    bt = b * e_neg
    khat = k * e_tail
    bhat = b * e_tail
    gl_ref[0] = jnp.exp(clast)

    U = range(WKV_NU)
    sl = lambda x, u: x[:, u * W:(u + 1) * W]
    st = lambda x, u: _stack_heads(sl(x, u), NH, HD)
    v_st = [st(v, u) for u in U]
    aa = [_mm_nt(jnp.concatenate([st(at, u), st(rt, u)], axis=0),
                 jnp.concatenate([st(bt, u), st(kt, u)], axis=0)) for u in U]
    a_ab = [jnp.where(strict, aa[u][:R, :R], 0.0) for u in U]
    a_ak = [jnp.where(strict, aa[u][:R, R:], 0.0) for u in U]
    a_rb = [jnp.where(causal, aa[u][R:, :R], 0.0) for u in U]
    a_rk = [jnp.where(causal, aa[u][R:, R:], 0.0) for u in U]
    av = [_mm(jnp.concatenate([a_ak[u], a_rk[u]], axis=0), v_st[u]) for u in U]
    vk = [_mm_tn(v_st[u], _expand_heads(sl(khat, u), NH, L, HD)) for u in U]
    tinv = _tri_inv([-x for x in a_ab], diff, eye, L)
    w_exp = [_mm(tinv[u], _expand_heads(sl(at, u), NH, L, HD)) for u in U]
    u0 = [_mm(tinv[u], av[u][:R]) for u in U]
    for u in U:
        x_ref[0, u] = jnp.concatenate([w_exp[u], _expand_heads(sl(rt, u), NH, L, HD)], axis=0).astype(BF16)
        u0_ref[0, u] = u0[u]
        o0_ref[0, u] = av[u][R:]
        arb_ref[0, u] = a_rb[u].astype(BF16)
        vk_ref[0, u] = vk[u]
        bh_ref[0, u] = _expand_heads(sl(bhat, u), NH, L, HD).astype(BF16)


def _wkv_inter_kernel(L, HG, x_ref, u0_ref, o0_ref, arb_ref, vk_ref, bh_ref, gl_ref, s0_ref,
                      bonus_ref, gate_ref, gnw_ref, gnb_ref, o_ref, so_ref, s_scr):
    c = pl.program_id(1)
    HD, NH = RWKV_HEAD, STACK
    R = NH * L
    W = NH * HD

    @pl.when(c == 0)
    def _():
        for g in range(HG):
            s_scr[g] = jnp.concatenate([s0_ref[0, NH * g + j] for j in range(NH)], axis=1)

    s_old = [s_scr[g] for g in range(HG)]
    d1 = [_mm_nt(x_ref[0, g], s_old[g]) for g in range(HG)]
    u_st = [u0_ref[0, g] + d1[g][:R] for g in range(HG)]
    o_st = [d1[g][R:] + o0_ref[0, g] + _dot(arb_ref[0, g], u_st[g].astype(BF16)) for g in range(HG)]
    for g in range(HG):
        s_scr[g] = s_old[g] * gl_ref[0, :, g * W:(g + 1) * W] + vk_ref[0, g] + _mm_tn(u_st[g], bh_ref[0, g])
    outs = []
    for g in range(HG):
        mean = jnp.mean(o_st[g], axis=-1, keepdims=True)
        d = o_st[g] - mean
        var = jnp.mean(d * d, axis=-1, keepdims=True)
        on = d * lax.rsqrt(var + RWKV_GN_EPS)
        outs += [on[j * L:(j + 1) * L] for j in range(NH)]
    o = jnp.concatenate(outs, axis=1) * gnw_ref[...] + gnb_ref[...] + bonus_ref[...]
    o_ref[...] = (o * _silu(gate_ref[...])).astype(BF16)

    @pl.when(c == pl.num_programs(1) - 1)
    def _():
        for g in range(HG):
            for j in range(NH):
                so_ref[0, NH * g + j] = s_scr[g][:, j * HD:(j + 1) * HD]


def wkv_mix(big, w_pre, a_pre, k_k, k_a, r_k, gn_w, gn_b, s0, row0, B, NC, L):
    HN = s0.shape[1]
    HD, NH, NU = RWKV_HEAD, STACK, WKV_NU
    HG = HN // NH
    W = NH * HD
    R = NH * L
    D = HN * HD
    Tseg = B * NC * L
    NCH = B * NC
    rb0 = row0 // L
    nsec = D // (NU * W)
    assert row0 % L == 0 and HG % NU == 0
    tok = lambda sec: pl.BlockSpec((L, NU * W), lambda bb, h, c: (rb0 + bb * NC + c, sec * nsec + h))
    par = pl.BlockSpec((1, NU * W), lambda bb, h, c: (0, h))
    unit = lambda w: pl.BlockSpec((1, NU, w[0], w[1]), lambda bb, h, c: (bb * NC + c, h, 0, 0))
    x, u0, o0, arb, vk, bh, gl, bonus = pl.pallas_call(
        functools.partial(_wkv_intra_kernel, L),
        grid=(B, HG // NU, NC),
        in_specs=[tok(0), tok(1), tok(2), tok(0), tok(0), par, par, par],
        out_specs=[unit((2 * R, W)), unit((R, HD)), unit((R, HD)), unit((R, R)), unit((HD, W)), unit((R, W)),
                   pl.BlockSpec((1, 1, NU * W), lambda bb, h, c: (bb * NC + c, 0, h)),
                   pl.BlockSpec((L, NU * W), lambda bb, h, c: (bb * NC + c, h))],
        out_shape=[jax.ShapeDtypeStruct((NCH, HG, 2 * R, W), BF16),
                   jax.ShapeDtypeStruct((NCH, HG, R, HD), F32),
                   jax.ShapeDtypeStruct((NCH, HG, R, HD), F32),
                   jax.ShapeDtypeStruct((NCH, HG, R, R), BF16),
                   jax.ShapeDtypeStruct((NCH, HG, HD, W), F32),
                   jax.ShapeDtypeStruct((NCH, HG, R, W), BF16),
                   jax.ShapeDtypeStruct((NCH, 1, D), F32),
                   jax.ShapeDtypeStruct((Tseg, D), F32)],
        compiler_params=_cparams(("parallel", "parallel", "parallel")),
        name="wkv_intra",
    )(big, big, big, w_pre, a_pre, k_k.reshape(1, D), k_a.reshape(1, D), r_k.reshape(1, D))

    allu = lambda w: pl.BlockSpec((1, HG, w[0], w[1]), lambda bb, c: (bb * NC + c, 0, 0, 0))
    o, s_new = pl.pallas_call(
        functools.partial(_wkv_inter_kernel, L, HG),
        grid=(B, NC),
        in_specs=[allu((2 * R, W)), allu((R, HD)), allu((R, HD)), allu((R, R)), allu((HD, W)), allu((R, W)),
                  pl.BlockSpec((1, 1, D), lambda bb, c: (bb * NC + c, 0, 0)),
                  pl.BlockSpec((1, HN, HD, HD), lambda bb, c: (bb, 0, 0, 0)),
                  pl.BlockSpec((L, D), lambda bb, c: (bb * NC + c, 0)),
                  pl.BlockSpec((L, D), lambda bb, c: (rb0 + bb * NC + c, 3)),
                  pl.BlockSpec((1, D), lambda bb, c: (0, 0)),
                  pl.BlockSpec((1, D), lambda bb, c: (0, 0))],
        out_specs=[pl.BlockSpec((L, D), lambda bb, c: (bb * NC + c, 0)),
                   pl.BlockSpec((1, HN, HD, HD), lambda bb, c: (bb, 0, 0, 0))],
        out_shape=[jax.ShapeDtypeStruct((Tseg, D), BF16),
                   jax.ShapeDtypeStruct((B, HN, HD, HD), F32)],
        scratch_shapes=[pltpu.VMEM((HG, HD, W), F32)],
        compiler_params=_cparams(("parallel", "arbitrary")),
        name="wkv_inter",
    )(x, u0, o0, arb, vk, bh, gl, s0, bonus, big, gn_w.reshape(1, D), gn_b.reshape(1, D))
    return o, s_new


def _pad_cols(w, n):
    return jnp.pad(w, ((0, 0), (0, n - w.shape[1])))


def _gdn_layer(X, seg, g_pre, g_post, w_in, conv_w, a_log, dt_bias, o_norm, w_out, st_S, st_conv):
    (T, BS, TS, NM) = seg
    HV = a_log.shape[0]
    CD = conv_w.shape[1]
    VD = HV * GDN_HEAD
    D = w_in.shape[0]
    NH = STACK
    HG = HV // NH
    w_main = w_in[:, :CD + VD].astype(BF16)
    wb = w_in[:, CD + VD:CD + VD + HV].reshape(D, HG, NH)
    wa = w_in[:, CD + VD + HV:].reshape(D, HG, NH)
    w_gate = jnp.pad(jnp.concatenate([wb, wa], axis=2), ((0, 0), (0, 0), (0, LANES - 2 * NH)))
    decay_lanes = lambda p: jnp.pad(p.reshape(HG, NH), ((0, 0), (NH, LANES - 2 * NH))).reshape(1, HG * LANES)
    a_par, dt_par = decay_lanes(a_log), decay_lanes(dt_bias)
    proj = norm_proj(X, g_pre, w_main)
    gates = norm_proj(X, g_pre, w_gate.reshape(D, HG * LANES).astype(BF16))
    gp = (gates, a_par, dt_par)

    r_s, r_m = T, T + BS * TS
    tail = GDN_CONV - 1
    c_m = proj[r_m + NM - tail:r_m + NM, :CD][None]
    c_p = proj[T - tail:T, :CD][None]
    c_s = proj[r_s:r_m, :CD].reshape(BS, TS, CD)[:, TS - tail:]
    z_conv = jnp.zeros((1, tail, CD), F32)
    z_S = jnp.zeros((1, HV, GDN_HEAD, GDN_HEAD), F32)
    o_m, S_m = gdn_mix(proj, *gp, z_conv, z_S, conv_w, o_norm, r_m, 1, 1, NM)
    LP = 64
    o_p, S_p = gdn_mix(proj, *gp, c_m, S_m, conv_w, o_norm, 0, 1, T // LP, LP)
    o_s, S_s = gdn_mix(proj, *gp, st_conv, st_S, conv_w, o_norm, r_s, BS, 1, TS)
    o = jnp.concatenate([o_p, o_s, o_m], axis=0)
    X = out_proj(o, w_out.astype(BF16), X, g_post)
    return X, (S_p, c_p, S_s, c_s)


def _fox_layer(X, seg, g_pre, g_post, w_in, b_f, w_out, caches, j):
    (T, BS, TS, NM) = seg
    cache_k, cache_v, cache_logf = caches
    H = b_f.shape[0]
    FD = H * FOX_HEAD
    qz = norm_proj(X, g_pre, jnp.concatenate([w_in[:, :FD], w_in[:, 3 * FD:4 * FD]], axis=1).astype(BF16))
    kv, kv16 = norm_proj(X, g_pre, w_in[:, FD:3 * FD].astype(BF16), also_bf16=True)
    gates = norm_proj(X, g_pre, _pad_cols(w_in[:, 4 * FD:], LANES).astype(BF16))
    r_s, r_m = T, T + BS * TS
    P = cache_k.shape[2]

    bias = _pad_cols(b_f.reshape(1, H), LANES)
    zero = jnp.zeros((1, LANES), F32)
    lf_m, cs_m = cumsum_rows(gates, r_m, NM, zero, bias)
    lf_p, cs_p = cumsum_rows(gates, 0, T, cs_m[NM - 1:NM], bias)
    lf_s, _ = cumsum_rows(gates, r_s, BS * TS, zero, bias)
    c_meta = (cs_m[:, :H] * LOG2E).T.reshape(H, 1, NM)
    c_rows = jnp.concatenate([cs_p[:, :H], jnp.zeros((BS * TS, H), F32), cs_m[:, :H]], axis=0) * LOG2E
    k_aug, v_aug = fox_augment(kv16, c_rows, H)
    _, cs_c = cumsum_rows(cache_logf[j].transpose(1, 0, 2).reshape(P, BS * H), 0, P, jnp.zeros((1, BS * H), F32))
    lf_new = lf_s[:, :H].reshape(BS, TS, H).transpose(1, 0, 2).reshape(TS, BS * H)
    _, cs_n = cumsum_rows(lf_new, 0, TS, cs_c[P - 1:P])
    pos_last = lambda c, n: (c * LOG2E).reshape(n, BS, H).transpose(1, 2, 0).reshape(BS, H, 1, n)
    c_cache, c_new = pos_last(cs_c, P), pos_last(cs_n, TS)
    logf = jnp.concatenate([lf_p[:, :H], lf_s[:, :H], lf_m[:, :H]], axis=0)

    o_p = fox_prompt(qz, k_aug, v_aug, T, r_m, NM, H)
    o_m = fox_seq(qz, kv16, c_meta.reshape(1, H, 1, NM), r_m, 1, NM, H)
    o_s = fox_seq(qz, kv16, c_new, r_s, BS, TS, H,
                  cache=(cache_k.reshape(-1, FOX_HEAD), cache_v.reshape(-1, FOX_HEAD), c_cache, j * BS))
    o = jnp.concatenate([o_p, o_s, o_m], axis=0)
    X = out_proj(o, w_out.astype(BF16), X, g_post)

    def seq(a, n):
        return jnp.concatenate([a[r_m:], a[:T]], axis=0).reshape(1, NM + T, H, n)

    k_all = kv[:, :FD]
    v_all = kv[:, FD:]
    outs = (seq(k_all, FOX_HEAD), seq(v_all, FOX_HEAD),
            jnp.concatenate([logf[r_m:], logf[:T]], axis=0).reshape(1, NM + T, H),
            k_all[r_s:r_m].reshape(BS, TS, H, FOX_HEAD), v_all[r_s:r_m].reshape(BS, TS, H, FOX_HEAD),
            logf[r_s:r_m].reshape(BS, TS, H))
    return X, outs


def _rwkv_layer(X, seg, g_pre, g_post, mu, w_r, w_k, w_v, w_g, w0, w1, w2, a0, a1, a2, k_k, k_a, r_k,
                gn_w, gn_b, w_out, st_S, st_shift):
    (T, BS, TS, NM) = seg
    D = X.shape[1]
    HN = r_k.shape[0]
    HD = RWKV_HEAD
    r_s, r_m = T, T + BS * TS
    h, prev = norm_shift(X, g_pre)
    h_s = h[r_s:r_m].reshape(BS, TS, D)
    starts = jnp.concatenate([jnp.zeros((1,), jnp.int32), r_s + TS * jnp.arange(BS, dtype=jnp.int32),
                              jnp.full((1,), r_m, jnp.int32)])
    before = jnp.concatenate([h[r_m + NM - 1:r_m + NM], st_shift, jnp.zeros((1, D), F32)], axis=0)
    prev = prev.at[starts].set(before)

    big = mix_proj(h, prev, mu[jnp.array([0, 2, 3, 5])], jnp.stack([w_r, w_k, w_v, w_g]).astype(BF16))
    lora_w = jnp.stack([_pad_cols(w1, LANES), _pad_cols(a1, LANES)]).astype(BF16)
    lo = mix_proj(h, prev, mu[jnp.array([1, 4])], lora_w)
    pad_rows = lambda w: jnp.pad(w, ((0, LANES - w.shape[0]), (0, 0))).astype(BF16)
    w_pre = lora_out(lo, 0, pad_rows(w2), w0, True)
    a_pre = lora_out(lo, 1, pad_rows(a2), a0, False)

    zS = jnp.zeros((1, HN, HD, HD), F32)
    par = (k_k, k_a, r_k, gn_w, gn_b)
    o_m, S_m = wkv_mix(big, w_pre, a_pre, *par, zS, r_m, 1, 1, NM)
    LP = 64
    o_p, S_p = wkv_mix(big, w_pre, a_pre, *par, S_m, 0, 1, T // LP, LP)
    o_s, S_s = wkv_mix(big, w_pre, a_pre, *par, st_S, r_s, BS, 1, TS)
    o = jnp.concatenate([o_p, o_s, o_m], axis=0)
    X = out_proj(o, w_out.astype(BF16), X, g_post)
    return X, (S_p, h[T - 1:T], S_s, h_s[:, -1])


def kernel(x_prompt, x_sample, state_gdn_S, state_gdn_conv, cache_fox_k, cache_fox_v, cache_fox_logf, state_rwkv_S, state_rwkv_shift, meta, norm_pre, norm_post, gdn_w_in, gdn_conv_w, gdn_a_log, gdn_dt_bias, gdn_o_norm, gdn_w_out, fox_w_in, fox_b_f, fox_w_out, rwkv_mu, rwkv_w_r, rwkv_w_k, rwkv_w_v, rwkv_w_g, rwkv_w0, rwkv_w1, rwkv_w2, rwkv_a0, rwkv_a1, rwkv_a2, rwkv_k_k, rwkv_k_a, rwkv_r_k, rwkv_gn_w, rwkv_gn_b, rwkv_w_out):
    _, T, D = x_prompt.shape
    BS, TS, _ = x_sample.shape
    NM = meta.shape[0]
    depth = norm_pre.shape[0]
    assert x_prompt.shape[0] == 1
    seg = (T, BS, TS, NM)
    X = jnp.concatenate([x_prompt[0], x_sample.reshape(BS * TS, D), meta.astype(x_prompt.dtype)], axis=0)

    gdn_out, fox_out, rwkv_out = [], [], []
    for i in range(depth):
        kind, j = i % 3, i // 3
        if kind == 0:
            X, st = _gdn_layer(X, seg, norm_pre[i], norm_post[i], gdn_w_in[j], gdn_conv_w[j], gdn_a_log[j],
                               gdn_dt_bias[j], gdn_o_norm[j], gdn_w_out[j], state_gdn_S[j], state_gdn_conv[j])
            gdn_out.append(st)
        elif kind == 1:
            X, st = _fox_layer(X, seg, norm_pre[i], norm_post[i], fox_w_in[j], fox_b_f[j], fox_w_out[j],
                               (cache_fox_k, cache_fox_v, cache_fox_logf), j)
            fox_out.append(st)
        else:
            X, st = _rwkv_layer(X, seg, norm_pre[i], norm_post[i], rwkv_mu[j], rwkv_w_r[j], rwkv_w_k[j],
                                rwkv_w_v[j], rwkv_w_g[j], rwkv_w0[j], rwkv_w1[j], rwkv_w2[j], rwkv_a0[j],
                                rwkv_a1[j], rwkv_a2[j], rwkv_k_k[j], rwkv_k_a[j], rwkv_r_k[j], rwkv_gn_w[j],
                                rwkv_gn_b[j], rwkv_w_out[j], state_rwkv_S[j], state_rwkv_shift[j])
            rwkv_out.append(st)

    stack = lambda items, n: jnp.stack([it[n] for it in items])
    y_prompt = X[:T][None]
    y_sample = X[T:T + BS * TS].reshape(BS, TS, D)
    return (y_prompt, y_sample,
            stack(gdn_out, 0), stack(gdn_out, 1),
            stack(fox_out, 0), stack(fox_out, 1), stack(fox_out, 2),
            stack(rwkv_out, 0), stack(rwkv_out, 1),
            stack(gdn_out, 2), stack(gdn_out, 3),
            stack(fox_out, 3), stack(fox_out, 4), stack(fox_out, 5),
            stack(rwkv_out, 2), stack(rwkv_out, 3))
```

```python
import functools

import jax
import jax.numpy as jnp
from jax import lax
from jax.experimental import pallas as pl
from jax.experimental.pallas import tpu as pltpu

F32 = jnp.float32
BF16 = jnp.bfloat16

NORM_EPS = 1e-6
L2_EPS = 1e-6
RWKV_GN_EPS = 64e-5
GDN_HEAD = 128
GDN_CONV = 4
FOX_HEAD = 128
RWKV_HEAD = 64
LANES = 128
SUBLANES = 8
CONV_PAD = 8
INV_BASE = 16
STACK = 4
VMEM_LIMIT = 48 * 1024 * 1024
VMEM_LIMIT_BIG = 56 * 1024 * 1024


def _cparams(sem, limit=VMEM_LIMIT):
    return pltpu.CompilerParams(dimension_semantics=sem, vmem_limit_bytes=limit)


def _log2(n):
    assert n & (n - 1) == 0
    return n.bit_length() - 1


def _dot(a, b):
    return jnp.dot(a, b, preferred_element_type=F32)


def _mm(a, b):
    return _dot(a.astype(BF16), b.astype(BF16))


def _mm_nt(a, b):
    return lax.dot_general(a.astype(BF16), b.astype(BF16), (((1,), (1,)), ((), ())),
                           preferred_element_type=F32)


def _mm_tn(a, b):
    return lax.dot_general(a.astype(BF16), b.astype(BF16), (((0,), (0,)), ((), ())),
                           preferred_element_type=F32)


def _split3(a):
    hi = a.astype(BF16)
    r = a - hi.astype(F32)
    mid = r.astype(BF16)
    lo = (r - mid.astype(F32)).astype(BF16)
    return hi, mid, lo


def _cumsum_rows(tril, g):
    hi, mid, lo = _split3(g)
    return _dot(tril, hi) + (_dot(tril, mid) + _dot(tril, lo))


def _cumsum_cols_t(g, triu):
    hi, mid, lo = _split3(g)
    return _mm_tn(hi, triu) + (_mm_tn(mid, triu) + _mm_tn(lo, triu))


def _tri_inv(a_list, diff, eye, L):
    base = diff < INV_BASE
    n1 = [jnp.where(base, -a, 0.0) for a in a_list]
    p = [eye + n for n in n1]
    npow = n1
    steps = 1
    while 2 * steps < INV_BASE:
        npow = [_mm(x, x) for x in npow]
        p = [pp + _mm(pp, x) for pp, x in zip(p, npow)]
        steps *= 2
    blk = INV_BASE
    while blk < L:
        off_diag = (diff < 2 * blk) & (diff >= blk)
        e = [jnp.where(off_diag, a, 0.0) for a in a_list]
        pe = [_mm(pp, ee) for pp, ee in zip(p, e)]
        p = [pp - _mm(x, pp) for pp, x in zip(p, pe)]
        blk *= 2
    return p


def _stack_heads(x, n, w):
    return jnp.concatenate([x[:, j * w:(j + 1) * w] for j in range(n)], axis=0)


def _expand_heads(x, n, L, w):
    t = jnp.concatenate([x] * n, axis=0)
    rh = lax.shift_right_logical(lax.broadcasted_iota(jnp.int32, t.shape, 0), _log2(L))
    ch = lax.shift_right_logical(lax.broadcasted_iota(jnp.int32, t.shape, 1), _log2(w))
    return jnp.where(rh == ch, t, 0.0)


def _silu(x):
    return x * jax.nn.sigmoid(x)


NORM_PROJ_ROWS = 1600
ROW_TILE_CAP = 544
OUT_PROJ_TILE_ELEMS = 1024 * 1024


def _row_tile(T, cap, exact=False):
    best = 0
    for t in range(16, cap + 1, 16):
        if T % t == 0:
            best = t
    assert best > 0
    return best if exact or best >= cap // 4 else (cap // 16) * 16


def _norm_proj_kernel(x_ref, g_ref, w_ref, o_ref, *rest):
    h_scr = rest[-1]

    @pl.when(pl.program_id(1) == 0)
    def _():
        x = x_ref[...]
        ms = jnp.mean(x * x, axis=-1, keepdims=True)
        h_scr[...] = (x * lax.rsqrt(ms + NORM_EPS) * g_ref[...]).astype(BF16)

    y = _dot(h_scr[...], w_ref[...])
    o_ref[...] = y
    if len(rest) == 2:
        rest[0][...] = y.astype(BF16)


def norm_proj(x, g, w, tn=512, also_bf16=False):
    T, D = x.shape
    N = w.shape[1]
    tn = min(tn, N)
    tm = _row_tile(T, NORM_PROJ_ROWS)
    assert N % tn == 0
    out_spec = pl.BlockSpec((tm, tn), lambda i, j: (i, j))
    out_specs, out_shape = out_spec, jax.ShapeDtypeStruct((T, N), F32)
    if also_bf16:
        out_specs, out_shape = [out_spec, out_spec], [out_shape, jax.ShapeDtypeStruct((T, N), BF16)]
    return pl.pallas_call(
        _norm_proj_kernel,
        grid=(pl.cdiv(T, tm), N // tn),
        in_specs=[pl.BlockSpec((tm, D), lambda i, j: (i, 0)),
                  pl.BlockSpec((1, D), lambda i, j: (0, 0)),
                  pl.BlockSpec((D, tn), lambda i, j: (0, j))],
        out_specs=out_specs,
        out_shape=out_shape,
        scratch_shapes=[pltpu.VMEM((tm, D), BF16)],
        compiler_params=_cparams(("parallel", "arbitrary"), VMEM_LIMIT_BIG),
        name="norm_proj",
    )(x, g.reshape(1, D), w)


def _norm_shift_kernel(x_ref, g_ref, h_ref, p_ref, last_scr):
    tm = x_ref.shape[0]

    @pl.when(pl.program_id(0) == 0)
    def _():
        last_scr[...] = jnp.zeros_like(last_scr)

    x = x_ref[...]
    ms = jnp.mean(x * x, axis=-1, keepdims=True)
    h = x * lax.rsqrt(ms + NORM_EPS) * g_ref[...]
    h_ref[...] = h
    p_ref[0:1, :] = last_scr[...]
    p_ref[1:tm, :] = h[0:tm - 1, :]
    last_scr[...] = h[tm - 1:tm, :]


def norm_shift(x, g):
    T, D = x.shape
    tm = _row_tile(T, ROW_TILE_CAP, exact=True)
    spec = pl.BlockSpec((tm, D), lambda i: (i, 0))
    return pl.pallas_call(
        _norm_shift_kernel,
        grid=(T // tm,),
        in_specs=[spec, pl.BlockSpec((1, D), lambda i: (0, 0))],
        out_specs=[spec, spec],
        out_shape=[jax.ShapeDtypeStruct((T, D), F32), jax.ShapeDtypeStruct((T, D), F32)],
        scratch_shapes=[pltpu.VMEM((1, D), F32)],
        compiler_params=_cparams(("arbitrary",)),
        name="norm_shift",
    )(x, g.reshape(1, D))


CUMSUM_ROWS = 512


def _cumsum_kernel(log_sigmoid, x_ref, b_ref, init_ref, f_ref, c_ref, carry):
    tp = x_ref.shape[0]

    @pl.when(pl.program_id(0) == 0)
    def _():
        carry[...] = init_ref[...]

    x = x_ref[...]
    if log_sigmoid:
        x = -(x + b_ref[...])
        x = -(jnp.maximum(x, 0.0) + jnp.log(1.0 + jnp.exp(-jnp.abs(x))))
    f_ref[...] = x
    r = lax.broadcasted_iota(jnp.int32, (tp, tp), 0)
    c = lax.broadcasted_iota(jnp.int32, (tp, tp), 1)
    y = _cumsum_rows(jnp.where(c <= r, 1.0, 0.0).astype(BF16), x) + carry[...]
    carry[...] = y[tp - 1:tp, :]
    c_ref[...] = y


def cumsum_rows(x, row0, n, init, bias=None):
    C = x.shape[1]
    tp = min(CUMSUM_ROWS, n)
    assert n % tp == 0 and row0 % tp == 0
    rb0 = row0 // tp
    b = jnp.zeros((1, C), F32) if bias is None else bias
    return pl.pallas_call(
        functools.partial(_cumsum_kernel, bias is not None),
        grid=(n // tp,),
        in_specs=[pl.BlockSpec((tp, C), lambda i: (rb0 + i, 0)),
                  pl.BlockSpec((1, C), lambda i: (0, 0)),
                  pl.BlockSpec((1, C), lambda i: (0, 0))],
        out_specs=[pl.BlockSpec((tp, C), lambda i: (i, 0)), pl.BlockSpec((tp, C), lambda i: (i, 0))],
        out_shape=[jax.ShapeDtypeStruct((n, C), F32), jax.ShapeDtypeStruct((n, C), F32)],
        scratch_shapes=[pltpu.VMEM((1, C), F32)],
        compiler_params=_cparams(("arbitrary",)),
        name="cumsum_rows",
    )(x, b, init)


def _mix_proj_kernel(h_ref, p_ref, mu_ref, w_ref, o_ref, l_scr):
    @pl.when(pl.program_id(2) == 0)
    def _():
        h = h_ref[...]
        l_scr[...] = (h + (p_ref[...] - h) * mu_ref[0]).astype(BF16)

    o_ref[...] = _dot(l_scr[...], w_ref[0])


def mix_proj(h, prev, mu, w, tn=1024):
    T, D = h.shape
    tm = _row_tile(T, ROW_TILE_CAP)
    G, _, N = w.shape
    tn = min(tn, N)
    nj = N // tn
    return pl.pallas_call(
        _mix_proj_kernel,
        grid=(pl.cdiv(T, tm), G, nj),
        in_specs=[pl.BlockSpec((tm, D), lambda i, g, j: (i, 0)),
                  pl.BlockSpec((tm, D), lambda i, g, j: (i, 0)),
                  pl.BlockSpec((1, 1, D), lambda i, g, j: (g, 0, 0)),
                  pl.BlockSpec((1, D, tn), lambda i, g, j: (g, 0, j))],
        out_specs=pl.BlockSpec((tm, tn), lambda i, g, j: (i, g * nj + j)),
        out_shape=jax.ShapeDtypeStruct((T, G * N), F32),
        scratch_shapes=[pltpu.VMEM((tm, D), BF16)],
        compiler_params=_cparams(("parallel", "arbitrary", "arbitrary")),
        name="mix_proj",
    )(h, prev, mu.reshape(G, 1, D), w)


def _lora_kernel(use_tanh, x_ref, w_ref, b_ref, o_ref):
    x = x_ref[...]
    if use_tanh:
        x = jnp.tanh(x)
    o_ref[...] = b_ref[...] + _dot(x.astype(BF16), w_ref[...])


def lora_out(x, col_block, w, b, use_tanh):
    T = x.shape[0]
    tm = _row_tile(T, ROW_TILE_CAP)
    K, N = w.shape
    return pl.pallas_call(
        functools.partial(_lora_kernel, use_tanh),
        grid=(pl.cdiv(T, tm),),
        in_specs=[pl.BlockSpec((tm, K), lambda i: (i, col_block)),
                  pl.BlockSpec((K, N), lambda i: (0, 0)),
                  pl.BlockSpec((1, N), lambda i: (0, 0))],
        out_specs=pl.BlockSpec((tm, N), lambda i: (i, 0)),
        out_shape=jax.ShapeDtypeStruct((T, N), F32),
        compiler_params=_cparams(("parallel",)),
        name="lora_out",
    )(x, w, b.reshape(1, N))


def _out_proj_kernel(tn, nt, D, a_ref, w_ref, x_ref, g_ref, o_ref, y_scr, ss_scr):
    j = pl.program_id(1)

    @pl.when(j < nt)
    def _():
        y = _dot(a_ref[...], w_ref[...])
        y_scr[:, pl.ds(pl.multiple_of(j * tn, tn), tn)] = y
        ss = jnp.sum(y * y, axis=-1, keepdims=True)
        ss_scr[...] = jnp.where(j == 0, ss, ss_scr[...] + ss)

    @pl.when(j >= nt)
    def _():
        y = y_scr[:, pl.ds(pl.multiple_of((j - nt) * tn, tn), tn)]
        o_ref[...] = x_ref[...] + y * lax.rsqrt(ss_scr[...] * (1.0 / D) + NORM_EPS) * g_ref[...]


def out_proj(a, w, x, g):
    T, K = a.shape
    D = w.shape[1]
    tm = _row_tile(T, NORM_PROJ_ROWS)
    tn = OUT_PROJ_TILE_ELEMS // K
    nt = D // tn
    return pl.pallas_call(
        functools.partial(_out_proj_kernel, tn, nt, D),
        grid=(pl.cdiv(T, tm), 2 * nt),
        in_specs=[pl.BlockSpec((tm, K), lambda i, j: (i, 0)),
                  pl.BlockSpec((K, tn), lambda i, j: (0, jnp.minimum(j, nt - 1))),
                  pl.BlockSpec((tm, tn), lambda i, j: (i, jnp.maximum(j - nt, 0))),
                  pl.BlockSpec((1, tn), lambda i, j: (0, jnp.maximum(j - nt, 0)))],
        out_specs=pl.BlockSpec((tm, tn), lambda i, j: (i, jnp.maximum(j - nt, 0))),
        out_shape=jax.ShapeDtypeStruct((T, D), F32),
        scratch_shapes=[pltpu.VMEM((tm, D), F32), pltpu.VMEM((tm, 1), F32)],
        compiler_params=_cparams(("parallel", "arbitrary"), VMEM_LIMIT_BIG),
        name="out_proj",
    )(a, w, x, g.reshape(1, D))


GDN_NU = 4


def _gdn_intra_kernel(L, qn_ref, kn_ref, vn_ref, qp_ref, kp_ref, vp_ref, cq_ref, ck_ref, cv_ref,
                      wq_ref, wk_ref, wv_ref, gt_ref, al_ref, dt_ref,
                      uv_ref, wqe_ref, kt_ref, qkd_ref, egl_ref, qbuf, kbuf, vbuf):
    c = pl.program_id(2)
    HD, NH = GDN_HEAD, STACK
    R = NH * L
    first = c == 0

    def conv_silu(buf, u_ref, p_ref, st_ref, w_ref):
        buf[CONV_PAD - 3:CONV_PAD, :] = jnp.where(first, st_ref[0], p_ref[SUBLANES - 3:SUBLANES, :])
        buf[CONV_PAD:CONV_PAD + L, :] = u_ref[...]
        acc = buf[CONV_PAD - 3:CONV_PAD - 3 + L, :] * w_ref[0:1, :]
        acc = acc + buf[CONV_PAD - 2:CONV_PAD - 2 + L, :] * w_ref[1:2, :]
        acc = acc + buf[CONV_PAD - 1:CONV_PAD - 1 + L, :] * w_ref[2:3, :]
        acc = acc + buf[CONV_PAD:CONV_PAD + L, :] * w_ref[3:4, :]
        return _silu(acc)

    q = conv_silu(qbuf, qn_ref, qp_ref, cq_ref, wq_ref)
    k = conv_silu(kbuf, kn_ref, kp_ref, ck_ref, wk_ref)
    v = conv_silu(vbuf, vn_ref, vp_ref, cv_ref, wv_ref)

    rl = lax.broadcasted_iota(jnp.int32, (L, L), 0)
    cl = lax.broadcasted_iota(jnp.int32, (L, L), 1)
    tril = jnp.where(cl <= rl, 1.0, 0.0).astype(BF16)
    triu = jnp.where(rl <= cl, 1.0, 0.0).astype(BF16)
    row = lax.broadcasted_iota(jnp.int32, (R, R), 0)
    col = lax.broadcasted_iota(jnp.int32, (R, R), 1)
    diff = row ^ col
    causal = (diff < L) & (col <= row)
    strict = (diff < L) & (col < row)
    eye = jnp.where(row == col, 1.0, 0.0).astype(F32)

    def l2n(x):
        return x * lax.rsqrt(jnp.sum(x * x, axis=-1, keepdims=True) + L2_EPS)

    def col_stack(x, first_lane):
        return jnp.concatenate([x[:, first_lane + j:first_lane + j + 1] for j in range(NH)], axis=0)

    kst, qst, vst, gst, bst, glast_st, grow_st, glast = [], [], [], [], [], [], [], []
    for u in range(GDN_NU):
        qh = [l2n(q[:, (2 * u + i) * HD:(2 * u + i + 1) * HD]) * (HD ** -0.5) for i in range(2)]
        kh = [l2n(k[:, (2 * u + i) * HD:(2 * u + i + 1) * HD]) for i in range(2)]
        kst.append(jnp.concatenate([kh[0], kh[0], kh[1], kh[1]], axis=0))
        qst.append(jnp.concatenate([qh[0], qh[0], qh[1], qh[1]], axis=0))
        vst.append(_stack_heads(v[:, u * NH * HD:(u + 1) * NH * HD], NH, HD))
        lanes = slice(u * LANES, (u + 1) * LANES)
        gu = gt_ref[:, lanes]
        beta = jax.nn.sigmoid(gu)
        sp = gu + dt_ref[:, lanes]
        sp = jnp.maximum(sp, 0.0) + jnp.log(1.0 + jnp.exp(-jnp.abs(sp)))
        g = -jnp.exp(al_ref[:, lanes]) * sp
        gcol = _cumsum_rows(tril, g)
        grow = _cumsum_cols_t(g, triu)
        gl = gcol[L - 1:L, :]
        glast.append(gl)
        gst.append(col_stack(gcol, NH))
        bst.append(col_stack(beta, 0))
        glast_st.append(jnp.concatenate([jnp.broadcast_to(gl[:, NH + j:NH + j + 1], (L, 1)) for j in range(NH)],
                                        axis=0))
        grow_st.append(jnp.concatenate([grow[NH + j:NH + j + 1, :] for j in range(NH)], axis=1))

    U = range(GDN_NU)
    kk = [_mm_nt(kst[u], kst[u]) for u in U]
    qk = [_mm_nt(qst[u], kst[u]) for u in U]
    decay = [jnp.where(causal, jnp.exp(jnp.where(causal, gst[u] - grow_st[u], 0.0)), 0.0) for u in U]
    a = [jnp.where(strict, kk[u] * decay[u] * bst[u], 0.0) for u in U]
    tinv = _tri_inv(a, diff, eye, L)
    rhs = [jnp.concatenate([vst[u] * bst[u], kst[u] * (bst[u] * jnp.exp(gst[u]))], axis=1) for u in U]
    sol = [_mm(tinv[u], rhs[u]) for u in U]
    for u in U:
        uv_ref[0, u] = sol[u][:, :HD]
        wk = sol[u][:, HD:]
        qe = qst[u] * jnp.exp(gst[u])
        wqe_ref[0, u] = jnp.concatenate(
            [x[j * L:(j + 1) * L] for j in range(NH) for x in (wk, qe)], axis=0).astype(BF16)
        kt_ref[0, u] = (kst[u] * jnp.exp(glast_st[u] - gst[u])).astype(BF16)
        qkd_ref[0, u] = (qk[u] * decay[u]).astype(BF16)
        egl_ref[0, u] = jnp.exp(glast[u])


def _gdn_inter_kernel(L, HG, uv_ref, wqe_ref, kt_ref, qkd_ref, egl_ref, z_ref, s0_ref, on_ref,
                      o_ref, so_ref, s_scr):
    c = pl.program_id(1)
    HD, NH = GDN_HEAD, STACK
    R = NH * L

    @pl.when(c == 0)
    def _():
        for g in range(HG):
            s_scr[g] = jnp.concatenate([s0_ref[0, NH * g + j] for j in range(NH)], axis=1)

    rh = lax.shift_right_logical(lax.broadcasted_iota(jnp.int32, (R, NH * HD), 0), _log2(L))
    ch = lax.shift_right_logical(lax.broadcasted_iota(jnp.int32, (R, NH * HD), 1), _log2(HD))
    own = rh == ch

    s_old = [s_scr[g] for g in range(HG)]
    d1 = [[_dot(wqe_ref[0, g, 2 * L * j:2 * L * (j + 1), :], s_old[g][:, j * HD:(j + 1) * HD].astype(BF16))
           for j in range(NH)] for g in range(HG)]
    v_new = [uv_ref[0, g] - jnp.concatenate([d1[g][j][:L] for j in range(NH)], axis=0) for g in range(HG)]
    o_st = [jnp.concatenate([d1[g][j][L:] for j in range(NH)], axis=0) + _dot(qkd_ref[0, g], v_new[g].astype(BF16))
            for g in range(HG)]
    for g in range(HG):
        v_exp = jnp.where(own, jnp.concatenate([v_new[g]] * NH, axis=1), 0.0)
        egl = egl_ref[0, g]
        e_exp = jnp.concatenate([jnp.broadcast_to(egl[:, NH + j:NH + j + 1], (1, HD)) for j in range(NH)], axis=1)
        s_scr[g] = s_old[g] * e_exp + _mm_tn(kt_ref[0, g], v_exp)

    outs = []
    for g in range(HG):
        for j in range(NH):
            o = o_st[g][j * L:(j + 1) * L]
            zj = z_ref[:, (NH * g + j) * HD:(NH * g + j + 1) * HD]
            ms = jnp.mean(o * o, axis=-1, keepdims=True)
            outs.append((o * lax.rsqrt(ms + NORM_EPS) * on_ref[...] * _silu(zj)).astype(BF16))
    o_ref[...] = jnp.concatenate(outs, axis=1)

    @pl.when(c == pl.num_programs(1) - 1)
    def _():
        for g in range(HG):
            for j in range(NH):
                so_ref[0, NH * g + j] = s_scr[g][:, j * HD:(j + 1) * HD]


def gdn_mix(proj, gates, a_par, dt_par, conv_state, s0, conv_w, o_norm, row0, B, NC, L):
    HV = s0.shape[1]
    HD, NH, NU = GDN_HEAD, STACK, GDN_NU
    HG = HV // NH
    QW = NU * (NH // 2) * HD
    VW = NU * NH * HD
    nq = (HV // 2) * HD // QW
    nv = HV * HD // VW
    Tseg = B * NC * L
    R = NH * L
    rb0 = row0 // L
    assert row0 % L == 0 and L % SUBLANES == 0 and HG % NU == 0

    cq = conv_state[:, :, :nq * QW]
    ck = conv_state[:, :, nq * QW:2 * nq * QW]
    cv = conv_state[:, :, 2 * nq * QW:]
    wq = conv_w[:, :nq * QW]
    wk = conv_w[:, nq * QW:2 * nq * QW]
    wv = conv_w[:, 2 * nq * QW:]

    chunk = lambda b, h, c: rb0 + b * NC + c
    before = lambda b, h, c: jnp.maximum(chunk(b, h, c) * (L // SUBLANES) - 1, 0)
    intra_in = [
        pl.BlockSpec((L, QW), lambda b, h, c: (chunk(b, h, c), h)),
        pl.BlockSpec((L, QW), lambda b, h, c: (chunk(b, h, c), nq + h)),
        pl.BlockSpec((L, VW), lambda b, h, c: (chunk(b, h, c), nv + h)),
        pl.BlockSpec((SUBLANES, QW), lambda b, h, c: (before(b, h, c), h)),
        pl.BlockSpec((SUBLANES, QW), lambda b, h, c: (before(b, h, c), nq + h)),
        pl.BlockSpec((SUBLANES, VW), lambda b, h, c: (before(b, h, c), nv + h)),
        pl.BlockSpec((1, GDN_CONV - 1, QW), lambda b, h, c: (b, 0, h)),
        pl.BlockSpec((1, GDN_CONV - 1, QW), lambda b, h, c: (b, 0, h)),
        pl.BlockSpec((1, GDN_CONV - 1, VW), lambda b, h, c: (b, 0, h)),
        pl.BlockSpec((GDN_CONV, QW), lambda b, h, c: (0, h)),
        pl.BlockSpec((GDN_CONV, QW), lambda b, h, c: (0, h)),
        pl.BlockSpec((GDN_CONV, VW), lambda b, h, c: (0, h)),
        pl.BlockSpec((L, NU * LANES), lambda b, h, c: (chunk(b, h, c), h)),
        pl.BlockSpec((1, NU * LANES), lambda b, h, c: (0, h)),
        pl.BlockSpec((1, NU * LANES), lambda b, h, c: (0, h)),
    ]
    unit = lambda w: pl.BlockSpec((1, NU, w[0], w[1]), lambda b, h, c: (b * NC + c, h, 0, 0))
    NCH = B * NC
    uv, wqe, kt, qkd, egl = pl.pallas_call(
        functools.partial(_gdn_intra_kernel, L),
        grid=(B, HG // NU, NC),
        in_specs=intra_in,
        out_specs=[unit((R, HD)), unit((2 * R, HD)), unit((R, HD)), unit((R, R)), unit((1, LANES))],
        out_shape=[jax.ShapeDtypeStruct((NCH, HG, R, HD), F32),
                   jax.ShapeDtypeStruct((NCH, HG, 2 * R, HD), BF16),
                   jax.ShapeDtypeStruct((NCH, HG, R, HD), BF16),
                   jax.ShapeDtypeStruct((NCH, HG, R, R), BF16),
                   jax.ShapeDtypeStruct((NCH, HG, 1, LANES), F32)],
        scratch_shapes=[pltpu.VMEM((CONV_PAD + L, QW), F32),
                        pltpu.VMEM((CONV_PAD + L, QW), F32),
                        pltpu.VMEM((CONV_PAD + L, VW), F32)],
        compiler_params=_cparams(("parallel", "parallel", "parallel")),
        name="gdn_intra",
    )(proj, proj, proj, proj, proj, proj, cq, ck, cv, wq, wk, wv, gates, a_par, dt_par)

    allu = lambda w: pl.BlockSpec((1, HG, w[0], w[1]), lambda b, c: (b * NC + c, 0, 0, 0))
    zblk = 2 * nv * VW // (HV * HD)
    o, s_new = pl.pallas_call(
        functools.partial(_gdn_inter_kernel, L, HG),
        grid=(B, NC),
        in_specs=[allu((R, HD)), allu((2 * R, HD)), allu((R, HD)), allu((R, R)), allu((1, LANES)),
                  pl.BlockSpec((L, HV * HD), lambda b, c: (rb0 + b * NC + c, zblk)),
                  pl.BlockSpec((1, HV, HD, HD), lambda b, c: (b, 0, 0, 0)),
                  pl.BlockSpec((1, HD), lambda b, c: (0, 0))],
        out_specs=[pl.BlockSpec((L, HV * HD), lambda b, c: (b * NC + c, 0)),
                   pl.BlockSpec((1, HV, HD, HD), lambda b, c: (b, 0, 0, 0))],
        out_shape=[jax.ShapeDtypeStruct((Tseg, HV * HD), BF16),
                   jax.ShapeDtypeStruct((B, HV, HD, HD), F32)],
        scratch_shapes=[pltpu.VMEM((HG, HD, NH * HD), F32)],
        compiler_params=_cparams(("parallel", "arbitrary")),
        name="gdn_inter",
    )(uv, wqe, kt, qkd, egl, proj, s0, o_norm.reshape(1, HD))
    return o, s_new


FOX_NH = 2
FOX_TP = 512
LOG2E = 1.4426950408889634


def _fox_prompt_kernel(TQ, scale, q_ref, k_ref, v_ref, km_ref, vm_ref, z_ref, ck_ref, cm_ref, o_ref,
                       sa_scr, sb_scr):
    qi = pl.program_id(1)
    HD = FOX_HEAD
    HS = range(FOX_NH)
    hs = lambda i: slice(i * HD, (i + 1) * HD)
    q = [(q_ref[:, hs(i)] * (scale * LOG2E)).astype(BF16) for i in HS]

    s = [_mm_nt(q[i], km_ref[:, hs(i)]) - cm_ref[i] for i in HS]
    m = [jnp.max(s[i], axis=-1, keepdims=True) for i in HS]
    p = [jnp.exp2(s[i] - m[i]) for i in HS]
    l = [jnp.sum(p[i], axis=-1, keepdims=True) for i in HS]
    acc = [_mm(p[i], vm_ref[:, hs(i)]) for i in HS]

    def scores_into(scr, kb):
        off = pl.multiple_of(kb * TQ, TQ)
        for i in HS:
            scr[i] = _mm_nt(q[i], k_ref[pl.ds(off, TQ), hs(i)]) - ck_ref[i, :, pl.ds(off, TQ)]

    def fold(scr, kb, m, l, acc, diagonal):
        off = pl.multiple_of(kb * TQ, TQ)
        s = [scr[i] for i in HS]
        if diagonal:
            row = lax.broadcasted_iota(jnp.int32, (TQ, TQ), 0)
            col = lax.broadcasted_iota(jnp.int32, (TQ, TQ), 1)
            s = [jnp.where(col <= row, s[i], -jnp.inf) for i in HS]
        m_new = [jnp.maximum(m[i], jnp.max(s[i], axis=-1, keepdims=True)) for i in HS]
        alpha = [jnp.exp2(m[i] - m_new[i]) for i in HS]
        p = [jnp.exp2(s[i] - m_new[i]) for i in HS]
        l = [alpha[i] * l[i] + jnp.sum(p[i], axis=-1, keepdims=True) for i in HS]
        acc = [alpha[i] * acc[i] + _mm(p[i], v_ref[pl.ds(off, TQ), hs(i)]) for i in HS]
        return tuple(m_new), tuple(l), tuple(acc)

    scores_into(sa_scr, 0)

    def pair(j, carry):
        scores_into(sb_scr, 2 * j + 1)
        carry = fold(sa_scr, 2 * j, *carry, False)
        scores_into(sa_scr, 2 * j + 2)
        return fold(sb_scr, 2 * j + 1, *carry, False)

    carry = lax.fori_loop(0, qi // 2, pair, (tuple(m), tuple(l), tuple(acc)))

    def last_even(*carry):
        return fold(sa_scr, qi, *carry, True)

    def last_odd(*carry):
        scores_into(sb_scr, qi)
        carry = fold(sa_scr, qi - 1, *carry, False)
        return fold(sb_scr, qi, *carry, True)

    m, l, acc = lax.cond(qi % 2 == 0, last_even, last_odd, *carry)
    o_ref[...] = jnp.concatenate([(acc[i] / l[i]) * _silu(z_ref[:, hs(i)]) for i in HS], axis=1).astype(BF16)


def fox_prompt(qz, kv16, c_main, c_meta, T, meta_row0, n_meta, H, TQ=512):
    HD = FOX_HEAD
    W = FOX_NH * HD
    HP = H // FOX_NH
    TQ = min(TQ, T)
    assert T % TQ == 0 and meta_row0 % n_meta == 0 and H % FOX_NH == 0
    mb = meta_row0 // n_meta
    return pl.pallas_call(
        functools.partial(_fox_prompt_kernel, TQ, HD ** -0.5),
        grid=(HP, T // TQ),
        in_specs=[pl.BlockSpec((TQ, W), lambda h, i: (i, h)),
                  pl.BlockSpec((T, W), lambda h, i: (0, h)),
                  pl.BlockSpec((T, W), lambda h, i: (0, HP + h)),
                  pl.BlockSpec((n_meta, W), lambda h, i: (mb, h)),
                  pl.BlockSpec((n_meta, W), lambda h, i: (mb, HP + h)),
                  pl.BlockSpec((TQ, W), lambda h, i: (i, HP + h)),
                  pl.BlockSpec((FOX_NH, 1, T), lambda h, i: (h, 0, 0)),
                  pl.BlockSpec((FOX_NH, 1, n_meta), lambda h, i: (h, 0, 0))],
        out_specs=pl.BlockSpec((TQ, W), lambda h, i: (i, h)),
        out_shape=jax.ShapeDtypeStruct((T, H * HD), BF16),
        scratch_shapes=[pltpu.VMEM((FOX_NH, TQ, TQ), F32), pltpu.VMEM((FOX_NH, TQ, TQ), F32)],
        compiler_params=_cparams(("parallel", "arbitrary"), VMEM_LIMIT_BIG),
        name="fox_prompt",
    )(qz, kv16, kv16, kv16, kv16, qz, c_main, c_meta)


def _fox_seq_kernel(has_cache, scale, q_ref, k_ref, v_ref, z_ref, cn_ref, *rest):
    if has_cache:
        kc_ref, vc_ref, cc_ref, o_ref, m_scr, l_scr, a_scr = rest
    else:
        o_ref, m_scr, l_scr, a_scr = rest
    c = pl.program_id(1)
    TQ = q_ref.shape[0]
    HD = FOX_HEAD
    H = q_ref.shape[1] // HD
    HS = range(H)
    hs = lambda h: slice(h * HD, (h + 1) * HD)
    q = [(q_ref[:, hs(h)] * (scale * LOG2E)).astype(BF16) for h in HS]

    @pl.when(c == 0)
    def _():
        row = lax.broadcasted_iota(jnp.int32, (TQ, TQ), 0)
        col = lax.broadcasted_iota(jnp.int32, (TQ, TQ), 1)
        s = [jnp.where(col <= row, _mm_nt(q[h], k_ref[:, hs(h)]) - cn_ref[0, h], -jnp.inf) for h in HS]
        m = [jnp.max(s[h], axis=-1, keepdims=True) for h in HS]
        p = [jnp.exp2(s[h] - m[h]) for h in HS]
        for h in HS:
            m_scr[h] = m[h]
            l_scr[h] = jnp.sum(p[h], axis=-1, keepdims=True)
            a_scr[h] = _mm(p[h], v_ref[:, hs(h)])

    if has_cache:
        TP = cc_ref.shape[-1]
        s = [_mm_nt(q[h], kc_ref[pl.ds(h, TP, stride=H), :]) - cc_ref[0, h] for h in HS]
        m_old = [m_scr[h] for h in HS]
        l_old = [l_scr[h] for h in HS]
        a_old = [a_scr[h] for h in HS]
        m_new = [jnp.maximum(m_old[h], jnp.max(s[h], axis=-1, keepdims=True)) for h in HS]
        alpha = [jnp.exp2(m_old[h] - m_new[h]) for h in HS]
        p = [jnp.exp2(s[h] - m_new[h]) for h in HS]
        pv = [_mm(p[h], vc_ref[pl.ds(h, TP, stride=H), :]) for h in HS]
        for h in HS:
            m_scr[h] = m_new[h]
            l_scr[h] = alpha[h] * l_old[h] + jnp.sum(p[h], axis=-1, keepdims=True)
            a_scr[h] = alpha[h] * a_old[h] + pv[h]

    @pl.when(c == pl.num_programs(1) - 1)
    def _():
        o_ref[...] = jnp.concatenate([(a_scr[h] / l_scr[h]) * _silu(z_ref[:, hs(h)]) for h in HS],
                                     axis=1).astype(BF16)


def fox_seq(qz, kv16, c_new, row0, B, TQ, H, cache=None):
    HD = FOX_HEAD
    FD = H * HD
    assert row0 % TQ == 0
    rb0 = row0 // TQ
    tok = lambda sec: pl.BlockSpec((TQ, FD), lambda b, c: (rb0 + b, sec))
    in_specs = [tok(0), tok(0), tok(1), tok(1), pl.BlockSpec((1, H, 1, TQ), lambda b, c: (b, 0, 0, 0))]
    args = [qz, kv16, kv16, qz, c_new]
    nck = 1
    if cache is not None:
        kc, vc, cc, b0 = cache
        P = cc.shape[-1]
        TP = min(FOX_TP, P)
        assert P % TP == 0
        nck = P // TP
        blk = pl.BlockSpec((TP * H, HD), lambda b, c: ((b0 + b) * nck + c, 0))
        in_specs += [blk, blk, pl.BlockSpec((1, H, 1, TP), lambda b, c: (b, 0, 0, c))]
        args += [kc, vc, cc]
    return pl.pallas_call(
        functools.partial(_fox_seq_kernel, cache is not None, HD ** -0.5),
        grid=(B, nck),
        in_specs=in_specs,
        out_specs=pl.BlockSpec((TQ, FD), lambda b, c: (b, 0)),
        out_shape=jax.ShapeDtypeStruct((B * TQ, FD), BF16),
        scratch_shapes=[pltpu.VMEM((H, TQ, 1), F32), pltpu.VMEM((H, TQ, 1), F32), pltpu.VMEM((H, TQ, HD), F32)],
        compiler_params=_cparams(("parallel", "arbitrary")),
        name="fox_seq",
    )(*args)


WKV_NU = 4


def _wkv_intra_kernel(L, r_ref, k_ref, v_ref, wp_ref, ap_ref, kk_ref, ka_ref, rk_ref,
                      x_ref, u0_ref, o0_ref, arb_ref, vk_ref, bh_ref, gl_ref, bonus_ref):
    HD, NH = RWKV_HEAD, STACK
    R = NH * L
    W = NH * HD

    rl = lax.broadcasted_iota(jnp.int32, (L, L), 0)
    cl = lax.broadcasted_iota(jnp.int32, (L, L), 1)
    tril = jnp.where(cl <= rl, 1.0, 0.0).astype(BF16)
    row = lax.broadcasted_iota(jnp.int32, (R, R), 0)
    col = lax.broadcasted_iota(jnp.int32, (R, R), 1)
    diff = row ^ col
    causal = (diff < L) & (col <= row)
    strict = (diff < L) & (col < row)
    eye = jnp.where(row == col, 1.0, 0.0).astype(F32)

    heads = [slice(j * HD, (j + 1) * HD) for j in range(WKV_NU * NH)]
    r = r_ref[...]
    v = v_ref[...]
    k_raw = k_ref[...]
    wp = -wp_ref[...]
    w = -(jnp.maximum(wp, 0.0) + jnp.log(1.0 + jnp.exp(-jnp.abs(wp)))) - 0.5
    lw = -jnp.exp(w)
    a_sig = jax.nn.sigmoid(ap_ref[...])
    kk = k_raw * kk_ref[...]
    kk = jnp.concatenate(
        [kk[:, h] * lax.rsqrt(jnp.sum(kk[:, h] * kk[:, h], axis=-1, keepdims=True) + L2_EPS) for h in heads], axis=1)
    k = k_raw * (1.0 + (a_sig - 1.0) * ka_ref[...])
    a = -kk
    b = kk * a_sig
    rk = r * k * rk_ref[...]
    bonus_ref[...] = jnp.concatenate([jnp.sum(rk[:, h], axis=-1, keepdims=True) * v[:, h] for h in heads], axis=1)

    cum = _cumsum_rows(tril, lw)
    clast = cum[L - 1:L, :]
    e_neg = jnp.exp(-cum)
    e_tail = jnp.exp(clast - cum)
    rt = r * jnp.exp(cum)
    at = a * jnp.exp(cum - lw)
    kt = k * e_neg
    bt = b * e_neg
    khat = k * e_tail
    bhat = b * e_tail
    gl_ref[0] = jnp.exp(clast)

    U = range(WKV_NU)
    sl = lambda x, u: x[:, u * W:(u + 1) * W]
    st = lambda x, u: _stack_heads(sl(x, u), NH, HD)
    v_st = [st(v, u) for u in U]
    aa = [_mm_nt(jnp.concatenate([st(at, u), st(rt, u)], axis=0),
                 jnp.concatenate([st(bt, u), st(kt, u)], axis=0)) for u in U]
    a_ab = [jnp.where(strict, aa[u][:R, :R], 0.0) for u in U]
    a_ak = [jnp.where(strict, aa[u][:R, R:], 0.0) for u in U]
    a_rb = [jnp.where(causal, aa[u][R:, :R], 0.0) for u in U]
    a_rk = [jnp.where(causal, aa[u][R:, R:], 0.0) for u in U]
    av = [_mm(jnp.concatenate([a_ak[u], a_rk[u]], axis=0), v_st[u]) for u in U]
    vk = [_mm_tn(v_st[u], _expand_heads(sl(khat, u), NH, L, HD)) for u in U]
    tinv = _tri_inv([-x for x in a_ab], diff, eye, L)
    w_exp = [_mm(tinv[u], _expand_heads(sl(at, u), NH, L, HD)) for u in U]
    u0 = [_mm(tinv[u], av[u][:R]) for u in U]
    for u in U:
        x_ref[0, u] = jnp.concatenate([w_exp[u], _expand_heads(sl(rt, u), NH, L, HD)], axis=0).astype(BF16)
        u0_ref[0, u] = u0[u]
        o0_ref[0, u] = av[u][R:]
        arb_ref[0, u] = a_rb[u].astype(BF16)
        vk_ref[0, u] = vk[u]
        bh_ref[0, u] = _expand_heads(sl(bhat, u), NH, L, HD).astype(BF16)


def _wkv_inter_kernel(L, HG, x_ref, u0_ref, o0_ref, arb_ref, vk_ref, bh_ref, gl_ref, s0_ref,
                      bonus_ref, gate_ref, gnw_ref, gnb_ref, o_ref, so_ref, s_scr):
    c = pl.program_id(1)
    HD, NH = RWKV_HEAD, STACK
    R = NH * L
    W = NH * HD

    @pl.when(c == 0)
    def _():
        for g in range(HG):
            s_scr[g] = jnp.concatenate([s0_ref[0, NH * g + j] for j in range(NH)], axis=1)

    s_old = [s_scr[g] for g in range(HG)]
    d1 = [_mm_nt(x_ref[0, g], s_old[g]) for g in range(HG)]
    u_st = [u0_ref[0, g] + d1[g][:R] for g in range(HG)]
    o_st = [d1[g][R:] + o0_ref[0, g] + _dot(arb_ref[0, g], u_st[g].astype(BF16)) for g in range(HG)]
    for g in range(HG):
        s_scr[g] = s_old[g] * gl_ref[0, :, g * W:(g + 1) * W] + vk_ref[0, g] + _mm_tn(u_st[g], bh_ref[0, g])
    outs = []
    for g in range(HG):
        mean = jnp.mean(o_st[g], axis=-1, keepdims=True)
        d = o_st[g] - mean
        var = jnp.mean(d * d, axis=-1, keepdims=True)
        on = d * lax.rsqrt(var + RWKV_GN_EPS)
        outs += [on[j * L:(j + 1) * L] for j in range(NH)]
    o = jnp.concatenate(outs, axis=1) * gnw_ref[...] + gnb_ref[...] + bonus_ref[...]
    o_ref[...] = (o * _silu(gate_ref[...])).astype(BF16)

    @pl.when(c == pl.num_programs(1) - 1)
    def _():
        for g in range(HG):
            for j in range(NH):
                so_ref[0, NH * g + j] = s_scr[g][:, j * HD:(j + 1) * HD]


def wkv_mix(big, w_pre, a_pre, k_k, k_a, r_k, gn_w, gn_b, s0, row0, B, NC, L):
    HN = s0.shape[1]
    HD, NH, NU = RWKV_HEAD, STACK, WKV_NU
    HG = HN // NH
    W = NH * HD
    R = NH * L
    D = HN * HD
    Tseg = B * NC * L
    NCH = B * NC
    rb0 = row0 // L
    nsec = D // (NU * W)
    assert row0 % L == 0 and HG % NU == 0
    tok = lambda sec: pl.BlockSpec((L, NU * W), lambda bb, h, c: (rb0 + bb * NC + c, sec * nsec + h))
    par = pl.BlockSpec((1, NU * W), lambda bb, h, c: (0, h))
    unit = lambda w: pl.BlockSpec((1, NU, w[0], w[1]), lambda bb, h, c: (bb * NC + c, h, 0, 0))
    x, u0, o0, arb, vk, bh, gl, bonus = pl.pallas_call(
        functools.partial(_wkv_intra_kernel, L),
        grid=(B, HG // NU, NC),
        in_specs=[tok(0), tok(1), tok(2), tok(0), tok(0), par, par, par],
        out_specs=[unit((2 * R, W)), unit((R, HD)), unit((R, HD)), unit((R, R)), unit((HD, W)), unit((R, W)),
                   pl.BlockSpec((1, 1, NU * W), lambda bb, h, c: (bb * NC + c, 0, h)),
                   pl.BlockSpec((L, NU * W), lambda bb, h, c: (bb * NC + c, h))],
        out_shape=[jax.ShapeDtypeStruct((NCH, HG, 2 * R, W), BF16),
                   jax.ShapeDtypeStruct((NCH, HG, R, HD), F32),
                   jax.ShapeDtypeStruct((NCH, HG, R, HD), F32),
                   jax.ShapeDtypeStruct((NCH, HG, R, R), BF16),
                   jax.ShapeDtypeStruct((NCH, HG, HD, W), F32),
                   jax.ShapeDtypeStruct((NCH, HG, R, W), BF16),
                   jax.ShapeDtypeStruct((NCH, 1, D), F32),
                   jax.ShapeDtypeStruct((Tseg, D), F32)],
        compiler_params=_cparams(("parallel", "parallel", "parallel")),
        name="wkv_intra",
    )(big, big, big, w_pre, a_pre, k_k.reshape(1, D), k_a.reshape(1, D), r_k.reshape(1, D))

    allu = lambda w: pl.BlockSpec((1, HG, w[0], w[1]), lambda bb, c: (bb * NC + c, 0, 0, 0))
    o, s_new = pl.pallas_call(
        functools.partial(_wkv_inter_kernel, L, HG),
        grid=(B, NC),
        in_specs=[allu((2 * R, W)), allu((R, HD)), allu((R, HD)), allu((R, R)), allu((HD, W)), allu((R, W)),
                  pl.BlockSpec((1, 1, D), lambda bb, c: (bb * NC + c, 0, 0)),
                  pl.BlockSpec((1, HN, HD, HD), lambda bb, c: (bb, 0, 0, 0)),
                  pl.BlockSpec((L, D), lambda bb, c: (bb * NC + c, 0)),
                  pl.BlockSpec((L, D), lambda bb, c: (rb0 + bb * NC + c, 3)),
                  pl.BlockSpec((1, D), lambda bb, c: (0, 0)),
                  pl.BlockSpec((1, D), lambda bb, c: (0, 0))],
        out_specs=[pl.BlockSpec((L, D), lambda bb, c: (bb * NC + c, 0)),
                   pl.BlockSpec((1, HN, HD, HD), lambda bb, c: (bb, 0, 0, 0))],
        out_shape=[jax.ShapeDtypeStruct((Tseg, D), BF16),
                   jax.ShapeDtypeStruct((B, HN, HD, HD), F32)],
        scratch_shapes=[pltpu.VMEM((HG, HD, W), F32)],
        compiler_params=_cparams(("parallel", "arbitrary")),
        name="wkv_inter",
    )(x, u0, o0, arb, vk, bh, gl, s0, bonus, big, gn_w.reshape(1, D), gn_b.reshape(1, D))
    return o, s_new


def _pad_cols(w, n):
    return jnp.pad(w, ((0, 0), (0, n - w.shape[1])))


def _gdn_layer(X, seg, g_pre, g_post, w_in, conv_w, a_log, dt_bias, o_norm, w_out, st_S, st_conv):
    (T, BS, TS, NM) = seg
    HV = a_log.shape[0]
    CD = conv_w.shape[1]
    VD = HV * GDN_HEAD
    D = w_in.shape[0]
    NH = STACK
    HG = HV // NH
    w_main = w_in[:, :CD + VD].astype(BF16)
    wb = w_in[:, CD + VD:CD + VD + HV].reshape(D, HG, NH)
    wa = w_in[:, CD + VD + HV:].reshape(D, HG, NH)
    w_gate = jnp.pad(jnp.concatenate([wb, wa], axis=2), ((0, 0), (0, 0), (0, LANES - 2 * NH)))
    decay_lanes = lambda p: jnp.pad(p.reshape(HG, NH), ((0, 0), (NH, LANES - 2 * NH))).reshape(1, HG * LANES)
    a_par, dt_par = decay_lanes(a_log), decay_lanes(dt_bias)
    proj = norm_proj(X, g_pre, w_main)
    gates = norm_proj(X, g_pre, w_gate.reshape(D, HG * LANES).astype(BF16))
    gp = (gates, a_par, dt_par)

    r_s, r_m = T, T + BS * TS
    tail = GDN_CONV - 1
    c_m = proj[r_m + NM - tail:r_m + NM, :CD][None]
    c_p = proj[T - tail:T, :CD][None]
    c_s = proj[r_s:r_m, :CD].reshape(BS, TS, CD)[:, TS - tail:]
    z_conv = jnp.zeros((1, tail, CD), F32)
    z_S = jnp.zeros((1, HV, GDN_HEAD, GDN_HEAD), F32)
    o_m, S_m = gdn_mix(proj, *gp, z_conv, z_S, conv_w, o_norm, r_m, 1, 1, NM)
    LP = 64
    o_p, S_p = gdn_mix(proj, *gp, c_m, S_m, conv_w, o_norm, 0, 1, T // LP, LP)
    o_s, S_s = gdn_mix(proj, *gp, st_conv, st_S, conv_w, o_norm, r_s, BS, 1, TS)
    o = jnp.concatenate([o_p, o_s, o_m], axis=0)
    X = out_proj(o, w_out.astype(BF16), X, g_post)
    return X, (S_p, c_p, S_s, c_s)


def _fox_layer(X, seg, g_pre, g_post, w_in, b_f, w_out, caches, j):
    (T, BS, TS, NM) = seg
    cache_k, cache_v, cache_logf = caches
    H = b_f.shape[0]
    FD = H * FOX_HEAD
    qz = norm_proj(X, g_pre, jnp.concatenate([w_in[:, :FD], w_in[:, 3 * FD:4 * FD]], axis=1).astype(BF16))
    kv, kv16 = norm_proj(X, g_pre, w_in[:, FD:3 * FD].astype(BF16), also_bf16=True)
    gates = norm_proj(X, g_pre, _pad_cols(w_in[:, 4 * FD:], LANES).astype(BF16))
    r_s, r_m = T, T + BS * TS
    P = cache_k.shape[2]

    bias = _pad_cols(b_f.reshape(1, H), LANES)
    zero = jnp.zeros((1, LANES), F32)
    lf_m, cs_m = cumsum_rows(gates, r_m, NM, zero, bias)
    lf_p, cs_p = cumsum_rows(gates, 0, T, cs_m[NM - 1:NM], bias)
    lf_s, _ = cumsum_rows(gates, r_s, BS * TS, zero, bias)
    lanes_to_heads = lambda c, n: (c[:, :H] * LOG2E).T.reshape(H, 1, n)
    c_meta, c_main = lanes_to_heads(cs_m, NM), lanes_to_heads(cs_p, T)
    _, cs_c = cumsum_rows(cache_logf[j].transpose(1, 0, 2).reshape(P, BS * H), 0, P, jnp.zeros((1, BS * H), F32))
    lf_new = lf_s[:, :H].reshape(BS, TS, H).transpose(1, 0, 2).reshape(TS, BS * H)
    _, cs_n = cumsum_rows(lf_new, 0, TS, cs_c[P - 1:P])
    pos_last = lambda c, n: (c * LOG2E).reshape(n, BS, H).transpose(1, 2, 0).reshape(BS, H, 1, n)
    c_cache, c_new = pos_last(cs_c, P), pos_last(cs_n, TS)
    logf = jnp.concatenate([lf_p[:, :H], lf_s[:, :H], lf_m[:, :H]], axis=0)

    o_p = fox_prompt(qz, kv16, c_main, c_meta, T, r_m, NM, H)
    o_m = fox_seq(qz, kv16, c_meta.reshape(1, H, 1, NM), r_m, 1, NM, H)
    o_s = fox_seq(qz, kv16, c_new, r_s, BS, TS, H,
                  cache=(cache_k.reshape(-1, FOX_HEAD), cache_v.reshape(-1, FOX_HEAD), c_cache, j * BS))
    o = jnp.concatenate([o_p, o_s, o_m], axis=0)
    X = out_proj(o, w_out.astype(BF16), X, g_post)

    def seq(a, n):
        return jnp.concatenate([a[r_m:], a[:T]], axis=0).reshape(1, NM + T, H, n)

    k_all = kv[:, :FD]
    v_all = kv[:, FD:]
    outs = (seq(k_all, FOX_HEAD), seq(v_all, FOX_HEAD),
            jnp.concatenate([logf[r_m:], logf[:T]], axis=0).reshape(1, NM + T, H),
            k_all[r_s:r_m].reshape(BS, TS, H, FOX_HEAD), v_all[r_s:r_m].reshape(BS, TS, H, FOX_HEAD),
            logf[r_s:r_m].reshape(BS, TS, H))
    return X, outs


def _rwkv_layer(X, seg, g_pre, g_post, mu, w_r, w_k, w_v, w_g, w0, w1, w2, a0, a1, a2, k_k, k_a, r_k,
                gn_w, gn_b, w_out, st_S, st_shift):
    (T, BS, TS, NM) = seg
    D = X.shape[1]
    HN = r_k.shape[0]
    HD = RWKV_HEAD
    r_s, r_m = T, T + BS * TS
    h, prev = norm_shift(X, g_pre)
    h_s = h[r_s:r_m].reshape(BS, TS, D)
    starts = jnp.concatenate([jnp.zeros((1,), jnp.int32), r_s + TS * jnp.arange(BS, dtype=jnp.int32),
                              jnp.full((1,), r_m, jnp.int32)])
    before = jnp.concatenate([h[r_m + NM - 1:r_m + NM], st_shift, jnp.zeros((1, D), F32)], axis=0)
    prev = prev.at[starts].set(before)

    big = mix_proj(h, prev, mu[jnp.array([0, 2, 3, 5])], jnp.stack([w_r, w_k, w_v, w_g]).astype(BF16))
    lora_w = jnp.stack([_pad_cols(w1, LANES), _pad_cols(a1, LANES)]).astype(BF16)
    lo = mix_proj(h, prev, mu[jnp.array([1, 4])], lora_w)
    pad_rows = lambda w: jnp.pad(w, ((0, LANES - w.shape[0]), (0, 0))).astype(BF16)
    w_pre = lora_out(lo, 0, pad_rows(w2), w0, True)
    a_pre = lora_out(lo, 1, pad_rows(a2), a0, False)

    zS = jnp.zeros((1, HN, HD, HD), F32)
    par = (k_k, k_a, r_k, gn_w, gn_b)
    o_m, S_m = wkv_mix(big, w_pre, a_pre, *par, zS, r_m, 1, 1, NM)
    LP = 64
    o_p, S_p = wkv_mix(big, w_pre, a_pre, *par, S_m, 0, 1, T // LP, LP)
    o_s, S_s = wkv_mix(big, w_pre, a_pre, *par, st_S, r_s, BS, 1, TS)
    o = jnp.concatenate([o_p, o_s, o_m], axis=0)
    X = out_proj(o, w_out.astype(BF16), X, g_post)
    return X, (S_p, h[T - 1:T], S_s, h_s[:, -1])


def kernel(x_prompt, x_sample, state_gdn_S, state_gdn_conv, cache_fox_k, cache_fox_v, cache_fox_logf, state_rwkv_S, state_rwkv_shift, meta, norm_pre, norm_post, gdn_w_in, gdn_conv_w, gdn_a_log, gdn_dt_bias, gdn_o_norm, gdn_w_out, fox_w_in, fox_b_f, fox_w_out, rwkv_mu, rwkv_w_r, rwkv_w_k, rwkv_w_v, rwkv_w_g, rwkv_w0, rwkv_w1, rwkv_w2, rwkv_a0, rwkv_a1, rwkv_a2, rwkv_k_k, rwkv_k_a, rwkv_r_k, rwkv_gn_w, rwkv_gn_b, rwkv_w_out):
    _, T, D = x_prompt.shape
    BS, TS, _ = x_sample.shape
    NM = meta.shape[0]
    depth = norm_pre.shape[0]
    assert x_prompt.shape[0] == 1
    seg = (T, BS, TS, NM)
    X = jnp.concatenate([x_prompt[0], x_sample.reshape(BS * TS, D), meta.astype(x_prompt.dtype)], axis=0)

    gdn_out, fox_out, rwkv_out = [], [], []
    for i in range(depth):
        kind, j = i % 3, i // 3
        if kind == 0:
            X, st = _gdn_layer(X, seg, norm_pre[i], norm_post[i], gdn_w_in[j], gdn_conv_w[j], gdn_a_log[j],
                               gdn_dt_bias[j], gdn_o_norm[j], gdn_w_out[j], state_gdn_S[j], state_gdn_conv[j])
            gdn_out.append(st)
        elif kind == 1:
            X, st = _fox_layer(X, seg, norm_pre[i], norm_post[i], fox_w_in[j], fox_b_f[j], fox_w_out[j],
                               (cache_fox_k, cache_fox_v, cache_fox_logf), j)
            fox_out.append(st)
        else:
            X, st = _rwkv_layer(X, seg, norm_pre[i], norm_post[i], rwkv_mu[j], rwkv_w_r[j], rwkv_w_k[j],
                                rwkv_w_v[j], rwkv_w_g[j], rwkv_w0[j], rwkv_w1[j], rwkv_w2[j], rwkv_a0[j],
                                rwkv_a1[j], rwkv_a2[j], rwkv_k_k[j], rwkv_k_a[j], rwkv_r_k[j], rwkv_gn_w[j],
                                rwkv_gn_b[j], rwkv_w_out[j], state_rwkv_S[j], state_rwkv_shift[j])
            rwkv_out.append(st)

    stack = lambda items, n: jnp.stack([it[n] for it in items])
    y_prompt = X[:T][None]
    y_sample = X[T:T + BS * TS].reshape(BS, TS, D)
    return (y_prompt, y_sample,
            stack(gdn_out, 0), stack(gdn_out, 1),
            stack(fox_out, 0), stack(fox_out, 1), stack(fox_out, 2),
            stack(rwkv_out, 0), stack(rwkv_out, 1),
            stack(gdn_out, 2), stack(gdn_out, 3),
            stack(fox_out, 3), stack(fox_out, 4), stack(fox_out, 5),
            stack(rwkv_out, 2), stack(rwkv_out, 3))
```

```python
import functools

import jax
import jax.numpy as jnp
from jax import lax
from jax.experimental import pallas as pl
from jax.experimental.pallas import tpu as pltpu

F32 = jnp.float32
BF16 = jnp.bfloat16

NORM_EPS = 1e-6
L2_EPS = 1e-6
RWKV_GN_EPS = 64e-5
GDN_HEAD = 128
GDN_CONV = 4
FOX_HEAD = 128
RWKV_HEAD = 64
LANES = 128
SUBLANES = 8
CONV_PAD = 8
INV_BASE = 16
STACK = 4
VMEM_LIMIT = 48 * 1024 * 1024
VMEM_LIMIT_BIG = 56 * 1024 * 1024


def _cparams(sem, limit=VMEM_LIMIT):
    return pltpu.CompilerParams(dimension_semantics=sem, vmem_limit_bytes=limit)


def _log2(n):
    assert n & (n - 1) == 0
    return n.bit_length() - 1


def _dot(a, b):
    return jnp.dot(a, b, preferred_element_type=F32)


def _mm(a, b):
    return _dot(a.astype(BF16), b.astype(BF16))


def _mm_nt(a, b):
    return lax.dot_general(a.astype(BF16), b.astype(BF16), (((1,), (1,)), ((), ())),
                           preferred_element_type=F32)


def _mm_tn(a, b):
    return lax.dot_general(a.astype(BF16), b.astype(BF16), (((0,), (0,)), ((), ())),
                           preferred_element_type=F32)


def _split3(a):
    hi = a.astype(BF16)
    r = a - hi.astype(F32)
    mid = r.astype(BF16)
    lo = (r - mid.astype(F32)).astype(BF16)
    return hi, mid, lo


def _cumsum_rows(tril, g):
    hi, mid, lo = _split3(g)
    return _dot(tril, hi) + (_dot(tril, mid) + _dot(tril, lo))


def _cumsum_cols_t(g, triu):
    hi, mid, lo = _split3(g)
    return _mm_tn(hi, triu) + (_mm_tn(mid, triu) + _mm_tn(lo, triu))


def _tri_inv(a_list, diff, eye, L):
    base = diff < INV_BASE
    n1 = [jnp.where(base, -a, 0.0) for a in a_list]
    p = [eye + n for n in n1]
    npow = n1
    steps = 1
    while 2 * steps < INV_BASE:
        npow = [_mm(x, x) for x in npow]
        p = [pp + _mm(pp, x) for pp, x in zip(p, npow)]
        steps *= 2
    blk = INV_BASE
    while blk < L:
        off_diag = (diff < 2 * blk) & (diff >= blk)
        e = [jnp.where(off_diag, a, 0.0) for a in a_list]
        pe = [_mm(pp, ee) for pp, ee in zip(p, e)]
        p = [pp - _mm(x, pp) for pp, x in zip(p, pe)]
        blk *= 2
    return p


def _stack_heads(x, n, w):
    return jnp.concatenate([x[:, j * w:(j + 1) * w] for j in range(n)], axis=0)


def _expand_heads(x, n, L, w):
    t = jnp.concatenate([x] * n, axis=0)
    rh = lax.shift_right_logical(lax.broadcasted_iota(jnp.int32, t.shape, 0), _log2(L))
    ch = lax.shift_right_logical(lax.broadcasted_iota(jnp.int32, t.shape, 1), _log2(w))
    return jnp.where(rh == ch, t, 0.0)


def _silu(x):
    return x * jax.nn.sigmoid(x)


NORM_PROJ_ROWS = 1600
ROW_TILE_CAP = 544


def _row_tile(T, cap, exact=False):
    best = 0
    for t in range(16, cap + 1, 16):
        if T % t == 0:
            best = t
    assert best > 0
    return best if exact or best >= cap // 4 else (cap // 16) * 16


def _norm_proj_kernel(x_ref, g_ref, w_ref, o_ref, *rest):
    h_scr = rest[-1]

    @pl.when(pl.program_id(1) == 0)
    def _():
        x = x_ref[...]
        ms = jnp.mean(x * x, axis=-1, keepdims=True)
        h_scr[...] = (x * lax.rsqrt(ms + NORM_EPS) * g_ref[...]).astype(BF16)

    y = _dot(h_scr[...], w_ref[...])
    o_ref[...] = y
    if len(rest) == 2:
        rest[0][...] = y.astype(BF16)


def norm_proj(x, g, w, tn=512, also_bf16=False):
    T, D = x.shape
    N = w.shape[1]
    tn = min(tn, N)
    tm = _row_tile(T, NORM_PROJ_ROWS)
    assert N % tn == 0
    out_spec = pl.BlockSpec((tm, tn), lambda i, j: (i, j))
    out_specs, out_shape = out_spec, jax.ShapeDtypeStruct((T, N), F32)
    if also_bf16:
        out_specs, out_shape = [out_spec, out_spec], [out_shape, jax.ShapeDtypeStruct((T, N), BF16)]
    return pl.pallas_call(
        _norm_proj_kernel,
        grid=(pl.cdiv(T, tm), N // tn),
        in_specs=[pl.BlockSpec((tm, D), lambda i, j: (i, 0)),
                  pl.BlockSpec((1, D), lambda i, j: (0, 0)),
                  pl.BlockSpec((D, tn), lambda i, j: (0, j))],
        out_specs=out_specs,
        out_shape=out_shape,
        scratch_shapes=[pltpu.VMEM((tm, D), BF16)],
        compiler_params=_cparams(("parallel", "arbitrary"), VMEM_LIMIT_BIG),
        name="norm_proj",
    )(x, g.reshape(1, D), w)


def _norm_shift_kernel(x_ref, g_ref, h_ref, p_ref, last_scr):
    tm = x_ref.shape[0]

    @pl.when(pl.program_id(0) == 0)
    def _():
        last_scr[...] = jnp.zeros_like(last_scr)

    x = x_ref[...]
    ms = jnp.mean(x * x, axis=-1, keepdims=True)
    h = x * lax.rsqrt(ms + NORM_EPS) * g_ref[...]
    h_ref[...] = h
    p_ref[0:1, :] = last_scr[...]
    p_ref[1:tm, :] = h[0:tm - 1, :]
    last_scr[...] = h[tm - 1:tm, :]


def norm_shift(x, g):
    T, D = x.shape
    tm = _row_tile(T, ROW_TILE_CAP, exact=True)
    spec = pl.BlockSpec((tm, D), lambda i: (i, 0))
    return pl.pallas_call(
        _norm_shift_kernel,
        grid=(T // tm,),
        in_specs=[spec, pl.BlockSpec((1, D), lambda i: (0, 0))],
        out_specs=[spec, spec],
        out_shape=[jax.ShapeDtypeStruct((T, D), F32), jax.ShapeDtypeStruct((T, D), F32)],
        scratch_shapes=[pltpu.VMEM((1, D), F32)],
        compiler_params=_cparams(("arbitrary",)),
        name="norm_shift",
    )(x, g.reshape(1, D))


CUMSUM_ROWS = 512


def _cumsum_kernel(log_sigmoid, x_ref, b_ref, init_ref, f_ref, c_ref, carry):
    tp = x_ref.shape[0]

    @pl.when(pl.program_id(0) == 0)
    def _():
        carry[...] = init_ref[...]

    x = x_ref[...]
    if log_sigmoid:
        x = -(x + b_ref[...])
        x = -(jnp.maximum(x, 0.0) + jnp.log(1.0 + jnp.exp(-jnp.abs(x))))
    f_ref[...] = x
    r = lax.broadcasted_iota(jnp.int32, (tp, tp), 0)
    c = lax.broadcasted_iota(jnp.int32, (tp, tp), 1)
    y = _cumsum_rows(jnp.where(c <= r, 1.0, 0.0).astype(BF16), x) + carry[...]
    carry[...] = y[tp - 1:tp, :]
    c_ref[...] = y


def cumsum_rows(x, row0, n, init, bias=None):
    C = x.shape[1]
    tp = min(CUMSUM_ROWS, n)
    assert n % tp == 0 and row0 % tp == 0
    rb0 = row0 // tp
    b = jnp.zeros((1, C), F32) if bias is None else bias
    return pl.pallas_call(
        functools.partial(_cumsum_kernel, bias is not None),
        grid=(n // tp,),
        in_specs=[pl.BlockSpec((tp, C), lambda i: (rb0 + i, 0)),
                  pl.BlockSpec((1, C), lambda i: (0, 0)),
                  pl.BlockSpec((1, C), lambda i: (0, 0))],
        out_specs=[pl.BlockSpec((tp, C), lambda i: (i, 0)), pl.BlockSpec((tp, C), lambda i: (i, 0))],
        out_shape=[jax.ShapeDtypeStruct((n, C), F32), jax.ShapeDtypeStruct((n, C), F32)],
        scratch_shapes=[pltpu.VMEM((1, C), F32)],
        compiler_params=_cparams(("arbitrary",)),
        name="cumsum_rows",
    )(x, b, init)


def _mix_proj_kernel(h_ref, p_ref, mu_ref, w_ref, o_ref, l_scr):
    @pl.when(pl.program_id(2) == 0)
    def _():
        h = h_ref[...]
        l_scr[...] = (h + (p_ref[...] - h) * mu_ref[0]).astype(BF16)

    o_ref[...] = _dot(l_scr[...], w_ref[0])


def mix_proj(h, prev, mu, w, tn=1024):
    T, D = h.shape
    tm = _row_tile(T, ROW_TILE_CAP)
    G, _, N = w.shape
    tn = min(tn, N)
    nj = N // tn
    return pl.pallas_call(
        _mix_proj_kernel,
        grid=(pl.cdiv(T, tm), G, nj),
        in_specs=[pl.BlockSpec((tm, D), lambda i, g, j: (i, 0)),
                  pl.BlockSpec((tm, D), lambda i, g, j: (i, 0)),
                  pl.BlockSpec((1, 1, D), lambda i, g, j: (g, 0, 0)),
                  pl.BlockSpec((1, D, tn), lambda i, g, j: (g, 0, j))],
        out_specs=pl.BlockSpec((tm, tn), lambda i, g, j: (i, g * nj + j)),
        out_shape=jax.ShapeDtypeStruct((T, G * N), F32),
        scratch_shapes=[pltpu.VMEM((tm, D), BF16)],
        compiler_params=_cparams(("parallel", "arbitrary", "arbitrary")),
        name="mix_proj",
    )(h, prev, mu.reshape(G, 1, D), w)


def _lora_kernel(use_tanh, x_ref, w_ref, b_ref, o_ref):
    x = x_ref[...]
    if use_tanh:
        x = jnp.tanh(x)
    o_ref[...] = b_ref[...] + _dot(x.astype(BF16), w_ref[...])


def lora_out(x, col_block, w, b, use_tanh):
    T = x.shape[0]
    tm = _row_tile(T, ROW_TILE_CAP)
    K, N = w.shape
    return pl.pallas_call(
        functools.partial(_lora_kernel, use_tanh),
        grid=(pl.cdiv(T, tm),),
        in_specs=[pl.BlockSpec((tm, K), lambda i: (i, col_block)),
                  pl.BlockSpec((K, N), lambda i: (0, 0)),
                  pl.BlockSpec((1, N), lambda i: (0, 0))],
        out_specs=pl.BlockSpec((tm, N), lambda i: (i, 0)),
        out_shape=jax.ShapeDtypeStruct((T, N), F32),
        compiler_params=_cparams(("parallel",)),
        name="lora_out",
    )(x, w, b.reshape(1, N))


def _out_proj_kernel(tn, a_ref, w_ref, x_ref, g_ref, o_ref, y_scr):
    j = pl.program_id(1)
    off = pl.multiple_of(j * tn, tn)
    y_scr[:, pl.ds(off, tn)] = _dot(a_ref[...], w_ref[...])

    @pl.when(j == pl.num_programs(1) - 1)
    def _():
        y = y_scr[...]
        ms = jnp.mean(y * y, axis=-1, keepdims=True)
        o_ref[...] = x_ref[...] + y * lax.rsqrt(ms + NORM_EPS) * g_ref[...]


def out_proj(a, w, x, g, tn=512):
    T, K = a.shape
    tm = _row_tile(T, ROW_TILE_CAP)
    D = w.shape[1]
    return pl.pallas_call(
        functools.partial(_out_proj_kernel, tn),
        grid=(pl.cdiv(T, tm), D // tn),
        in_specs=[pl.BlockSpec((tm, K), lambda i, j: (i, 0)),
                  pl.BlockSpec((K, tn), lambda i, j: (0, j)),
                  pl.BlockSpec((tm, D), lambda i, j: (i, 0)),
                  pl.BlockSpec((1, D), lambda i, j: (0, 0))],
        out_specs=pl.BlockSpec((tm, D), lambda i, j: (i, 0)),
        out_shape=jax.ShapeDtypeStruct((T, D), F32),
        scratch_shapes=[pltpu.VMEM((tm, D), F32)],
        compiler_params=_cparams(("parallel", "arbitrary")),
        name="out_proj",
    )(a, w, x, g.reshape(1, D))


GDN_NU = 8


def _gdn_intra_kernel(L, qn_ref, kn_ref, vn_ref, qp_ref, kp_ref, vp_ref, cq_ref, ck_ref, cv_ref,
                      wq_ref, wk_ref, wv_ref, gt_ref, al_ref, dt_ref,
                      uv_ref, wqe_ref, kt_ref, qkd_ref, egl_ref, qbuf, kbuf, vbuf):
    c = pl.program_id(2)
    HD, NH = GDN_HEAD, STACK
    R = NH * L
    first = c == 0

    def conv_silu(buf, u_ref, p_ref, st_ref, w_ref):
        buf[CONV_PAD - 3:CONV_PAD, :] = jnp.where(first, st_ref[0], p_ref[SUBLANES - 3:SUBLANES, :])
        buf[CONV_PAD:CONV_PAD + L, :] = u_ref[...]
        acc = buf[CONV_PAD - 3:CONV_PAD - 3 + L, :] * w_ref[0:1, :]
        acc = acc + buf[CONV_PAD - 2:CONV_PAD - 2 + L, :] * w_ref[1:2, :]
        acc = acc + buf[CONV_PAD - 1:CONV_PAD - 1 + L, :] * w_ref[2:3, :]
        acc = acc + buf[CONV_PAD:CONV_PAD + L, :] * w_ref[3:4, :]
        return _silu(acc)

    q = conv_silu(qbuf, qn_ref, qp_ref, cq_ref, wq_ref)
    k = conv_silu(kbuf, kn_ref, kp_ref, ck_ref, wk_ref)
    v = conv_silu(vbuf, vn_ref, vp_ref, cv_ref, wv_ref)

    rl = lax.broadcasted_iota(jnp.int32, (L, L), 0)
    cl = lax.broadcasted_iota(jnp.int32, (L, L), 1)
    tril = jnp.where(cl <= rl, 1.0, 0.0).astype(BF16)
    triu = jnp.where(rl <= cl, 1.0, 0.0).astype(BF16)
    row = lax.broadcasted_iota(jnp.int32, (R, R), 0)
    col = lax.broadcasted_iota(jnp.int32, (R, R), 1)
    diff = row ^ col
    causal = (diff < L) & (col <= row)
    strict = (diff < L) & (col < row)
    eye = jnp.where(row == col, 1.0, 0.0).astype(F32)

    def l2n(x):
        return x * lax.rsqrt(jnp.sum(x * x, axis=-1, keepdims=True) + L2_EPS)

    def col_stack(x, first_lane):
        return jnp.concatenate([x[:, first_lane + j:first_lane + j + 1] for j in range(NH)], axis=0)

    kst, qst, vst, gst, bst, glast_st, grow_st, glast = [], [], [], [], [], [], [], []
    for u in range(GDN_NU):
        qh = [l2n(q[:, (2 * u + i) * HD:(2 * u + i + 1) * HD]) * (HD ** -0.5) for i in range(2)]
        kh = [l2n(k[:, (2 * u + i) * HD:(2 * u + i + 1) * HD]) for i in range(2)]
        kst.append(jnp.concatenate([kh[0], kh[0], kh[1], kh[1]], axis=0))
        qst.append(jnp.concatenate([qh[0], qh[0], qh[1], qh[1]], axis=0))
        vst.append(_stack_heads(v[:, u * NH * HD:(u + 1) * NH * HD], NH, HD))
        lanes = slice(u * LANES, (u + 1) * LANES)
        gu = gt_ref[:, lanes]
        beta = jax.nn.sigmoid(gu)
        sp = gu + dt_ref[:, lanes]
        sp = jnp.maximum(sp, 0.0) + jnp.log(1.0 + jnp.exp(-jnp.abs(sp)))
        g = -jnp.exp(al_ref[:, lanes]) * sp
        gcol = _cumsum_rows(tril, g)
        grow = _cumsum_cols_t(g, triu)
        gl = gcol[L - 1:L, :]
        glast.append(gl)
        gst.append(col_stack(gcol, NH))
        bst.append(col_stack(beta, 0))
        glast_st.append(jnp.concatenate([jnp.broadcast_to(gl[:, NH + j:NH + j + 1], (L, 1)) for j in range(NH)],
                                        axis=0))
        grow_st.append(jnp.concatenate([grow[NH + j:NH + j + 1, :] for j in range(NH)], axis=1))

    U = range(GDN_NU)
    kk = [_mm_nt(kst[u], kst[u]) for u in U]
    qk = [_mm_nt(qst[u], kst[u]) for u in U]
    decay = [jnp.where(causal, jnp.exp(jnp.where(causal, gst[u] - grow_st[u], 0.0)), 0.0) for u in U]
    a = [jnp.where(strict, kk[u] * decay[u] * bst[u], 0.0) for u in U]
    tinv = _tri_inv(a, diff, eye, L)
    rhs = [jnp.concatenate([vst[u] * bst[u], kst[u] * (bst[u] * jnp.exp(gst[u]))], axis=1) for u in U]
    sol = [_mm(tinv[u], rhs[u]) for u in U]
    for u in U:
        uv_ref[0, u] = sol[u][:, :HD]
        wk = sol[u][:, HD:]
        qe = qst[u] * jnp.exp(gst[u])
        wqe_ref[0, u] = jnp.concatenate(
            [x[j * L:(j + 1) * L] for j in range(NH) for x in (wk, qe)], axis=0).astype(BF16)
        kt_ref[0, u] = (kst[u] * jnp.exp(glast_st[u] - gst[u])).astype(BF16)
        qkd_ref[0, u] = (qk[u] * decay[u]).astype(BF16)
        egl_ref[0, u] = jnp.exp(glast[u])


def _gdn_inter_kernel(L, HG, uv_ref, wqe_ref, kt_ref, qkd_ref, egl_ref, z_ref, s0_ref, on_ref,
                      o_ref, so_ref, s_scr):
    c = pl.program_id(1)
    HD, NH = GDN_HEAD, STACK
    R = NH * L

    @pl.when(c == 0)
    def _():
        for g in range(HG):
            s_scr[g] = jnp.concatenate([s0_ref[0, NH * g + j] for j in range(NH)], axis=1)

    rh = lax.shift_right_logical(lax.broadcasted_iota(jnp.int32, (R, NH * HD), 0), _log2(L))
    ch = lax.shift_right_logical(lax.broadcasted_iota(jnp.int32, (R, NH * HD), 1), _log2(HD))
    own = rh == ch

    s_old = [s_scr[g] for g in range(HG)]
    d1 = [[_dot(wqe_ref[0, g, 2 * L * j:2 * L * (j + 1), :], s_old[g][:, j * HD:(j + 1) * HD].astype(BF16))
           for j in range(NH)] for g in range(HG)]
    v_new = [uv_ref[0, g] - jnp.concatenate([d1[g][j][:L] for j in range(NH)], axis=0) for g in range(HG)]
    o_st = [jnp.concatenate([d1[g][j][L:] for j in range(NH)], axis=0) + _dot(qkd_ref[0, g], v_new[g].astype(BF16))
            for g in range(HG)]
    for g in range(HG):
        v_exp = jnp.where(own, jnp.concatenate([v_new[g]] * NH, axis=1), 0.0)
        egl = egl_ref[0, g]
        e_exp = jnp.concatenate([jnp.broadcast_to(egl[:, NH + j:NH + j + 1], (1, HD)) for j in range(NH)], axis=1)
        s_scr[g] = s_old[g] * e_exp + _mm_tn(kt_ref[0, g], v_exp)

    outs = []
    for g in range(HG):
        for j in range(NH):
            o = o_st[g][j * L:(j + 1) * L]
            zj = z_ref[:, (NH * g + j) * HD:(NH * g + j + 1) * HD]
            ms = jnp.mean(o * o, axis=-1, keepdims=True)
            outs.append((o * lax.rsqrt(ms + NORM_EPS) * on_ref[...] * _silu(zj)).astype(BF16))
    o_ref[...] = jnp.concatenate(outs, axis=1)

    @pl.when(c == pl.num_programs(1) - 1)
    def _():
        for g in range(HG):
            for j in range(NH):
                so_ref[0, NH * g + j] = s_scr[g][:, j * HD:(j + 1) * HD]


def gdn_mix(proj, gate_col, a_par, dt_par, conv_state, s0, conv_w, o_norm, row0, B, NC, L):
    HV = s0.shape[1]
    HD, NH, NU = GDN_HEAD, STACK, GDN_NU
    HG = HV // NH
    QW = NU * (NH // 2) * HD
    VW = NU * NH * HD
    nq = (HV // 2) * HD // QW
    nv = HV * HD // VW
    Tseg = B * NC * L
    R = NH * L
    rb0 = row0 // L
    assert row0 % L == 0 and L % SUBLANES == 0 and HG % NU == 0 and gate_col % (NU * LANES) == 0
    gate_blk = gate_col // (NU * LANES)

    cq = conv_state[:, :, :nq * QW]
    ck = conv_state[:, :, nq * QW:2 * nq * QW]
    cv = conv_state[:, :, 2 * nq * QW:]
    wq = conv_w[:, :nq * QW]
    wk = conv_w[:, nq * QW:2 * nq * QW]
    wv = conv_w[:, 2 * nq * QW:]

    chunk = lambda b, h, c: rb0 + b * NC + c
    before = lambda b, h, c: jnp.maximum(chunk(b, h, c) * (L // SUBLANES) - 1, 0)
    intra_in = [
        pl.BlockSpec((L, QW), lambda b, h, c: (chunk(b, h, c), h)),
        pl.BlockSpec((L, QW), lambda b, h, c: (chunk(b, h, c), nq + h)),
        pl.BlockSpec((L, VW), lambda b, h, c: (chunk(b, h, c), nv + h)),
        pl.BlockSpec((SUBLANES, QW), lambda b, h, c: (before(b, h, c), h)),
        pl.BlockSpec((SUBLANES, QW), lambda b, h, c: (before(b, h, c), nq + h)),
        pl.BlockSpec((SUBLANES, VW), lambda b, h, c: (before(b, h, c), nv + h)),
        pl.BlockSpec((1, GDN_CONV - 1, QW), lambda b, h, c: (b, 0, h)),
        pl.BlockSpec((1, GDN_CONV - 1, QW), lambda b, h, c: (b, 0, h)),
        pl.BlockSpec((1, GDN_CONV - 1, VW), lambda b, h, c: (b, 0, h)),
        pl.BlockSpec((GDN_CONV, QW), lambda b, h, c: (0, h)),
        pl.BlockSpec((GDN_CONV, QW), lambda b, h, c: (0, h)),
        pl.BlockSpec((GDN_CONV, VW), lambda b, h, c: (0, h)),
        pl.BlockSpec((L, NU * LANES), lambda b, h, c: (chunk(b, h, c), gate_blk + h)),
        pl.BlockSpec((1, NU * LANES), lambda b, h, c: (0, h)),
        pl.BlockSpec((1, NU * LANES), lambda b, h, c: (0, h)),
    ]
    unit = lambda w: pl.BlockSpec((1, NU, w[0], w[1]), lambda b, h, c: (b * NC + c, h, 0, 0))
    NCH = B * NC
    uv, wqe, kt, qkd, egl = pl.pallas_call(
        functools.partial(_gdn_intra_kernel, L),
        grid=(B, HG // NU, NC),
        in_specs=intra_in,
        out_specs=[unit((R, HD)), unit((2 * R, HD)), unit((R, HD)), unit((R, R)), unit((1, LANES))],
        out_shape=[jax.ShapeDtypeStruct((NCH, HG, R, HD), F32),
                   jax.ShapeDtypeStruct((NCH, HG, 2 * R, HD), BF16),
                   jax.ShapeDtypeStruct((NCH, HG, R, HD), BF16),
                   jax.ShapeDtypeStruct((NCH, HG, R, R), BF16),
                   jax.ShapeDtypeStruct((NCH, HG, 1, LANES), F32)],
        scratch_shapes=[pltpu.VMEM((CONV_PAD + L, QW), F32),
                        pltpu.VMEM((CONV_PAD + L, QW), F32),
                        pltpu.VMEM((CONV_PAD + L, VW), F32)],
        compiler_params=_cparams(("parallel", "parallel", "parallel")),
        name="gdn_intra",
    )(proj, proj, proj, proj, proj, proj, cq, ck, cv, wq, wk, wv, proj, a_par, dt_par)

    allu = lambda w: pl.BlockSpec((1, HG, w[0], w[1]), lambda b, c: (b * NC + c, 0, 0, 0))
    zblk = 2 * nv * VW // (HV * HD)
    o, s_new = pl.pallas_call(
        functools.partial(_gdn_inter_kernel, L, HG),
        grid=(B, NC),
        in_specs=[allu((R, HD)), allu((2 * R, HD)), allu((R, HD)), allu((R, R)), allu((1, LANES)),
                  pl.BlockSpec((L, HV * HD), lambda b, c: (rb0 + b * NC + c, zblk)),
                  pl.BlockSpec((1, HV, HD, HD), lambda b, c: (b, 0, 0, 0)),
                  pl.BlockSpec((1, HD), lambda b, c: (0, 0))],
        out_specs=[pl.BlockSpec((L, HV * HD), lambda b, c: (b * NC + c, 0)),
                   pl.BlockSpec((1, HV, HD, HD), lambda b, c: (b, 0, 0, 0))],
        out_shape=[jax.ShapeDtypeStruct((Tseg, HV * HD), BF16),
                   jax.ShapeDtypeStruct((B, HV, HD, HD), F32)],
        scratch_shapes=[pltpu.VMEM((HG, HD, NH * HD), F32)],
        compiler_params=_cparams(("parallel", "arbitrary")),
        name="gdn_inter",
    )(uv, wqe, kt, qkd, egl, proj, s0, o_norm.reshape(1, HD))
    return o, s_new


FOX_NH = 2
FOX_TP = 512
LOG2E = 1.4426950408889634


def _fox_prompt_kernel(TQ, scale, q_ref, k_ref, v_ref, km_ref, vm_ref, z_ref, ck_ref, cm_ref, o_ref,
                       sa_scr, sb_scr):
    qi = pl.program_id(1)
    HD = FOX_HEAD
    HS = range(FOX_NH)
    hs = lambda i: slice(i * HD, (i + 1) * HD)
    q = [(q_ref[:, hs(i)] * (scale * LOG2E)).astype(BF16) for i in HS]

    s = [_mm_nt(q[i], km_ref[:, hs(i)]) - cm_ref[i] for i in HS]
    m = [jnp.max(s[i], axis=-1, keepdims=True) for i in HS]
    p = [jnp.exp2(s[i] - m[i]) for i in HS]
    l = [jnp.sum(p[i], axis=-1, keepdims=True) for i in HS]
    acc = [_mm(p[i], vm_ref[:, hs(i)]) for i in HS]

    def scores_into(scr, kb):
        off = pl.multiple_of(kb * TQ, TQ)
        for i in HS:
            scr[i] = _mm_nt(q[i], k_ref[pl.ds(off, TQ), hs(i)]) - ck_ref[i, :, pl.ds(off, TQ)]

    def fold(scr, kb, m, l, acc, diagonal):
        off = pl.multiple_of(kb * TQ, TQ)
        s = [scr[i] for i in HS]
        if diagonal:
            row = lax.broadcasted_iota(jnp.int32, (TQ, TQ), 0)
            col = lax.broadcasted_iota(jnp.int32, (TQ, TQ), 1)
            s = [jnp.where(col <= row, s[i], -jnp.inf) for i in HS]
        m_new = [jnp.maximum(m[i], jnp.max(s[i], axis=-1, keepdims=True)) for i in HS]
        alpha = [jnp.exp2(m[i] - m_new[i]) for i in HS]
        p = [jnp.exp2(s[i] - m_new[i]) for i in HS]
        l = [alpha[i] * l[i] + jnp.sum(p[i], axis=-1, keepdims=True) for i in HS]
        acc = [alpha[i] * acc[i] + _mm(p[i], v_ref[pl.ds(off, TQ), hs(i)]) for i in HS]
        return tuple(m_new), tuple(l), tuple(acc)

    scores_into(sa_scr, 0)

    def pair(j, carry):
        scores_into(sb_scr, 2 * j + 1)
        carry = fold(sa_scr, 2 * j, *carry, False)
        scores_into(sa_scr, 2 * j + 2)
        return fold(sb_scr, 2 * j + 1, *carry, False)

    carry = lax.fori_loop(0, qi // 2, pair, (tuple(m), tuple(l), tuple(acc)))

    def last_even(*carry):
        return fold(sa_scr, qi, *carry, True)

    def last_odd(*carry):
        scores_into(sb_scr, qi)
        carry = fold(sa_scr, qi - 1, *carry, False)
        return fold(sb_scr, qi, *carry, True)

    m, l, acc = lax.cond(qi % 2 == 0, last_even, last_odd, *carry)
    o_ref[...] = jnp.concatenate([(acc[i] / l[i]) * _silu(z_ref[:, hs(i)]) for i in HS], axis=1).astype(BF16)


def fox_prompt(qz, kv16, c_main, c_meta, T, meta_row0, n_meta, H, TQ=512):
    HD = FOX_HEAD
    W = FOX_NH * HD
    HP = H // FOX_NH
    TQ = min(TQ, T)
    assert T % TQ == 0 and meta_row0 % n_meta == 0 and H % FOX_NH == 0
    mb = meta_row0 // n_meta
    return pl.pallas_call(
        functools.partial(_fox_prompt_kernel, TQ, HD ** -0.5),
        grid=(HP, T // TQ),
        in_specs=[pl.BlockSpec((TQ, W), lambda h, i: (i, h)),
                  pl.BlockSpec((T, W), lambda h, i: (0, h)),
                  pl.BlockSpec((T, W), lambda h, i: (0, HP + h)),
                  pl.BlockSpec((n_meta, W), lambda h, i: (mb, h)),
                  pl.BlockSpec((n_meta, W), lambda h, i: (mb, HP + h)),
                  pl.BlockSpec((TQ, W), lambda h, i: (i, HP + h)),
                  pl.BlockSpec((FOX_NH, 1, T), lambda h, i: (h, 0, 0)),
                  pl.BlockSpec((FOX_NH, 1, n_meta), lambda h, i: (h, 0, 0))],
        out_specs=pl.BlockSpec((TQ, W), lambda h, i: (i, h)),
        out_shape=jax.ShapeDtypeStruct((T, H * HD), BF16),
        scratch_shapes=[pltpu.VMEM((FOX_NH, TQ, TQ), F32), pltpu.VMEM((FOX_NH, TQ, TQ), F32)],
        compiler_params=_cparams(("parallel", "arbitrary"), VMEM_LIMIT_BIG),
        name="fox_prompt",
    )(qz, kv16, kv16, kv16, kv16, qz, c_main, c_meta)


def _fox_seq_kernel(has_cache, scale, q_ref, k_ref, v_ref, z_ref, cn_ref, *rest):
    if has_cache:
        kc_ref, vc_ref, cc_ref, o_ref, m_scr, l_scr, a_scr = rest
    else:
        o_ref, m_scr, l_scr, a_scr = rest
    c = pl.program_id(1)
    TQ = q_ref.shape[0]
    HD = FOX_HEAD
    H = q_ref.shape[1] // HD
    HS = range(H)
    hs = lambda h: slice(h * HD, (h + 1) * HD)
    q = [(q_ref[:, hs(h)] * (scale * LOG2E)).astype(BF16) for h in HS]

    @pl.when(c == 0)
    def _():
        row = lax.broadcasted_iota(jnp.int32, (TQ, TQ), 0)
        col = lax.broadcasted_iota(jnp.int32, (TQ, TQ), 1)
        s = [jnp.where(col <= row, _mm_nt(q[h], k_ref[:, hs(h)]) - cn_ref[0, h], -jnp.inf) for h in HS]
        m = [jnp.max(s[h], axis=-1, keepdims=True) for h in HS]
        p = [jnp.exp2(s[h] - m[h]) for h in HS]
        for h in HS:
            m_scr[h] = m[h]
            l_scr[h] = jnp.sum(p[h], axis=-1, keepdims=True)
            a_scr[h] = _mm(p[h], v_ref[:, hs(h)])

    if has_cache:
        TP = cc_ref.shape[-1]
        s = [_mm_nt(q[h], kc_ref[pl.ds(h, TP, stride=H), :]) - cc_ref[0, h] for h in HS]
        m_old = [m_scr[h] for h in HS]
        l_old = [l_scr[h] for h in HS]
        a_old = [a_scr[h] for h in HS]
        m_new = [jnp.maximum(m_old[h], jnp.max(s[h], axis=-1, keepdims=True)) for h in HS]
        alpha = [jnp.exp2(m_old[h] - m_new[h]) for h in HS]
        p = [jnp.exp2(s[h] - m_new[h]) for h in HS]
        pv = [_mm(p[h], vc_ref[pl.ds(h, TP, stride=H), :]) for h in HS]
        for h in HS:
            m_scr[h] = m_new[h]
            l_scr[h] = alpha[h] * l_old[h] + jnp.sum(p[h], axis=-1, keepdims=True)
            a_scr[h] = alpha[h] * a_old[h] + pv[h]

    @pl.when(c == pl.num_programs(1) - 1)
    def _():
        o_ref[...] = jnp.concatenate([(a_scr[h] / l_scr[h]) * _silu(z_ref[:, hs(h)]) for h in HS],
                                     axis=1).astype(BF16)


def fox_seq(qz, kv16, c_new, row0, B, TQ, H, cache=None):
    HD = FOX_HEAD
    FD = H * HD
    assert row0 % TQ == 0
    rb0 = row0 // TQ
    tok = lambda sec: pl.BlockSpec((TQ, FD), lambda b, c: (rb0 + b, sec))
    in_specs = [tok(0), tok(0), tok(1), tok(1), pl.BlockSpec((1, H, 1, TQ), lambda b, c: (b, 0, 0, 0))]
    args = [qz, kv16, kv16, qz, c_new]
    nck = 1
    if cache is not None:
        kc, vc, cc, b0 = cache
        P = cc.shape[-1]
        TP = min(FOX_TP, P)
        assert P % TP == 0
        nck = P // TP
        blk = pl.BlockSpec((TP * H, HD), lambda b, c: ((b0 + b) * nck + c, 0))
        in_specs += [blk, blk, pl.BlockSpec((1, H, 1, TP), lambda b, c: (b, 0, 0, c))]
        args += [kc, vc, cc]
    return pl.pallas_call(
        functools.partial(_fox_seq_kernel, cache is not None, HD ** -0.5),
        grid=(B, nck),
        in_specs=in_specs,
        out_specs=pl.BlockSpec((TQ, FD), lambda b, c: (b, 0)),
        out_shape=jax.ShapeDtypeStruct((B * TQ, FD), BF16),
        scratch_shapes=[pltpu.VMEM((H, TQ, 1), F32), pltpu.VMEM((H, TQ, 1), F32), pltpu.VMEM((H, TQ, HD), F32)],
        compiler_params=_cparams(("parallel", "arbitrary")),
        name="fox_seq",
    )(*args)


WKV_NU = 8


def _wkv_intra_kernel(L, r_ref, k_ref, v_ref, wp_ref, ap_ref, kk_ref, ka_ref, rk_ref,
                      x_ref, u0_ref, o0_ref, arb_ref, vk_ref, bh_ref, gl_ref, bonus_ref):
    HD, NH = RWKV_HEAD, STACK
    R = NH * L
    W = NH * HD

    rl = lax.broadcasted_iota(jnp.int32, (L, L), 0)
    cl = lax.broadcasted_iota(jnp.int32, (L, L), 1)
    tril = jnp.where(cl <= rl, 1.0, 0.0).astype(BF16)
    row = lax.broadcasted_iota(jnp.int32, (R, R), 0)
    col = lax.broadcasted_iota(jnp.int32, (R, R), 1)
    diff = row ^ col
    causal = (diff < L) & (col <= row)
    strict = (diff < L) & (col < row)
    eye = jnp.where(row == col, 1.0, 0.0).astype(F32)

    heads = [slice(j * HD, (j + 1) * HD) for j in range(WKV_NU * NH)]
    r = r_ref[...]
    v = v_ref[...]
    k_raw = k_ref[...]
    wp = -wp_ref[...]
    w = -(jnp.maximum(wp, 0.0) + jnp.log(1.0 + jnp.exp(-jnp.abs(wp)))) - 0.5
    lw = -jnp.exp(w)
    a_sig = jax.nn.sigmoid(ap_ref[...])
    kk = k_raw * kk_ref[...]
    kk = jnp.concatenate(
        [kk[:, h] * lax.rsqrt(jnp.sum(kk[:, h] * kk[:, h], axis=-1, keepdims=True) + L2_EPS) for h in heads], axis=1)
    k = k_raw * (1.0 + (a_sig - 1.0) * ka_ref[...])
    a = -kk
    b = kk * a_sig
    rk = r * k * rk_ref[...]
    bonus_ref[...] = jnp.concatenate([jnp.sum(rk[:, h], axis=-1, keepdims=True) * v[:, h] for h in heads], axis=1)

    cum = _cumsum_rows(tril, lw)
    clast = cum[L - 1:L, :]
    e_neg = jnp.exp(-cum)
    e_tail = jnp.exp(clast - cum)
    rt = r * jnp.exp(cum)
    at = a * jnp.exp(cum - lw)
    kt = k * e_neg
    bt = b * e_neg
    khat = k * e_tail
    bhat = b * e_tail
    gl_ref[0] = jnp.exp(clast)

    U = range(WKV_NU)
    sl = lambda x, u: x[:, u * W:(u + 1) * W]
    st = lambda x, u: _stack_heads(sl(x, u), NH, HD)
    v_st = [st(v, u) for u in U]
    aa = [_mm_nt(jnp.concatenate([st(at, u), st(rt, u)], axis=0),
                 jnp.concatenate([st(bt, u), st(kt, u)], axis=0)) for u in U]
    a_ab = [jnp.where(strict, aa[u][:R, :R], 0.0) for u in U]
    a_ak = [jnp.where(strict, aa[u][:R, R:], 0.0) for u in U]
    a_rb = [jnp.where(causal, aa[u][R:, :R], 0.0) for u in U]
    a_rk = [jnp.where(causal, aa[u][R:, R:], 0.0) for u in U]
    av = [_mm(jnp.concatenate([a_ak[u], a_rk[u]], axis=0), v_st[u]) for u in U]
    vk = [_mm_tn(v_st[u], _expand_heads(sl(khat, u), NH, L, HD)) for u in U]
    tinv = _tri_inv([-x for x in a_ab], diff, eye, L)
    w_exp = [_mm(tinv[u], _expand_heads(sl(at, u), NH, L, HD)) for u in U]
    u0 = [_mm(tinv[u], av[u][:R]) for u in U]
    for u in U:
        x_ref[0, u] = jnp.concatenate([w_exp[u], _expand_heads(sl(rt, u), NH, L, HD)], axis=0).astype(BF16)
        u0_ref[0, u] = u0[u]
        o0_ref[0, u] = av[u][R:]
        arb_ref[0, u] = a_rb[u].astype(BF16)
        vk_ref[0, u] = vk[u]
        bh_ref[0, u] = _expand_heads(sl(bhat, u), NH, L, HD).astype(BF16)


def _wkv_inter_kernel(L, HG, x_ref, u0_ref, o0_ref, arb_ref, vk_ref, bh_ref, gl_ref, s0_ref,
                      bonus_ref, gate_ref, gnw_ref, gnb_ref, o_ref, so_ref, s_scr):
    c = pl.program_id(1)
    HD, NH = RWKV_HEAD, STACK
    R = NH * L
    W = NH * HD

    @pl.when(c == 0)
    def _():
        for g in range(HG):
            s_scr[g] = jnp.concatenate([s0_ref[0, NH * g + j] for j in range(NH)], axis=1)

    s_old = [s_scr[g] for g in range(HG)]
    d1 = [_mm_nt(x_ref[0, g], s_old[g]) for g in range(HG)]
    u_st = [u0_ref[0, g] + d1[g][:R] for g in range(HG)]
    o_st = [d1[g][R:] + o0_ref[0, g] + _dot(arb_ref[0, g], u_st[g].astype(BF16)) for g in range(HG)]
    for g in range(HG):
        s_scr[g] = s_old[g] * gl_ref[0, :, g * W:(g + 1) * W] + vk_ref[0, g] + _mm_tn(u_st[g], bh_ref[0, g])
    outs = []
    for g in range(HG):
        mean = jnp.mean(o_st[g], axis=-1, keepdims=True)
        d = o_st[g] - mean
        var = jnp.mean(d * d, axis=-1, keepdims=True)
        on = d * lax.rsqrt(var + RWKV_GN_EPS)
        outs += [on[j * L:(j + 1) * L] for j in range(NH)]
    o = jnp.concatenate(outs, axis=1) * gnw_ref[...] + gnb_ref[...] + bonus_ref[...]
    o_ref[...] = (o * _silu(gate_ref[...])).astype(BF16)

    @pl.when(c == pl.num_programs(1) - 1)
    def _():
        for g in range(HG):
            for j in range(NH):
                so_ref[0, NH * g + j] = s_scr[g][:, j * HD:(j + 1) * HD]


def wkv_mix(big, w_pre, a_pre, k_k, k_a, r_k, gn_w, gn_b, s0, row0, B, NC, L):
    HN = s0.shape[1]
    HD, NH, NU = RWKV_HEAD, STACK, WKV_NU
    HG = HN // NH
    W = NH * HD
    R = NH * L
    D = HN * HD
    Tseg = B * NC * L
    NCH = B * NC
    rb0 = row0 // L
    nsec = D // (NU * W)
    assert row0 % L == 0 and HG % NU == 0
    tok = lambda sec: pl.BlockSpec((L, NU * W), lambda bb, h, c: (rb0 + bb * NC + c, sec * nsec + h))
    par = pl.BlockSpec((1, NU * W), lambda bb, h, c: (0, h))
    unit = lambda w: pl.BlockSpec((1, NU, w[0], w[1]), lambda bb, h, c: (bb * NC + c, h, 0, 0))
    x, u0, o0, arb, vk, bh, gl, bonus = pl.pallas_call(
        functools.partial(_wkv_intra_kernel, L),
        grid=(B, HG // NU, NC),
        in_specs=[tok(0), tok(1), tok(2), tok(0), tok(0), par, par, par],
        out_specs=[unit((2 * R, W)), unit((R, HD)), unit((R, HD)), unit((R, R)), unit((HD, W)), unit((R, W)),
                   pl.BlockSpec((1, 1, NU * W), lambda bb, h, c: (bb * NC + c, 0, h)),
                   pl.BlockSpec((L, NU * W), lambda bb, h, c: (bb * NC + c, h))],
        out_shape=[jax.ShapeDtypeStruct((NCH, HG, 2 * R, W), BF16),
                   jax.ShapeDtypeStruct((NCH, HG, R, HD), F32),
                   jax.ShapeDtypeStruct((NCH, HG, R, HD), F32),
                   jax.ShapeDtypeStruct((NCH, HG, R, R), BF16),
                   jax.ShapeDtypeStruct((NCH, HG, HD, W), F32),
                   jax.ShapeDtypeStruct((NCH, HG, R, W), BF16),
                   jax.ShapeDtypeStruct((NCH, 1, D), F32),
                   jax.ShapeDtypeStruct((Tseg, D), F32)],
        compiler_params=_cparams(("parallel", "parallel", "parallel")),
        name="wkv_intra",
    )(big, big, big, w_pre, a_pre, k_k.reshape(1, D), k_a.reshape(1, D), r_k.reshape(1, D))

    allu = lambda w: pl.BlockSpec((1, HG, w[0], w[1]), lambda bb, c: (bb * NC + c, 0, 0, 0))
    o, s_new = pl.pallas_call(
        functools.partial(_wkv_inter_kernel, L, HG),
        grid=(B, NC),
        in_specs=[allu((2 * R, W)), allu((R, HD)), allu((R, HD)), allu((R, R)), allu((HD, W)), allu((R, W)),
                  pl.BlockSpec((1, 1, D), lambda bb, c: (bb * NC + c, 0, 0)),
                  pl.BlockSpec((1, HN, HD, HD), lambda bb, c: (bb, 0, 0, 0)),
                  pl.BlockSpec((L, D), lambda bb, c: (bb * NC + c, 0)),
                  pl.BlockSpec((L, D), lambda bb, c: (rb0 + bb * NC + c, 3)),
                  pl.BlockSpec((1, D), lambda bb, c: (0, 0)),
                  pl.BlockSpec((1, D), lambda bb, c: (0, 0))],
        out_specs=[pl.BlockSpec((L, D), lambda bb, c: (bb * NC + c, 0)),
                   pl.BlockSpec((1, HN, HD, HD), lambda bb, c: (bb, 0, 0, 0))],
        out_shape=[jax.ShapeDtypeStruct((Tseg, D), BF16),
                   jax.ShapeDtypeStruct((B, HN, HD, HD), F32)],
        scratch_shapes=[pltpu.VMEM((HG, HD, W), F32)],
        compiler_params=_cparams(("parallel", "arbitrary")),
        name="wkv_inter",
    )(x, u0, o0, arb, vk, bh, gl, s0, bonus, big, gn_w.reshape(1, D), gn_b.reshape(1, D))
    return o, s_new


def _pad_cols(w, n):
    return jnp.pad(w, ((0, 0), (0, n - w.shape[1])))


def _gdn_layer(X, seg, g_pre, g_post, w_in, conv_w, a_log, dt_bias, o_norm, w_out, st_S, st_conv):
    (T, BS, TS, NM) = seg
    HV = a_log.shape[0]
    CD = conv_w.shape[1]
    VD = HV * GDN_HEAD
    D = w_in.shape[0]
    NH = STACK
    HG = HV // NH
    w_main = w_in[:, :CD + VD].astype(BF16)
    wb = w_in[:, CD + VD:CD + VD + HV].reshape(D, HG, NH)
    wa = w_in[:, CD + VD + HV:].reshape(D, HG, NH)
    w_gate = jnp.pad(jnp.concatenate([wb, wa], axis=2), ((0, 0), (0, 0), (0, LANES - 2 * NH)))
    decay_lanes = lambda p: jnp.pad(p.reshape(HG, NH), ((0, 0), (NH, LANES - 2 * NH))).reshape(1, HG * LANES)
    a_par, dt_par = decay_lanes(a_log), decay_lanes(dt_bias)
    proj = norm_proj(X, g_pre, jnp.concatenate([w_main, w_gate.reshape(D, HG * LANES).astype(BF16)], axis=1))
    gp = (CD + VD, a_par, dt_par)

    r_s, r_m = T, T + BS * TS
    tail = GDN_CONV - 1
    c_m = proj[r_m + NM - tail:r_m + NM, :CD][None]
    c_p = proj[T - tail:T, :CD][None]
    c_s = proj[r_s:r_m, :CD].reshape(BS, TS, CD)[:, TS - tail:]
    z_conv = jnp.zeros((1, tail, CD), F32)
    z_S = jnp.zeros((1, HV, GDN_HEAD, GDN_HEAD), F32)
    o_m, S_m = gdn_mix(proj, *gp, z_conv, z_S, conv_w, o_norm, r_m, 1, 1, NM)
    LP = 64
    o_p, S_p = gdn_mix(proj, *gp, c_m, S_m, conv_w, o_norm, 0, 1, T // LP, LP)
    o_s, S_s = gdn_mix(proj, *gp, st_conv, st_S, conv_w, o_norm, r_s, BS, 1, TS)
    o = jnp.concatenate([o_p, o_s, o_m], axis=0)
    X = out_proj(o, w_out.astype(BF16), X, g_post)
    return X, (S_p, c_p, S_s, c_s)


def _fox_layer(X, seg, g_pre, g_post, w_in, b_f, w_out, caches, j):
    (T, BS, TS, NM) = seg
    cache_k, cache_v, cache_logf = caches
    H = b_f.shape[0]
    FD = H * FOX_HEAD
    qz = norm_proj(X, g_pre, jnp.concatenate([w_in[:, :FD], w_in[:, 3 * FD:4 * FD]], axis=1).astype(BF16))
    kv, kv16 = norm_proj(X, g_pre, w_in[:, FD:3 * FD].astype(BF16), also_bf16=True)
    gates = norm_proj(X, g_pre, _pad_cols(w_in[:, 4 * FD:], LANES).astype(BF16))
    r_s, r_m = T, T + BS * TS
    P = cache_k.shape[2]

    bias = _pad_cols(b_f.reshape(1, H), LANES)
    zero = jnp.zeros((1, LANES), F32)
    lf_m, cs_m = cumsum_rows(gates, r_m, NM, zero, bias)
    lf_p, cs_p = cumsum_rows(gates, 0, T, cs_m[NM - 1:NM], bias)
    lf_s, _ = cumsum_rows(gates, r_s, BS * TS, zero, bias)
    lanes_to_heads = lambda c, n: (c[:, :H] * LOG2E).T.reshape(H, 1, n)
    c_meta, c_main = lanes_to_heads(cs_m, NM), lanes_to_heads(cs_p, T)
    _, cs_c = cumsum_rows(cache_logf[j].transpose(1, 0, 2).reshape(P, BS * H), 0, P, jnp.zeros((1, BS * H), F32))
    lf_new = lf_s[:, :H].reshape(BS, TS, H).transpose(1, 0, 2).reshape(TS, BS * H)
    _, cs_n = cumsum_rows(lf_new, 0, TS, cs_c[P - 1:P])
    pos_last = lambda c, n: (c * LOG2E).reshape(n, BS, H).transpose(1, 2, 0).reshape(BS, H, 1, n)
    c_cache, c_new = pos_last(cs_c, P), pos_last(cs_n, TS)
    logf = jnp.concatenate([lf_p[:, :H], lf_s[:, :H], lf_m[:, :H]], axis=0)

    o_p = fox_prompt(qz, kv16, c_main, c_meta, T, r_m, NM, H)
    o_m = fox_seq(qz, kv16, c_meta.reshape(1, H, 1, NM), r_m, 1, NM, H)
    o_s = fox_seq(qz, kv16, c_new, r_s, BS, TS, H,
                  cache=(cache_k.reshape(-1, FOX_HEAD), cache_v.reshape(-1, FOX_HEAD), c_cache, j * BS))
    o = jnp.concatenate([o_p, o_s, o_m], axis=0)
    X = out_proj(o, w_out.astype(BF16), X, g_post)

    def seq(a, n):
        return jnp.concatenate([a[r_m:], a[:T]], axis=0).reshape(1, NM + T, H, n)

    k_all = kv[:, :FD]
    v_all = kv[:, FD:]
    outs = (seq(k_all, FOX_HEAD), seq(v_all, FOX_HEAD),
            jnp.concatenate([logf[r_m:], logf[:T]], axis=0).reshape(1, NM + T, H),
            k_all[r_s:r_m].reshape(BS, TS, H, FOX_HEAD), v_all[r_s:r_m].reshape(BS, TS, H, FOX_HEAD),
            logf[r_s:r_m].reshape(BS, TS, H))
    return X, outs


def _rwkv_layer(X, seg, g_pre, g_post, mu, w_r, w_k, w_v, w_g, w0, w1, w2, a0, a1, a2, k_k, k_a, r_k,
                gn_w, gn_b, w_out, st_S, st_shift):
    (T, BS, TS, NM) = seg
    D = X.shape[1]
    HN = r_k.shape[0]
    HD = RWKV_HEAD
    r_s, r_m = T, T + BS * TS
    h, prev = norm_shift(X, g_pre)
    h_s = h[r_s:r_m].reshape(BS, TS, D)
    starts = jnp.concatenate([jnp.zeros((1,), jnp.int32), r_s + TS * jnp.arange(BS, dtype=jnp.int32),
                              jnp.full((1,), r_m, jnp.int32)])
    before = jnp.concatenate([h[r_m + NM - 1:r_m + NM], st_shift, jnp.zeros((1, D), F32)], axis=0)
    prev = prev.at[starts].set(before)

    big = mix_proj(h, prev, mu[jnp.array([0, 2, 3, 5])], jnp.stack([w_r, w_k, w_v, w_g]).astype(BF16))
    lora_w = jnp.stack([_pad_cols(w1, LANES), _pad_cols(a1, LANES)]).astype(BF16)
    lo = mix_proj(h, prev, mu[jnp.array([1, 4])], lora_w)
    pad_rows = lambda w: jnp.pad(w, ((0, LANES - w.shape[0]), (0, 0))).astype(BF16)
    w_pre = lora_out(lo, 0, pad_rows(w2), w0, True)
    a_pre = lora_out(lo, 1, pad_rows(a2), a0, False)

    zS = jnp.zeros((1, HN, HD, HD), F32)
    par = (k_k, k_a, r_k, gn_w, gn_b)
    o_m, S_m = wkv_mix(big, w_pre, a_pre, *par, zS, r_m, 1, 1, NM)
    LP = 64
    o_p, S_p = wkv_mix(big, w_pre, a_pre, *par, S_m, 0, 1, T // LP, LP)
    o_s, S_s = wkv_mix(big, w_pre, a_pre, *par, st_S, r_s, BS, 1, TS)
    o = jnp.concatenate([o_p, o_s, o_m], axis=0)
    X = out_proj(o, w_out.astype(BF16), X, g_post)
    return X, (S_p, h[T - 1:T], S_s, h_s[:, -1])


def kernel(x_prompt, x_sample, state_gdn_S, state_gdn_conv, cache_fox_k, cache_fox_v, cache_fox_logf, state_rwkv_S, state_rwkv_shift, meta, norm_pre, norm_post, gdn_w_in, gdn_conv_w, gdn_a_log, gdn_dt_bias, gdn_o_norm, gdn_w_out, fox_w_in, fox_b_f, fox_w_out, rwkv_mu, rwkv_w_r, rwkv_w_k, rwkv_w_v, rwkv_w_g, rwkv_w0, rwkv_w1, rwkv_w2, rwkv_a0, rwkv_a1, rwkv_a2, rwkv_k_k, rwkv_k_a, rwkv_r_k, rwkv_gn_w, rwkv_gn_b, rwkv_w_out):
    _, T, D = x_prompt.shape
    BS, TS, _ = x_sample.shape
    NM = meta.shape[0]
    depth = norm_pre.shape[0]
    assert x_prompt.shape[0] == 1
    seg = (T, BS, TS, NM)
    X = jnp.concatenate([x_prompt[0], x_sample.reshape(BS * TS, D), meta.astype(x_prompt.dtype)], axis=0)

    gdn_out, fox_out, rwkv_out = [], [], []
    for i in range(depth):
        kind, j = i % 3, i // 3
        if kind == 0:
            X, st = _gdn_layer(X, seg, norm_pre[i], norm_post[i], gdn_w_in[j], gdn_conv_w[j], gdn_a_log[j],
                               gdn_dt_bias[j], gdn_o_norm[j], gdn_w_out[j], state_gdn_S[j], state_gdn_conv[j])
            gdn_out.append(st)
        elif kind == 1:
            X, st = _fox_layer(X, seg, norm_pre[i], norm_post[i], fox_w_in[j], fox_b_f[j], fox_w_out[j],
                               (cache_fox_k, cache_fox_v, cache_fox_logf), j)
            fox_out.append(st)
        else:
            X, st = _rwkv_layer(X, seg, norm_pre[i], norm_post[i], rwkv_mu[j], rwkv_w_r[j], rwkv_w_k[j],
                                rwkv_w_v[j], rwkv_w_g[j], rwkv_w0[j], rwkv_w1[j], rwkv_w2[j], rwkv_a0[j],
                                rwkv_a1[j], rwkv_a2[j], rwkv_k_k[j], rwkv_k_a[j], rwkv_r_k[j], rwkv_gn_w[j],
                                rwkv_gn_b[j], rwkv_w_out[j], state_rwkv_S[j], state_rwkv_shift[j])
            rwkv_out.append(st)

    stack = lambda items, n: jnp.stack([it[n] for it in items])
    y_prompt = X[:T][None]
    y_sample = X[T:T + BS * TS].reshape(BS, TS, D)
    return (y_prompt, y_sample,
            stack(gdn_out, 0), stack(gdn_out, 1),
            stack(fox_out, 0), stack(fox_out, 1), stack(fox_out, 2),
            stack(rwkv_out, 0), stack(rwkv_out, 1),
            stack(gdn_out, 2), stack(gdn_out, 3),
            stack(fox_out, 3), stack(fox_out, 4), stack(fox_out, 5),
            stack(rwkv_out, 2), stack(rwkv_out, 3))
```

```python
import functools

import jax
import jax.numpy as jnp
from jax import lax
from jax.experimental import pallas as pl
from jax.experimental.pallas import tpu as pltpu

F32 = jnp.float32
BF16 = jnp.bfloat16

NORM_EPS = 1e-6
L2_EPS = 1e-6
RWKV_GN_EPS = 64e-5
GDN_HEAD = 128
GDN_CONV = 4
FOX_HEAD = 128
RWKV_HEAD = 64
LANES = 128
SUBLANES = 8
CONV_PAD = 8
INV_BASE = 16
STACK = 4
VMEM_LIMIT = 48 * 1024 * 1024
VMEM_LIMIT_BIG = 56 * 1024 * 1024


def _cparams(sem, limit=VMEM_LIMIT):
    return pltpu.CompilerParams(dimension_semantics=sem, vmem_limit_bytes=limit)


def _log2(n):
    assert n & (n - 1) == 0
    return n.bit_length() - 1


def _dot(a, b):
    return jnp.dot(a, b, preferred_element_type=F32)


def _mm(a, b):
    return _dot(a.astype(BF16), b.astype(BF16))


def _mm_nt(a, b):
    return lax.dot_general(a.astype(BF16), b.astype(BF16), (((1,), (1,)), ((), ())),
                           preferred_element_type=F32)


def _mm_tn(a, b):
    return lax.dot_general(a.astype(BF16), b.astype(BF16), (((0,), (0,)), ((), ())),
                           preferred_element_type=F32)


def _split3(a):
    hi = a.astype(BF16)
    r = a - hi.astype(F32)
    mid = r.astype(BF16)
    lo = (r - mid.astype(F32)).astype(BF16)
    return hi, mid, lo


def _cumsum_rows(tril, g):
    hi, mid, lo = _split3(g)
    return _dot(tril, hi) + (_dot(tril, mid) + _dot(tril, lo))


def _cumsum_cols_t(g, triu):
    hi, mid, lo = _split3(g)
    return _mm_tn(hi, triu) + (_mm_tn(mid, triu) + _mm_tn(lo, triu))


def _tri_inv(a_list, diff, eye, L):
    base = diff < INV_BASE
    n1 = [jnp.where(base, -a, 0.0) for a in a_list]
    p = [eye + n for n in n1]
    npow = n1
    steps = 1
    while 2 * steps < INV_BASE:
        npow = [_mm(x, x) for x in npow]
        p = [pp + _mm(pp, x) for pp, x in zip(p, npow)]
        steps *= 2
    blk = INV_BASE
    while blk < L:
        off_diag = (diff < 2 * blk) & (diff >= blk)
        e = [jnp.where(off_diag, a, 0.0) for a in a_list]
        pe = [_mm(pp, ee) for pp, ee in zip(p, e)]
        p = [pp - _mm(x, pp) for pp, x in zip(p, pe)]
        blk *= 2
    return p


def _stack_heads(x, n, w):
    return jnp.concatenate([x[:, j * w:(j + 1) * w] for j in range(n)], axis=0)


def _expand_heads(x, n, L, w):
    t = jnp.concatenate([x] * n, axis=0)
    rh = lax.shift_right_logical(lax.broadcasted_iota(jnp.int32, t.shape, 0), _log2(L))
    ch = lax.shift_right_logical(lax.broadcasted_iota(jnp.int32, t.shape, 1), _log2(w))
    return jnp.where(rh == ch, t, 0.0)


def _silu(x):
    return x * jax.nn.sigmoid(x)


NORM_PROJ_ROWS = 1600
ROW_TILE_CAP = 544


def _row_tile(T, cap, exact=False):
    best = 0
    for t in range(16, cap + 1, 16):
        if T % t == 0:
            best = t
    assert best > 0
    return best if exact or best >= cap // 4 else (cap // 16) * 16


def _norm_proj_kernel(x_ref, g_ref, w_ref, o_ref, *rest):
    h_scr = rest[-1]

    @pl.when(pl.program_id(1) == 0)
    def _():
        x = x_ref[...]
        ms = jnp.mean(x * x, axis=-1, keepdims=True)
        h_scr[...] = (x * lax.rsqrt(ms + NORM_EPS) * g_ref[...]).astype(BF16)

    y = _dot(h_scr[...], w_ref[...])
    o_ref[...] = y
    if len(rest) == 2:
        rest[0][...] = y.astype(BF16)


def norm_proj(x, g, w, tn=512, also_bf16=False):
    T, D = x.shape
    N = w.shape[1]
    tn = min(tn, N)
    tm = _row_tile(T, NORM_PROJ_ROWS)
    assert N % tn == 0
    out_spec = pl.BlockSpec((tm, tn), lambda i, j: (i, j))
    out_specs, out_shape = out_spec, jax.ShapeDtypeStruct((T, N), F32)
    if also_bf16:
        out_specs, out_shape = [out_spec, out_spec], [out_shape, jax.ShapeDtypeStruct((T, N), BF16)]
    return pl.pallas_call(
        _norm_proj_kernel,
        grid=(pl.cdiv(T, tm), N // tn),
        in_specs=[pl.BlockSpec((tm, D), lambda i, j: (i, 0)),
                  pl.BlockSpec((1, D), lambda i, j: (0, 0)),
                  pl.BlockSpec((D, tn), lambda i, j: (0, j))],
        out_specs=out_specs,
        out_shape=out_shape,
        scratch_shapes=[pltpu.VMEM((tm, D), BF16)],
        compiler_params=_cparams(("parallel", "arbitrary"), VMEM_LIMIT_BIG),
        name="norm_proj",
    )(x, g.reshape(1, D), w)


def _norm_shift_kernel(x_ref, g_ref, h_ref, p_ref, last_scr):
    tm = x_ref.shape[0]

    @pl.when(pl.program_id(0) == 0)
    def _():
        last_scr[...] = jnp.zeros_like(last_scr)

    x = x_ref[...]
    ms = jnp.mean(x * x, axis=-1, keepdims=True)
    h = x * lax.rsqrt(ms + NORM_EPS) * g_ref[...]
    h_ref[...] = h
    p_ref[0:1, :] = last_scr[...]
    p_ref[1:tm, :] = h[0:tm - 1, :]
    last_scr[...] = h[tm - 1:tm, :]


def norm_shift(x, g):
    T, D = x.shape
    tm = _row_tile(T, ROW_TILE_CAP, exact=True)
    spec = pl.BlockSpec((tm, D), lambda i: (i, 0))
    return pl.pallas_call(
        _norm_shift_kernel,
        grid=(T // tm,),
        in_specs=[spec, pl.BlockSpec((1, D), lambda i: (0, 0))],
        out_specs=[spec, spec],
        out_shape=[jax.ShapeDtypeStruct((T, D), F32), jax.ShapeDtypeStruct((T, D), F32)],
        scratch_shapes=[pltpu.VMEM((1, D), F32)],
        compiler_params=_cparams(("arbitrary",)),
        name="norm_shift",
    )(x, g.reshape(1, D))


CUMSUM_ROWS = 512


def _cumsum_kernel(log_sigmoid, x_ref, b_ref, init_ref, f_ref, c_ref, carry):
    tp = x_ref.shape[0]

    @pl.when(pl.program_id(0) == 0)
    def _():
        carry[...] = init_ref[...]

    x = x_ref[...]
    if log_sigmoid:
        x = -(x + b_ref[...])
        x = -(jnp.maximum(x, 0.0) + jnp.log(1.0 + jnp.exp(-jnp.abs(x))))
    f_ref[...] = x
    r = lax.broadcasted_iota(jnp.int32, (tp, tp), 0)
    c = lax.broadcasted_iota(jnp.int32, (tp, tp), 1)
    y = _cumsum_rows(jnp.where(c <= r, 1.0, 0.0).astype(BF16), x) + carry[...]
    carry[...] = y[tp - 1:tp, :]
    c_ref[...] = y


def cumsum_rows(x, row0, n, init, bias=None):
    C = x.shape[1]
    tp = min(CUMSUM_ROWS, n)
    assert n % tp == 0 and row0 % tp == 0
    rb0 = row0 // tp
    b = jnp.zeros((1, C), F32) if bias is None else bias
    return pl.pallas_call(
        functools.partial(_cumsum_kernel, bias is not None),
        grid=(n // tp,),
        in_specs=[pl.BlockSpec((tp, C), lambda i: (rb0 + i, 0)),
                  pl.BlockSpec((1, C), lambda i: (0, 0)),
                  pl.BlockSpec((1, C), lambda i: (0, 0))],
        out_specs=[pl.BlockSpec((tp, C), lambda i: (i, 0)), pl.BlockSpec((tp, C), lambda i: (i, 0))],
        out_shape=[jax.ShapeDtypeStruct((n, C), F32), jax.ShapeDtypeStruct((n, C), F32)],
        scratch_shapes=[pltpu.VMEM((1, C), F32)],
        compiler_params=_cparams(("arbitrary",)),
        name="cumsum_rows",
    )(x, b, init)


def _mix_proj_kernel(h_ref, p_ref, mu_ref, w_ref, o_ref, l_scr):
    @pl.when(pl.program_id(2) == 0)
    def _():
        h = h_ref[...]
        l_scr[...] = (h + (p_ref[...] - h) * mu_ref[0]).astype(BF16)

    o_ref[...] = _dot(l_scr[...], w_ref[0])


def mix_proj(h, prev, mu, w, tn=1024):
    T, D = h.shape
    tm = _row_tile(T, ROW_TILE_CAP)
    G, _, N = w.shape
    tn = min(tn, N)
    nj = N // tn
    return pl.pallas_call(
        _mix_proj_kernel,
        grid=(pl.cdiv(T, tm), G, nj),
        in_specs=[pl.BlockSpec((tm, D), lambda i, g, j: (i, 0)),
                  pl.BlockSpec((tm, D), lambda i, g, j: (i, 0)),
                  pl.BlockSpec((1, 1, D), lambda i, g, j: (g, 0, 0)),
                  pl.BlockSpec((1, D, tn), lambda i, g, j: (g, 0, j))],
        out_specs=pl.BlockSpec((tm, tn), lambda i, g, j: (i, g * nj + j)),
        out_shape=jax.ShapeDtypeStruct((T, G * N), F32),
        scratch_shapes=[pltpu.VMEM((tm, D), BF16)],
        compiler_params=_cparams(("parallel", "arbitrary", "arbitrary")),
        name="mix_proj",
    )(h, prev, mu.reshape(G, 1, D), w)


def _lora_kernel(use_tanh, x_ref, w_ref, b_ref, o_ref):
    x = x_ref[...]
    if use_tanh:
        x = jnp.tanh(x)
    o_ref[...] = b_ref[...] + _dot(x.astype(BF16), w_ref[...])


def lora_out(x, col_block, w, b, use_tanh):
    T = x.shape[0]
    tm = _row_tile(T, ROW_TILE_CAP)
    K, N = w.shape
    return pl.pallas_call(
        functools.partial(_lora_kernel, use_tanh),
        grid=(pl.cdiv(T, tm),),
        in_specs=[pl.BlockSpec((tm, K), lambda i: (i, col_block)),
                  pl.BlockSpec((K, N), lambda i: (0, 0)),
                  pl.BlockSpec((1, N), lambda i: (0, 0))],
        out_specs=pl.BlockSpec((tm, N), lambda i: (i, 0)),
        out_shape=jax.ShapeDtypeStruct((T, N), F32),
        compiler_params=_cparams(("parallel",)),
        name="lora_out",
    )(x, w, b.reshape(1, N))


def _out_proj_kernel(tn, a_ref, w_ref, x_ref, g_ref, o_ref, y_scr):
    j = pl.program_id(1)
    off = pl.multiple_of(j * tn, tn)
    y_scr[:, pl.ds(off, tn)] = _dot(a_ref[...], w_ref[...])

    @pl.when(j == pl.num_programs(1) - 1)
    def _():
        y = y_scr[...]
        ms = jnp.mean(y * y, axis=-1, keepdims=True)
        o_ref[...] = x_ref[...] + y * lax.rsqrt(ms + NORM_EPS) * g_ref[...]


def out_proj(a, w, x, g, tn=512):
    T, K = a.shape
    tm = _row_tile(T, ROW_TILE_CAP)
    D = w.shape[1]
    return pl.pallas_call(
        functools.partial(_out_proj_kernel, tn),
        grid=(pl.cdiv(T, tm), D // tn),
        in_specs=[pl.BlockSpec((tm, K), lambda i, j: (i, 0)),
                  pl.BlockSpec((K, tn), lambda i, j: (0, j)),
                  pl.BlockSpec((tm, D), lambda i, j: (i, 0)),
                  pl.BlockSpec((1, D), lambda i, j: (0, 0))],
        out_specs=pl.BlockSpec((tm, D), lambda i, j: (i, 0)),
        out_shape=jax.ShapeDtypeStruct((T, D), F32),
        scratch_shapes=[pltpu.VMEM((tm, D), F32)],
        compiler_params=_cparams(("parallel", "arbitrary")),
        name="out_proj",
    )(a, w, x, g.reshape(1, D))


GDN_NU = 8


def _gdn_intra_kernel(L, qn_ref, kn_ref, vn_ref, qp_ref, kp_ref, vp_ref, cq_ref, ck_ref, cv_ref,
                      wq_ref, wk_ref, wv_ref, gt_ref, al_ref, dt_ref,
                      uv_ref, wqe_ref, kt_ref, qkd_ref, egl_ref, qbuf, kbuf, vbuf):
    c = pl.program_id(2)
    HD, NH = GDN_HEAD, STACK
    R = NH * L
    first = c == 0

    def conv_silu(buf, u_ref, p_ref, st_ref, w_ref):
        buf[CONV_PAD - 3:CONV_PAD, :] = jnp.where(first, st_ref[0], p_ref[SUBLANES - 3:SUBLANES, :])
        buf[CONV_PAD:CONV_PAD + L, :] = u_ref[...]
        acc = buf[CONV_PAD - 3:CONV_PAD - 3 + L, :] * w_ref[0:1, :]
        acc = acc + buf[CONV_PAD - 2:CONV_PAD - 2 + L, :] * w_ref[1:2, :]
        acc = acc + buf[CONV_PAD - 1:CONV_PAD - 1 + L, :] * w_ref[2:3, :]
        acc = acc + buf[CONV_PAD:CONV_PAD + L, :] * w_ref[3:4, :]
        return _silu(acc)

    q = conv_silu(qbuf, qn_ref, qp_ref, cq_ref, wq_ref)
    k = conv_silu(kbuf, kn_ref, kp_ref, ck_ref, wk_ref)
    v = conv_silu(vbuf, vn_ref, vp_ref, cv_ref, wv_ref)

    rl = lax.broadcasted_iota(jnp.int32, (L, L), 0)
    cl = lax.broadcasted_iota(jnp.int32, (L, L), 1)
    tril = jnp.where(cl <= rl, 1.0, 0.0).astype(BF16)
    triu = jnp.where(rl <= cl, 1.0, 0.0).astype(BF16)
    row = lax.broadcasted_iota(jnp.int32, (R, R), 0)
    col = lax.broadcasted_iota(jnp.int32, (R, R), 1)
    diff = row ^ col
    causal = (diff < L) & (col <= row)
    strict = (diff < L) & (col < row)
    eye = jnp.where(row == col, 1.0, 0.0).astype(F32)

    def l2n(x):
        return x * lax.rsqrt(jnp.sum(x * x, axis=-1, keepdims=True) + L2_EPS)

    def col_stack(x, first_lane):
        return jnp.concatenate([x[:, first_lane + j:first_lane + j + 1] for j in range(NH)], axis=0)

    kst, qst, vst, gst, bst, glast_st, grow_st, glast = [], [], [], [], [], [], [], []
    for u in range(GDN_NU):
        qh = [l2n(q[:, (2 * u + i) * HD:(2 * u + i + 1) * HD]) * (HD ** -0.5) for i in range(2)]
        kh = [l2n(k[:, (2 * u + i) * HD:(2 * u + i + 1) * HD]) for i in range(2)]
        kst.append(jnp.concatenate([kh[0], kh[0], kh[1], kh[1]], axis=0))
        qst.append(jnp.concatenate([qh[0], qh[0], qh[1], qh[1]], axis=0))
        vst.append(_stack_heads(v[:, u * NH * HD:(u + 1) * NH * HD], NH, HD))
        lanes = slice(u * LANES, (u + 1) * LANES)
        gu = gt_ref[:, lanes]
        beta = jax.nn.sigmoid(gu)
        sp = gu + dt_ref[:, lanes]
        sp = jnp.maximum(sp, 0.0) + jnp.log(1.0 + jnp.exp(-jnp.abs(sp)))
        g = -jnp.exp(al_ref[:, lanes]) * sp
        gcol = _cumsum_rows(tril, g)
        grow = _cumsum_cols_t(g, triu)
        gl = gcol[L - 1:L, :]
        glast.append(gl)
        gst.append(col_stack(gcol, NH))
        bst.append(col_stack(beta, 0))
        glast_st.append(jnp.concatenate([jnp.broadcast_to(gl[:, NH + j:NH + j + 1], (L, 1)) for j in range(NH)],
                                        axis=0))
        grow_st.append(jnp.concatenate([grow[NH + j:NH + j + 1, :] for j in range(NH)], axis=1))

    U = range(GDN_NU)
    kk = [_mm_nt(kst[u], kst[u]) for u in U]
    qk = [_mm_nt(qst[u], kst[u]) for u in U]
    decay = [jnp.where(causal, jnp.exp(jnp.where(causal, gst[u] - grow_st[u], 0.0)), 0.0) for u in U]
    a = [jnp.where(strict, kk[u] * decay[u] * bst[u], 0.0) for u in U]
    tinv = _tri_inv(a, diff, eye, L)
    rhs = [jnp.concatenate([vst[u] * bst[u], kst[u] * (bst[u] * jnp.exp(gst[u]))], axis=1) for u in U]
    sol = [_mm(tinv[u], rhs[u]) for u in U]
    for u in U:
        uv_ref[0, u] = sol[u][:, :HD]
        wk = sol[u][:, HD:]
        qe = qst[u] * jnp.exp(gst[u])
        wqe_ref[0, u] = jnp.concatenate(
            [x[j * L:(j + 1) * L] for j in range(NH) for x in (wk, qe)], axis=0).astype(BF16)
        kt_ref[0, u] = (kst[u] * jnp.exp(glast_st[u] - gst[u])).astype(BF16)
        qkd_ref[0, u] = (qk[u] * decay[u]).astype(BF16)
        egl_ref[0, u] = jnp.exp(glast[u])


def _gdn_inter_kernel(L, HG, uv_ref, wqe_ref, kt_ref, qkd_ref, egl_ref, z_ref, s0_ref, on_ref,
                      o_ref, so_ref, s_scr):
    c = pl.program_id(1)
    HD, NH = GDN_HEAD, STACK
    R = NH * L

    @pl.when(c == 0)
    def _():
        for g in range(HG):
            s_scr[g] = jnp.concatenate([s0_ref[0, NH * g + j] for j in range(NH)], axis=1)

    rh = lax.shift_right_logical(lax.broadcasted_iota(jnp.int32, (R, NH * HD), 0), _log2(L))
    ch = lax.shift_right_logical(lax.broadcasted_iota(jnp.int32, (R, NH * HD), 1), _log2(HD))
    own = rh == ch

    s_old = [s_scr[g] for g in range(HG)]
    d1 = [[_dot(wqe_ref[0, g, 2 * L * j:2 * L * (j + 1), :], s_old[g][:, j * HD:(j + 1) * HD].astype(BF16))
           for j in range(NH)] for g in range(HG)]
    v_new = [uv_ref[0, g] - jnp.concatenate([d1[g][j][:L] for j in range(NH)], axis=0) for g in range(HG)]
    o_st = [jnp.concatenate([d1[g][j][L:] for j in range(NH)], axis=0) + _dot(qkd_ref[0, g], v_new[g].astype(BF16))
            for g in range(HG)]
    for g in range(HG):
        v_exp = jnp.where(own, jnp.concatenate([v_new[g]] * NH, axis=1), 0.0)
        egl = egl_ref[0, g]
        e_exp = jnp.concatenate([jnp.broadcast_to(egl[:, NH + j:NH + j + 1], (1, HD)) for j in range(NH)], axis=1)
        s_scr[g] = s_old[g] * e_exp + _mm_tn(kt_ref[0, g], v_exp)

    outs = []
    for g in range(HG):
        for j in range(NH):
            o = o_st[g][j * L:(j + 1) * L]
            zj = z_ref[:, (NH * g + j) * HD:(NH * g + j + 1) * HD]
            ms = jnp.mean(o * o, axis=-1, keepdims=True)
            outs.append((o * lax.rsqrt(ms + NORM_EPS) * on_ref[...] * _silu(zj)).astype(BF16))
    o_ref[...] = jnp.concatenate(outs, axis=1)

    @pl.when(c == pl.num_programs(1) - 1)
    def _():
        for g in range(HG):
            for j in range(NH):
                so_ref[0, NH * g + j] = s_scr[g][:, j * HD:(j + 1) * HD]


def gdn_mix(proj, gate_col, a_par, dt_par, conv_state, s0, conv_w, o_norm, row0, B, NC, L):
    HV = s0.shape[1]
    HD, NH, NU = GDN_HEAD, STACK, GDN_NU
    HG = HV // NH
    QW = NU * (NH // 2) * HD
    VW = NU * NH * HD
    nq = (HV // 2) * HD // QW
    nv = HV * HD // VW
    Tseg = B * NC * L
    R = NH * L
    rb0 = row0 // L
    assert row0 % L == 0 and L % SUBLANES == 0 and HG % NU == 0 and gate_col % (NU * LANES) == 0
    gate_blk = gate_col // (NU * LANES)

    cq = conv_state[:, :, :nq * QW]
    ck = conv_state[:, :, nq * QW:2 * nq * QW]
    cv = conv_state[:, :, 2 * nq * QW:]
    wq = conv_w[:, :nq * QW]
    wk = conv_w[:, nq * QW:2 * nq * QW]
    wv = conv_w[:, 2 * nq * QW:]

    chunk = lambda b, h, c: rb0 + b * NC + c
    before = lambda b, h, c: jnp.maximum(chunk(b, h, c) * (L // SUBLANES) - 1, 0)
    intra_in = [
        pl.BlockSpec((L, QW), lambda b, h, c: (chunk(b, h, c), h)),
        pl.BlockSpec((L, QW), lambda b, h, c: (chunk(b, h, c), nq + h)),
        pl.BlockSpec((L, VW), lambda b, h, c: (chunk(b, h, c), nv + h)),
        pl.BlockSpec((SUBLANES, QW), lambda b, h, c: (before(b, h, c), h)),
        pl.BlockSpec((SUBLANES, QW), lambda b, h, c: (before(b, h, c), nq + h)),
        pl.BlockSpec((SUBLANES, VW), lambda b, h, c: (before(b, h, c), nv + h)),
        pl.BlockSpec((1, GDN_CONV - 1, QW), lambda b, h, c: (b, 0, h)),
        pl.BlockSpec((1, GDN_CONV - 1, QW), lambda b, h, c: (b, 0, h)),
        pl.BlockSpec((1, GDN_CONV - 1, VW), lambda b, h, c: (b, 0, h)),
        pl.BlockSpec((GDN_CONV, QW), lambda b, h, c: (0, h)),
        pl.BlockSpec((GDN_CONV, QW), lambda b, h, c: (0, h)),
        pl.BlockSpec((GDN_CONV, VW), lambda b, h, c: (0, h)),
        pl.BlockSpec((L, NU * LANES), lambda b, h, c: (chunk(b, h, c), gate_blk + h)),
        pl.BlockSpec((1, NU * LANES), lambda b, h, c: (0, h)),
        pl.BlockSpec((1, NU * LANES), lambda b, h, c: (0, h)),
    ]
    unit = lambda w: pl.BlockSpec((1, NU, w[0], w[1]), lambda b, h, c: (b * NC + c, h, 0, 0))
    NCH = B * NC
    uv, wqe, kt, qkd, egl = pl.pallas_call(
        functools.partial(_gdn_intra_kernel, L),
        grid=(B, HG // NU, NC),
        in_specs=intra_in,
        out_specs=[unit((R, HD)), unit((2 * R, HD)), unit((R, HD)), unit((R, R)), unit((1, LANES))],
        out_shape=[jax.ShapeDtypeStruct((NCH, HG, R, HD), F32),
                   jax.ShapeDtypeStruct((NCH, HG, 2 * R, HD), BF16),
                   jax.ShapeDtypeStruct((NCH, HG, R, HD), BF16),
                   jax.ShapeDtypeStruct((NCH, HG, R, R), BF16),
                   jax.ShapeDtypeStruct((NCH, HG, 1, LANES), F32)],
        scratch_shapes=[pltpu.VMEM((CONV_PAD + L, QW), F32),
                        pltpu.VMEM((CONV_PAD + L, QW), F32),
                        pltpu.VMEM((CONV_PAD + L, VW), F32)],
        compiler_params=_cparams(("parallel", "parallel", "parallel")),
        name="gdn_intra",
    )(proj, proj, proj, proj, proj, proj, cq, ck, cv, wq, wk, wv, proj, a_par, dt_par)

    allu = lambda w: pl.BlockSpec((1, HG, w[0], w[1]), lambda b, c: (b * NC + c, 0, 0, 0))
    zblk = 2 * nv * VW // (HV * HD)
    o, s_new = pl.pallas_call(
        functools.partial(_gdn_inter_kernel, L, HG),
        grid=(B, NC),
        in_specs=[allu((R, HD)), allu((2 * R, HD)), allu((R, HD)), allu((R, R)), allu((1, LANES)),
                  pl.BlockSpec((L, HV * HD), lambda b, c: (rb0 + b * NC + c, zblk)),
                  pl.BlockSpec((1, HV, HD, HD), lambda b, c: (b, 0, 0, 0)),
                  pl.BlockSpec((1, HD), lambda b, c: (0, 0))],
        out_specs=[pl.BlockSpec((L, HV * HD), lambda b, c: (b * NC + c, 0)),
                   pl.BlockSpec((1, HV, HD, HD), lambda b, c: (b, 0, 0, 0))],
        out_shape=[jax.ShapeDtypeStruct((Tseg, HV * HD), BF16),
                   jax.ShapeDtypeStruct((B, HV, HD, HD), F32)],
        scratch_shapes=[pltpu.VMEM((HG, HD, NH * HD), F32)],
        compiler_params=_cparams(("parallel", "arbitrary")),
        name="gdn_inter",
    )(uv, wqe, kt, qkd, egl, proj, s0, o_norm.reshape(1, HD))
    return o, s_new


FOX_NH = 2
FOX_TP = 512
LOG2E = 1.4426950408889634


def _fox_prompt_kernel(TQ, scale, q_ref, k_ref, v_ref, km_ref, vm_ref, z_ref, ck_ref, cm_ref, o_ref,
                       sa_scr, sb_scr):
    qi = pl.program_id(1)
    HD = FOX_HEAD
    HS = range(FOX_NH)
    hs = lambda i: slice(i * HD, (i + 1) * HD)
    q = [(q_ref[:, hs(i)] * (scale * LOG2E)).astype(BF16) for i in HS]

    s = [_mm_nt(q[i], km_ref[:, hs(i)]) - cm_ref[i] for i in HS]
    m = [jnp.max(s[i], axis=-1, keepdims=True) for i in HS]
    p = [jnp.exp2(s[i] - m[i]) for i in HS]
    l = [jnp.sum(p[i], axis=-1, keepdims=True) for i in HS]
    acc = [_mm(p[i], vm_ref[:, hs(i)]) for i in HS]

    def scores_into(scr, kb):
        off = pl.multiple_of(kb * TQ, TQ)
        for i in HS:
            scr[i] = _mm_nt(q[i], k_ref[pl.ds(off, TQ), hs(i)]) - ck_ref[i, :, pl.ds(off, TQ)]

    def fold(scr, kb, m, l, acc, diagonal):
        off = pl.multiple_of(kb * TQ, TQ)
        s = [scr[i] for i in HS]
        if diagonal:
            row = lax.broadcasted_iota(jnp.int32, (TQ, TQ), 0)
            col = lax.broadcasted_iota(jnp.int32, (TQ, TQ), 1)
            s = [jnp.where(col <= row, s[i], -jnp.inf) for i in HS]
        m_new = [jnp.maximum(m[i], jnp.max(s[i], axis=-1, keepdims=True)) for i in HS]
        alpha = [jnp.exp2(m[i] - m_new[i]) for i in HS]
        p = [jnp.exp2(s[i] - m_new[i]) for i in HS]
        l = [alpha[i] * l[i] + jnp.sum(p[i], axis=-1, keepdims=True) for i in HS]
        acc = [alpha[i] * acc[i] + _mm(p[i], v_ref[pl.ds(off, TQ), hs(i)]) for i in HS]
        return tuple(m_new), tuple(l), tuple(acc)

    scores_into(sa_scr, 0)

    def pair(j, carry):
        scores_into(sb_scr, 2 * j + 1)
        carry = fold(sa_scr, 2 * j, *carry, False)
        scores_into(sa_scr, 2 * j + 2)
        return fold(sb_scr, 2 * j + 1, *carry, False)

    carry = lax.fori_loop(0, qi // 2, pair, (tuple(m), tuple(l), tuple(acc)))

    def last_even(*carry):
        return fold(sa_scr, qi, *carry, True)

    def last_odd(*carry):
        scores_into(sb_scr, qi)
        carry = fold(sa_scr, qi - 1, *carry, False)
        return fold(sb_scr, qi, *carry, True)

    m, l, acc = lax.cond(qi % 2 == 0, last_even, last_odd, *carry)
    o_ref[...] = jnp.concatenate([(acc[i] / l[i]) * _silu(z_ref[:, hs(i)]) for i in HS], axis=1).astype(BF16)


def fox_prompt(qz, kv16, c_main, c_meta, T, meta_row0, n_meta, H, TQ=512):
    HD = FOX_HEAD
    W = FOX_NH * HD
    HP = H // FOX_NH
    TQ = min(TQ, T)
    assert T % TQ == 0 and meta_row0 % n_meta == 0 and H % FOX_NH == 0
    mb = meta_row0 // n_meta
    return pl.pallas_call(
        functools.partial(_fox_prompt_kernel, TQ, HD ** -0.5),
        grid=(HP, T // TQ),
        in_specs=[pl.BlockSpec((TQ, W), lambda h, i: (i, h)),
                  pl.BlockSpec((T, W), lambda h, i: (0, h)),
                  pl.BlockSpec((T, W), lambda h, i: (0, HP + h)),
                  pl.BlockSpec((n_meta, W), lambda h, i: (mb, h)),
                  pl.BlockSpec((n_meta, W), lambda h, i: (mb, HP + h)),
                  pl.BlockSpec((TQ, W), lambda h, i: (i, HP + h)),
                  pl.BlockSpec((FOX_NH, 1, T), lambda h, i: (h, 0, 0)),
                  pl.BlockSpec((FOX_NH, 1, n_meta), lambda h, i: (h, 0, 0))],
        out_specs=pl.BlockSpec((TQ, W), lambda h, i: (i, h)),
        out_shape=jax.ShapeDtypeStruct((T, H * HD), BF16),
        scratch_shapes=[pltpu.VMEM((FOX_NH, TQ, TQ), F32), pltpu.VMEM((FOX_NH, TQ, TQ), F32)],
        compiler_params=_cparams(("parallel", "arbitrary"), VMEM_LIMIT_BIG),
        name="fox_prompt",
    )(qz, kv16, kv16, kv16, kv16, qz, c_main, c_meta)


def _fox_seq_kernel(has_cache, scale, q_ref, k_ref, v_ref, z_ref, cn_ref, *rest):
    if has_cache:
        kc_ref, vc_ref, cc_ref, o_ref, m_scr, l_scr, a_scr = rest
    else:
        o_ref, m_scr, l_scr, a_scr = rest
    c = pl.program_id(1)
    TQ = q_ref.shape[0]
    HD = FOX_HEAD
    H = q_ref.shape[1] // HD
    HS = range(H)
    hs = lambda h: slice(h * HD, (h + 1) * HD)
    q = [(q_ref[:, hs(h)] * (scale * LOG2E)).astype(BF16) for h in HS]

    @pl.when(c == 0)
    def _():
        row = lax.broadcasted_iota(jnp.int32, (TQ, TQ), 0)
        col = lax.broadcasted_iota(jnp.int32, (TQ, TQ), 1)
        s = [jnp.where(col <= row, _mm_nt(q[h], k_ref[:, hs(h)]) - cn_ref[0, h], -jnp.inf) for h in HS]
        m = [jnp.max(s[h], axis=-1, keepdims=True) for h in HS]
        p = [jnp.exp2(s[h] - m[h]) for h in HS]
        for h in HS:
            m_scr[h] = m[h]
            l_scr[h] = jnp.sum(p[h], axis=-1, keepdims=True)
            a_scr[h] = _mm(p[h], v_ref[:, hs(h)])

    if has_cache:
        TP = cc_ref.shape[-1]
        kc_all = pltpu.einshape("(ph)d->hpd", kc_ref[...].astype(BF16), h=H)
        vc_all = pltpu.einshape("(ph)d->hpd", vc_ref[...].astype(BF16), h=H)
        s = [_mm_nt(q[h], kc_all[h]) - cc_ref[0, h] for h in HS]
        m_old = [m_scr[h] for h in HS]
        l_old = [l_scr[h] for h in HS]
        a_old = [a_scr[h] for h in HS]
        m_new = [jnp.maximum(m_old[h], jnp.max(s[h], axis=-1, keepdims=True)) for h in HS]
        alpha = [jnp.exp2(m_old[h] - m_new[h]) for h in HS]
        p = [jnp.exp2(s[h] - m_new[h]) for h in HS]
        pv = [_mm(p[h], vc_all[h]) for h in HS]
        for h in HS:
            m_scr[h] = m_new[h]
            l_scr[h] = alpha[h] * l_old[h] + jnp.sum(p[h], axis=-1, keepdims=True)
            a_scr[h] = alpha[h] * a_old[h] + pv[h]

    @pl.when(c == pl.num_programs(1) - 1)
    def _():
        o_ref[...] = jnp.concatenate([(a_scr[h] / l_scr[h]) * _silu(z_ref[:, hs(h)]) for h in HS],
                                     axis=1).astype(BF16)


def fox_seq(qz, kv16, c_new, row0, B, TQ, H, cache=None):
    HD = FOX_HEAD
    FD = H * HD
    assert row0 % TQ == 0
    rb0 = row0 // TQ
    tok = lambda sec: pl.BlockSpec((TQ, FD), lambda b, c: (rb0 + b, sec))
    in_specs = [tok(0), tok(0), tok(1), tok(1), pl.BlockSpec((1, H, 1, TQ), lambda b, c: (b, 0, 0, 0))]
    args = [qz, kv16, kv16, qz, c_new]
    nck = 1
    if cache is not None:
        kc, vc, cc, b0 = cache
        P = cc.shape[-1]
        TP = min(FOX_TP, P)
        assert P % TP == 0
        nck = P // TP
        blk = pl.BlockSpec((TP * H, HD), lambda b, c: ((b0 + b) * nck + c, 0))
        in_specs += [blk, blk, pl.BlockSpec((1, H, 1, TP), lambda b, c: (b, 0, 0, c))]
        args += [kc, vc, cc]
    return pl.pallas_call(
        functools.partial(_fox_seq_kernel, cache is not None, HD ** -0.5),
        grid=(B, nck),
        in_specs=in_specs,
        out_specs=pl.BlockSpec((TQ, FD), lambda b, c: (b, 0)),
        out_shape=jax.ShapeDtypeStruct((B * TQ, FD), BF16),
        scratch_shapes=[pltpu.VMEM((H, TQ, 1), F32), pltpu.VMEM((H, TQ, 1), F32), pltpu.VMEM((H, TQ, HD), F32)],
        compiler_params=_cparams(("parallel", "arbitrary")),
        name="fox_seq",
    )(*args)


WKV_NU = 8


def _wkv_intra_kernel(L, r_ref, k_ref, v_ref, wp_ref, ap_ref, kk_ref, ka_ref, rk_ref,
                      x_ref, u0_ref, o0_ref, arb_ref, vk_ref, bh_ref, gl_ref, bonus_ref):
    HD, NH = RWKV_HEAD, STACK
    R = NH * L
    W = NH * HD

    rl = lax.broadcasted_iota(jnp.int32, (L, L), 0)
    cl = lax.broadcasted_iota(jnp.int32, (L, L), 1)
    tril = jnp.where(cl <= rl, 1.0, 0.0).astype(BF16)
    row = lax.broadcasted_iota(jnp.int32, (R, R), 0)
    col = lax.broadcasted_iota(jnp.int32, (R, R), 1)
    diff = row ^ col
    causal = (diff < L) & (col <= row)
    strict = (diff < L) & (col < row)
    eye = jnp.where(row == col, 1.0, 0.0).astype(F32)

    heads = [slice(j * HD, (j + 1) * HD) for j in range(WKV_NU * NH)]
    r = r_ref[...]
    v = v_ref[...]
    k_raw = k_ref[...]
    wp = -wp_ref[...]
    w = -(jnp.maximum(wp, 0.0) + jnp.log(1.0 + jnp.exp(-jnp.abs(wp)))) - 0.5
    lw = -jnp.exp(w)
    a_sig = jax.nn.sigmoid(ap_ref[...])
    kk = k_raw * kk_ref[...]
    kk = jnp.concatenate(
        [kk[:, h] * lax.rsqrt(jnp.sum(kk[:, h] * kk[:, h], axis=-1, keepdims=True) + L2_EPS) for h in heads], axis=1)
    k = k_raw * (1.0 + (a_sig - 1.0) * ka_ref[...])
    a = -kk
    b = kk * a_sig
    rk = r * k * rk_ref[...]
    bonus_ref[...] = jnp.concatenate([jnp.sum(rk[:, h], axis=-1, keepdims=True) * v[:, h] for h in heads], axis=1)

    cum = _cumsum_rows(tril, lw)
    clast = cum[L - 1:L, :]
    e_neg = jnp.exp(-cum)
    e_tail = jnp.exp(clast - cum)
    rt = r * jnp.exp(cum)
    at = a * jnp.exp(cum - lw)
    kt = k * e_neg
    bt = b * e_neg
    khat = k * e_tail
    bhat = b * e_tail
    gl_ref[0] = jnp.exp(clast)

    U = range(WKV_NU)
    sl = lambda x, u: x[:, u * W:(u + 1) * W]
    st = lambda x, u: _stack_heads(sl(x, u), NH, HD)
    v_st = [st(v, u) for u in U]
    aa = [_mm_nt(jnp.concatenate([st(at, u), st(rt, u)], axis=0),
                 jnp.concatenate([st(bt, u), st(kt, u)], axis=0)) for u in U]
    a_ab = [jnp.where(strict, aa[u][:R, :R], 0.0) for u in U]
    a_ak = [jnp.where(strict, aa[u][:R, R:], 0.0) for u in U]
    a_rb = [jnp.where(causal, aa[u][R:, :R], 0.0) for u in U]
    a_rk = [jnp.where(causal, aa[u][R:, R:], 0.0) for u in U]
    av = [_mm(jnp.concatenate([a_ak[u], a_rk[u]], axis=0), v_st[u]) for u in U]
    vk = [_mm_tn(v_st[u], _expand_heads(sl(khat, u), NH, L, HD)) for u in U]
    tinv = _tri_inv([-x for x in a_ab], diff, eye, L)
    w_exp = [_mm(tinv[u], _expand_heads(sl(at, u), NH, L, HD)) for u in U]
    u0 = [_mm(tinv[u], av[u][:R]) for u in U]
    for u in U:
        x_ref[0, u] = jnp.concatenate([w_exp[u], _expand_heads(sl(rt, u), NH, L, HD)], axis=0).astype(BF16)
        u0_ref[0, u] = u0[u]
        o0_ref[0, u] = av[u][R:]
        arb_ref[0, u] = a_rb[u].astype(BF16)
        vk_ref[0, u] = vk[u]
        bh_ref[0, u] = _expand_heads(sl(bhat, u), NH, L, HD).astype(BF16)


def _wkv_inter_kernel(L, HG, x_ref, u0_ref, o0_ref, arb_ref, vk_ref, bh_ref, gl_ref, s0_ref,
                      bonus_ref, gate_ref, gnw_ref, gnb_ref, o_ref, so_ref, s_scr):
    c = pl.program_id(1)
    HD, NH = RWKV_HEAD, STACK
    R = NH * L
    W = NH * HD

    @pl.when(c == 0)
    def _():
        for g in range(HG):
            s_scr[g] = jnp.concatenate([s0_ref[0, NH * g + j] for j in range(NH)], axis=1)

    s_old = [s_scr[g] for g in range(HG)]
    d1 = [_mm_nt(x_ref[0, g], s_old[g]) for g in range(HG)]
    u_st = [u0_ref[0, g] + d1[g][:R] for g in range(HG)]
    o_st = [d1[g][R:] + o0_ref[0, g] + _dot(arb_ref[0, g], u_st[g].astype(BF16)) for g in range(HG)]
    for g in range(HG):
        s_scr[g] = s_old[g] * gl_ref[0, :, g * W:(g + 1) * W] + vk_ref[0, g] + _mm_tn(u_st[g], bh_ref[0, g])
    outs = []
    for g in range(HG):
        mean = jnp.mean(o_st[g], axis=-1, keepdims=True)
        d = o_st[g] - mean
        var = jnp.mean(d * d, axis=-1, keepdims=True)
        on = d * lax.rsqrt(var + RWKV_GN_EPS)
        outs += [on[j * L:(j + 1) * L] for j in range(NH)]
    o = jnp.concatenate(outs, axis=1) * gnw_ref[...] + gnb_ref[...] + bonus_ref[...]
    o_ref[...] = (o * _silu(gate_ref[...])).astype(BF16)

    @pl.when(c == pl.num_programs(1) - 1)
    def _():
        for g in range(HG):
            for j in range(NH):
                so_ref[0, NH * g + j] = s_scr[g][:, j * HD:(j + 1) * HD]


def wkv_mix(big, w_pre, a_pre, k_k, k_a, r_k, gn_w, gn_b, s0, row0, B, NC, L):
    HN = s0.shape[1]
    HD, NH, NU = RWKV_HEAD, STACK, WKV_NU
    HG = HN // NH
    W = NH * HD
    R = NH * L
    D = HN * HD
    Tseg = B * NC * L
    NCH = B * NC
    rb0 = row0 // L
    nsec = D // (NU * W)
    assert row0 % L == 0 and HG % NU == 0
    tok = lambda sec: pl.BlockSpec((L, NU * W), lambda bb, h, c: (rb0 + bb * NC + c, sec * nsec + h))
    par = pl.BlockSpec((1, NU * W), lambda bb, h, c: (0, h))
    unit = lambda w: pl.BlockSpec((1, NU, w[0], w[1]), lambda bb, h, c: (bb * NC + c, h, 0, 0))
    x, u0, o0, arb, vk, bh, gl, bonus = pl.pallas_call(
        functools.partial(_wkv_intra_kernel, L),
        grid=(B, HG // NU, NC),
        in_specs=[tok(0), tok(1), tok(2), tok(0), tok(0), par, par, par],
        out_specs=[unit((2 * R, W)), unit((R, HD)), unit((R, HD)), unit((R, R)), unit((HD, W)), unit((R, W)),
                   pl.BlockSpec((1, 1, NU * W), lambda bb, h, c: (bb * NC + c, 0, h)),
                   pl.BlockSpec((L, NU * W), lambda bb, h, c: (bb * NC + c, h))],
        out_shape=[jax.ShapeDtypeStruct((NCH, HG, 2 * R, W), BF16),
                   jax.ShapeDtypeStruct((NCH, HG, R, HD), F32),
                   jax.ShapeDtypeStruct((NCH, HG, R, HD), F32),
                   jax.ShapeDtypeStruct((NCH, HG, R, R), BF16),
                   jax.ShapeDtypeStruct((NCH, HG, HD, W), F32),
                   jax.ShapeDtypeStruct((NCH, HG, R, W), BF16),
                   jax.ShapeDtypeStruct((NCH, 1, D), F32),
                   jax.ShapeDtypeStruct((Tseg, D), F32)],
        compiler_params=_cparams(("parallel", "parallel", "parallel")),
        name="wkv_intra",
    )(big, big, big, w_pre, a_pre, k_k.reshape(1, D), k_a.reshape(1, D), r_k.reshape(1, D))

    allu = lambda w: pl.BlockSpec((1, HG, w[0], w[1]), lambda bb, c: (bb * NC + c, 0, 0, 0))
    o, s_new = pl.pallas_call(
        functools.partial(_wkv_inter_kernel, L, HG),
        grid=(B, NC),
        in_specs=[allu((2 * R, W)), allu((R, HD)), allu((R, HD)), allu((R, R)), allu((HD, W)), allu((R, W)),
                  pl.BlockSpec((1, 1, D), lambda bb, c: (bb * NC + c, 0, 0)),
                  pl.BlockSpec((1, HN, HD, HD), lambda bb, c: (bb, 0, 0, 0)),
                  pl.BlockSpec((L, D), lambda bb, c: (bb * NC + c, 0)),
                  pl.BlockSpec((L, D), lambda bb, c: (rb0 + bb * NC + c, 3)),
                  pl.BlockSpec((1, D), lambda bb, c: (0, 0)),
                  pl.BlockSpec((1, D), lambda bb, c: (0, 0))],
        out_specs=[pl.BlockSpec((L, D), lambda bb, c: (bb * NC + c, 0)),
                   pl.BlockSpec((1, HN, HD, HD), lambda bb, c: (bb, 0, 0, 0))],
        out_shape=[jax.ShapeDtypeStruct((Tseg, D), BF16),
                   jax.ShapeDtypeStruct((B, HN, HD, HD), F32)],
        scratch_shapes=[pltpu.VMEM((HG, HD, W), F32)],
        compiler_params=_cparams(("parallel", "arbitrary")),
        name="wkv_inter",
    )(x, u0, o0, arb, vk, bh, gl, s0, bonus, big, gn_w.reshape(1, D), gn_b.reshape(1, D))
    return o, s_new


def _pad_cols(w, n):
    return jnp.pad(w, ((0, 0), (0, n - w.shape[1])))


def _gdn_layer(X, seg, g_pre, g_post, w_in, conv_w, a_log, dt_bias, o_norm, w_out, st_S, st_conv):
    (T, BS, TS, NM) = seg
    HV = a_log.shape[0]
    CD = conv_w.shape[1]
    VD = HV * GDN_HEAD
    D = w_in.shape[0]
    NH = STACK
    HG = HV // NH
    w_main = w_in[:, :CD + VD].astype(BF16)
    wb = w_in[:, CD + VD:CD + VD + HV].reshape(D, HG, NH)
    wa = w_in[:, CD + VD + HV:].reshape(D, HG, NH)
    w_gate = jnp.pad(jnp.concatenate([wb, wa], axis=2), ((0, 0), (0, 0), (0, LANES - 2 * NH)))
    decay_lanes = lambda p: jnp.pad(p.reshape(HG, NH), ((0, 0), (NH, LANES - 2 * NH))).reshape(1, HG * LANES)
    a_par, dt_par = decay_lanes(a_log), decay_lanes(dt_bias)
    proj = norm_proj(X, g_pre, jnp.concatenate([w_main, w_gate.reshape(D, HG * LANES).astype(BF16)], axis=1))
    gp = (CD + VD, a_par, dt_par)

    r_s, r_m = T, T + BS * TS
    tail = GDN_CONV - 1
    c_m = proj[r_m + NM - tail:r_m + NM, :CD][None]
    c_p = proj[T - tail:T, :CD][None]
    c_s = proj[r_s:r_m, :CD].reshape(BS, TS, CD)[:, TS - tail:]
    z_conv = jnp.zeros((1, tail, CD), F32)
    z_S = jnp.zeros((1, HV, GDN_HEAD, GDN_HEAD), F32)
    o_m, S_m = gdn_mix(proj, *gp, z_conv, z_S, conv_w, o_norm, r_m, 1, 1, NM)
    LP = 64
    o_p, S_p = gdn_mix(proj, *gp, c_m, S_m, conv_w, o_norm, 0, 1, T // LP, LP)
    o_s, S_s = gdn_mix(proj, *gp, st_conv, st_S, conv_w, o_norm, r_s, BS, 1, TS)
    o = jnp.concatenate([o_p, o_s, o_m], axis=0)
    X = out_proj(o, w_out.astype(BF16), X, g_post)
    return X, (S_p, c_p, S_s, c_s)


def _fox_layer(X, seg, g_pre, g_post, w_in, b_f, w_out, caches, j):
    (T, BS, TS, NM) = seg
    cache_k, cache_v, cache_logf = caches
    H = b_f.shape[0]
    FD = H * FOX_HEAD
    qz = norm_proj(X, g_pre, jnp.concatenate([w_in[:, :FD], w_in[:, 3 * FD:4 * FD]], axis=1).astype(BF16))
    kv, kv16 = norm_proj(X, g_pre, w_in[:, FD:3 * FD].astype(BF16), also_bf16=True)
    gates = norm_proj(X, g_pre, _pad_cols(w_in[:, 4 * FD:], LANES).astype(BF16))
    r_s, r_m = T, T + BS * TS
    P = cache_k.shape[2]

    bias = _pad_cols(b_f.reshape(1, H), LANES)
    zero = jnp.zeros((1, LANES), F32)
    lf_m, cs_m = cumsum_rows(gates, r_m, NM, zero, bias)
    lf_p, cs_p = cumsum_rows(gates, 0, T, cs_m[NM - 1:NM], bias)
    lf_s, _ = cumsum_rows(gates, r_s, BS * TS, zero, bias)
    lanes_to_heads = lambda c, n: (c[:, :H] * LOG2E).T.reshape(H, 1, n)
    c_meta, c_main = lanes_to_heads(cs_m, NM), lanes_to_heads(cs_p, T)
    _, cs_c = cumsum_rows(cache_logf[j].transpose(1, 0, 2).reshape(P, BS * H), 0, P, jnp.zeros((1, BS * H), F32))
    lf_new = lf_s[:, :H].reshape(BS, TS, H).transpose(1, 0, 2).reshape(TS, BS * H)
    _, cs_n = cumsum_rows(lf_new, 0, TS, cs_c[P - 1:P])
    pos_last = lambda c, n: (c * LOG2E).reshape(n, BS, H).transpose(1, 2, 0).reshape(BS, H, 1, n)
    c_cache, c_new = pos_last(cs_c, P), pos_last(cs_n, TS)
    logf = jnp.concatenate([lf_p[:, :H], lf_s[:, :H], lf_m[:, :H]], axis=0)

    o_p = fox_prompt(qz, kv16, c_main, c_meta, T, r_m, NM, H)
    o_m = fox_seq(qz, kv16, c_meta.reshape(1, H, 1, NM), r_m, 1, NM, H)
    o_s = fox_seq(qz, kv16, c_new, r_s, BS, TS, H,
                  cache=(cache_k.reshape(-1, FOX_HEAD), cache_v.reshape(-1, FOX_HEAD), c_cache, j * BS))
    o = jnp.concatenate([o_p, o_s, o_m], axis=0)
    X = out_proj(o, w_out.astype(BF16), X, g_post)

    def seq(a, n):
        return jnp.concatenate([a[r_m:], a[:T]], axis=0).reshape(1, NM + T, H, n)

    k_all = kv[:, :FD]
    v_all = kv[:, FD:]
    outs = (seq(k_all, FOX_HEAD), seq(v_all, FOX_HEAD),
            jnp.concatenate([logf[r_m:], logf[:T]], axis=0).reshape(1, NM + T, H),
            k_all[r_s:r_m].reshape(BS, TS, H, FOX_HEAD), v_all[r_s:r_m].reshape(BS, TS, H, FOX_HEAD),
            logf[r_s:r_m].reshape(BS, TS, H))
    return X, outs


def _rwkv_layer(X, seg, g_pre, g_post, mu, w_r, w_k, w_v, w_g, w0, w1, w2, a0, a1, a2, k_k, k_a, r_k,
                gn_w, gn_b, w_out, st_S, st_shift):
    (T, BS, TS, NM) = seg
    D = X.shape[1]
    HN = r_k.shape[0]
    HD = RWKV_HEAD
    r_s, r_m = T, T + BS * TS
    h, prev = norm_shift(X, g_pre)
    h_s = h[r_s:r_m].reshape(BS, TS, D)
    starts = jnp.concatenate([jnp.zeros((1,), jnp.int32), r_s + TS * jnp.arange(BS, dtype=jnp.int32),
                              jnp.full((1,), r_m, jnp.int32)])
    before = jnp.concatenate([h[r_m + NM - 1:r_m + NM], st_shift, jnp.zeros((1, D), F32)], axis=0)
    prev = prev.at[starts].set(before)

    big = mix_proj(h, prev, mu[jnp.array([0, 2, 3, 5])], jnp.stack([w_r, w_k, w_v, w_g]).astype(BF16))
    lora_w = jnp.stack([_pad_cols(w1, LANES), _pad_cols(a1, LANES)]).astype(BF16)
    lo = mix_proj(h, prev, mu[jnp.array([1, 4])], lora_w)
    pad_rows = lambda w: jnp.pad(w, ((0, LANES - w.shape[0]), (0, 0))).astype(BF16)
    w_pre = lora_out(lo, 0, pad_rows(w2), w0, True)
    a_pre = lora_out(lo, 1, pad_rows(a2), a0, False)

    zS = jnp.zeros((1, HN, HD, HD), F32)
    par = (k_k, k_a, r_k, gn_w, gn_b)
    o_m, S_m = wkv_mix(big, w_pre, a_pre, *par, zS, r_m, 1, 1, NM)
    LP = 64
    o_p, S_p = wkv_mix(big, w_pre, a_pre, *par, S_m, 0, 1, T // LP, LP)
    o_s, S_s = wkv_mix(big, w_pre, a_pre, *par, st_S, r_s, BS, 1, TS)
    o = jnp.concatenate([o_p, o_s, o_m], axis=0)
    X = out_proj(o, w_out.astype(BF16), X, g_post)
    return X, (S_p, h[T - 1:T], S_s, h_s[:, -1])


def kernel(x_prompt, x_sample, state_gdn_S, state_gdn_conv, cache_fox_k, cache_fox_v, cache_fox_logf, state_rwkv_S, state_rwkv_shift, meta, norm_pre, norm_post, gdn_w_in, gdn_conv_w, gdn_a_log, gdn_dt_bias, gdn_o_norm, gdn_w_out, fox_w_in, fox_b_f, fox_w_out, rwkv_mu, rwkv_w_r, rwkv_w_k, rwkv_w_v, rwkv_w_g, rwkv_w0, rwkv_w1, rwkv_w2, rwkv_a0, rwkv_a1, rwkv_a2, rwkv_k_k, rwkv_k_a, rwkv_r_k, rwkv_gn_w, rwkv_gn_b, rwkv_w_out):
    _, T, D = x_prompt.shape
    BS, TS, _ = x_sample.shape
    NM = meta.shape[0]
    depth = norm_pre.shape[0]
    assert x_prompt.shape[0] == 1
    seg = (T, BS, TS, NM)
    X = jnp.concatenate([x_prompt[0], x_sample.reshape(BS * TS, D), meta.astype(x_prompt.dtype)], axis=0)

    gdn_out, fox_out, rwkv_out = [], [], []
    for i in range(depth):
        kind, j = i % 3, i // 3
        if kind == 0:
            X, st = _gdn_layer(X, seg, norm_pre[i], norm_post[i], gdn_w_in[j], gdn_conv_w[j], gdn_a_log[j],
                               gdn_dt_bias[j], gdn_o_norm[j], gdn_w_out[j], state_gdn_S[j], state_gdn_conv[j])
            gdn_out.append(st)
        elif kind == 1:
            X, st = _fox_layer(X, seg, norm_pre[i], norm_post[i], fox_w_in[j], fox_b_f[j], fox_w_out[j],
                               (cache_fox_k, cache_fox_v, cache_fox_logf), j)
            fox_out.append(st)
        else:
            X, st = _rwkv_layer(X, seg, norm_pre[i], norm_post[i], rwkv_mu[j], rwkv_w_r[j], rwkv_w_k[j],
                                rwkv_w_v[j], rwkv_w_g[j], rwkv_w0[j], rwkv_w1[j], rwkv_w2[j], rwkv_a0[j],
                                rwkv_a1[j], rwkv_a2[j], rwkv_k_k[j], rwkv_k_a[j], rwkv_r_k[j], rwkv_gn_w[j],
                                rwkv_gn_b[j], rwkv_w_out[j], state_rwkv_S[j], state_rwkv_shift[j])
            rwkv_out.append(st)

    stack = lambda items, n: jnp.stack([it[n] for it in items])
    y_prompt = X[:T][None]
    y_sample = X[T:T + BS * TS].reshape(BS, TS, D)
    return (y_prompt, y_sample,
            stack(gdn_out, 0), stack(gdn_out, 1),
            stack(fox_out, 0), stack(fox_out, 1), stack(fox_out, 2),
            stack(rwkv_out, 0), stack(rwkv_out, 1),
            stack(gdn_out, 2), stack(gdn_out, 3),
            stack(fox_out, 3), stack(fox_out, 4), stack(fox_out, 5),
            stack(rwkv_out, 2), stack(rwkv_out, 3))
```
